```python
import math
import jax, jax.numpy as jnp
from jax import lax
import numpy as np

D_MODEL = 1024
BATCH = 8
SEQ = 4096
DEPTH = 2

HEAD_DIM = 64
A_HEADS_PER_GROUP = 16
DIL_PAIRS = ((128, 1), (512, 4), (2048, 16))
N_DIL_GROUPS = len(DIL_PAIRS)
DIL_BLOCK = 128
B_HEADS = D_MODEL // HEAD_DIM
Q_BLOCK = 128
NUM_BUCKETS = 32
MAX_DISTANCE = 2048
N_EXPERTS = 32
TOP_K = 4
D_FF = D_MODEL
SWIGLU_LIMIT = 7.0
SWIGLU_ALPHA = 1.702
MOE_BLOCK = 256
RMS_EPS = 1e-6
NEG = -1e30

kernel_name = 'yoco_dilated_fox_moe_block'


def rms_norm(x, g):
    xf = x.astype(jnp.float32)
    y = xf * lax.rsqrt(jnp.mean(xf * xf, axis=-1, keepdims=True) + RMS_EPS)
    return (y * g.astype(jnp.float32)).astype(x.dtype)


def modulate(h, shift, scale):
    return h * (1.0 + scale[:, None, :]) + shift[:, None, :]


def t5_bucket(n):
    max_exact = NUM_BUCKETS // 2
    nf = jnp.maximum(n, 1).astype(jnp.float32)
    large = max_exact + (jnp.log(nf / max_exact) / math.log(MAX_DISTANCE / max_exact)
                         * (NUM_BUCKETS - max_exact)).astype(jnp.int32)
    large = jnp.minimum(large, NUM_BUCKETS - 1)
    return jnp.where(n < max_exact, n, large)


def dilated_group_attention(q, k, v, bias_tab, dilation, window):
    Bq, S, H, dh = q.shape
    steps = window // dilation
    L = -(-S // dilation)
    nb = -(-L // DIL_BLOCK)
    Lp = nb * DIL_BLOCK
    Sp = Lp * dilation

    def to_blocks(t):
        t = jnp.pad(t, ((0, 0), (0, Sp - S), (0, 0), (0, 0)))
        t = t.reshape(Bq, Lp, dilation, H, dh).transpose(0, 2, 3, 1, 4)
        return t.reshape(Bq, dilation, H, nb, DIL_BLOCK, dh)

    def with_prev(t):
        prev = jnp.pad(t[:, :, :, :-1], ((0, 0), (0, 0), (0, 0), (1, 0), (0, 0), (0, 0)))
        return jnp.concatenate([prev, t], axis=4)

    qb = to_blocks(q)
    kk = with_prev(to_blocks(k))
    vv = with_prev(to_blocks(v))

    qi = jnp.arange(DIL_BLOCK, dtype=jnp.int32)[:, None]
    kj = jnp.arange(2 * DIL_BLOCK, dtype=jnp.int32)[None, :]
    delta = qi + DIL_BLOCK - kj
    blk = jnp.arange(nb, dtype=jnp.int32)[:, None, None]
    valid = (delta >= 0) & (delta <= steps) & (blk * DIL_BLOCK + kj - DIL_BLOCK >= 0)
    bucket = t5_bucket(jnp.clip(delta, 0, None) * dilation)
    bias = bias_tab[bucket].astype(jnp.float32).transpose(2, 0, 1)

    s = jnp.einsum('brhnqd,brhnkd->brhnqk', qb, kk).astype(jnp.float32) * (HEAD_DIM ** -0.5)
    s = jnp.where(valid, s + bias[:, None], NEG)
    m = jnp.max(s, axis=-1, keepdims=True)
    p = jnp.exp(s - m)
    den = jnp.sum(p, axis=-1, keepdims=True)
    o = jnp.einsum('brhnqk,brhnkd->brhnqd', (p / den).astype(v.dtype), vv)
    lse = (m + jnp.log(den))[..., 0]

    o = o.reshape(Bq, dilation, H, Lp, dh).transpose(0, 3, 1, 2, 4).reshape(Bq, Sp, H, dh)[:, :S]
    lse = lse.reshape(Bq, dilation, H, Lp).transpose(0, 3, 1, 2).reshape(Bq, Sp, H)[:, :S]
    return o, lse


def dilated_mixer(h, w_qkv, w_o, rel_bias):
    Bq, S, _ = h.shape
    qkv = (h @ w_qkv).reshape(Bq, S, N_DIL_GROUPS, 3, A_HEADS_PER_GROUP, HEAD_DIM)
    outs, lses = [], []
    for g, (window, dilation) in enumerate(DIL_PAIRS):
        tab = rel_bias[:, g * A_HEADS_PER_GROUP:(g + 1) * A_HEADS_PER_GROUP]
        o, lse = dilated_group_attention(qkv[:, :, g, 0], qkv[:, :, g, 1], qkv[:, :, g, 2],
                                         tab, dilation, window)
        outs.append(o)
        lses.append(lse)
    wts = jax.nn.softmax(jnp.stack(lses, axis=0), axis=0)
    o = jnp.sum(wts[..., None] * jnp.stack(outs, axis=0).astype(jnp.float32), axis=0)
    o = o.astype(h.dtype).reshape(Bq, S, A_HEADS_PER_GROUP * HEAD_DIM)
    return o @ w_o


def shared_kv(x, kv_norm_g, w_kvf, b_f):
    Bq, S, _ = x.shape
    z = rms_norm(x, kv_norm_g) @ w_kvf
    nkv = B_HEADS * HEAD_DIM
    k = z[..., :nkv].reshape(Bq, S, B_HEADS, HEAD_DIM)
    v = z[..., nkv:2 * nkv].reshape(Bq, S, B_HEADS, HEAD_DIM)
    log_f = jax.nn.log_sigmoid(z[..., 2 * nkv:].astype(jnp.float32) + b_f.astype(jnp.float32))
    fcum = jnp.cumsum(log_f, axis=1)
    return k, v, fcum


def forgetting_attention(q, k, v, fcum):
    Bq, S, H, dh = q.shape
    nb = S // Q_BLOCK
    qb = q.reshape(Bq, nb, Q_BLOCK, H, dh).transpose(1, 0, 3, 2, 4)
    fq = fcum.reshape(Bq, nb, Q_BLOCK, H).transpose(1, 0, 3, 2)
    kh = k.transpose(0, 2, 1, 3)
    vh = v.transpose(0, 2, 1, 3)
    fk = fcum.transpose(0, 2, 1)
    kpos = jnp.arange(S, dtype=jnp.int32)

    def block(args):
        q_i, f_i, i = args
        s = jnp.einsum('bhqd,bhkd->bhqk', q_i, kh).astype(jnp.float32) * (dh ** -0.5)
        s = s + f_i[..., None] - fk[:, :, None, :]
        qpos = i * Q_BLOCK + jnp.arange(Q_BLOCK, dtype=jnp.int32)
        s = jnp.where(kpos[None, :] <= qpos[:, None], s, NEG)
        p = jax.nn.softmax(s, axis=-1)
        return jnp.einsum('bhqk,bhkd->bhqd', p.astype(vh.dtype), vh)

    o = lax.map(block, (qb, fq, jnp.arange(nb, dtype=jnp.int32)))
    return o.transpose(1, 0, 3, 2, 4).reshape(Bq, S, H, dh)


def forgetting_mixer(h, w_q, w_o, k, v, fcum):
    Bq, S, _ = h.shape
    q = (h @ w_q).reshape(Bq, S, B_HEADS, HEAD_DIM)
    o = forgetting_attention(q, k, v, fcum).reshape(Bq, S, B_HEADS * HEAD_DIM)
    return o @ w_o


def moe_ffn(h, w_router, b_router, w_gu, b_gu, w_down, b_down):
    Bq, S, D = h.shape
    T = Bq * S
    n_slots = T * TOP_K
    xf = h.reshape(T, D)
    logits = (xf @ w_router).astype(jnp.float32) + b_router.astype(jnp.float32)
    top_val, top_idx = lax.top_k(logits, TOP_K)
    gates = jax.nn.softmax(top_val, axis=-1)

    e_flat = top_idx.reshape(-1).astype(jnp.int32)
    tok_flat = jnp.arange(n_slots, dtype=jnp.int32) // TOP_K
    order = jnp.argsort(e_flat)
    e_sorted = e_flat[order]
    tok_sorted = tok_flat[order]
    gate_sorted = gates.reshape(-1)[order]

    counts = jnp.bincount(e_flat, length=N_EXPERTS)
    starts = jnp.cumsum(counts) - counts
    padded = (counts + MOE_BLOCK - 1) // MOE_BLOCK * MOE_BLOCK
    pad_ends = jnp.cumsum(padded)
    pad_starts = pad_ends - padded
    dest = pad_starts[e_sorted] + (jnp.arange(n_slots, dtype=jnp.int32) - starts[e_sorted])

    n_rows = n_slots + N_EXPERTS * MOE_BLOCK
    n_blocks = n_rows // MOE_BLOCK
    xbuf = jnp.zeros((n_rows, D), h.dtype).at[dest].set(xf[tok_sorted])
    block_start = jnp.arange(n_blocks, dtype=jnp.int32) * MOE_BLOCK
    block_expert = jnp.minimum(jnp.searchsorted(pad_ends, block_start, side='right'), N_EXPERTS - 1)

    def expert_block(args):
        xb, e = args
        gu = xb @ w_gu[e] + b_gu[e]
        g, u = jnp.split(gu, 2, axis=-1)
        g = jnp.minimum(g, SWIGLU_LIMIT)
        u = jnp.clip(u, -SWIGLU_LIMIT, SWIGLU_LIMIT)
        act = (u + 1.0) * g * jax.nn.sigmoid(SWIGLU_ALPHA * g)
        return act @ w_down[e] + b_down[e]

    ybuf = lax.map(expert_block, (xbuf.reshape(n_blocks, MOE_BLOCK, D), block_expert))
    y_slots = ybuf.reshape(n_rows, D)[dest] * gate_sorted[:, None].astype(h.dtype)
    out = jnp.zeros((T, D), h.dtype).at[tok_sorted].add(y_slots)
    return out.reshape(Bq, S, D)


def setup_inputs(seed: int = 0) -> dict:
    key = jax.random.key(seed)
    ks = jax.random.split(key, 24)
    D = D_MODEL
    n_a = DEPTH // 2
    n_b = DEPTH - n_a

    def nrm(k, shape, scale):
        return jax.random.normal(k, shape, jnp.float32) * scale

    a_width = A_HEADS_PER_GROUP * HEAD_DIM
    b_width = B_HEADS * HEAD_DIM
    return {
        'x': nrm(ks[0], (BATCH, SEQ, D), 1.0),
        'c': nrm(ks[1], (BATCH, D), 1.0),
        'ada_w': nrm(ks[2], (DEPTH, D, 6 * D), 0.5 * D ** -0.5),
        'ada_b': nrm(ks[3], (DEPTH, 6 * D), 0.02),
        'norm_mix_g': 1.0 + nrm(ks[4], (DEPTH, D), 0.02),
        'norm_ffn_g': 1.0 + nrm(ks[5], (DEPTH, D), 0.02),
        'a_w_qkv': nrm(ks[6], (n_a, D, N_DIL_GROUPS * 3 * a_width), D ** -0.5),
        'a_w_o': nrm(ks[7], (n_a, a_width, D), a_width ** -0.5),
        'rel_bias': nrm(ks[8], (NUM_BUCKETS, N_DIL_GROUPS * A_HEADS_PER_GROUP), 0.5),
        'kv_norm_g': 1.0 + nrm(ks[9], (D,), 0.02),
        'w_kvf': nrm(ks[10], (D, 2 * b_width + B_HEADS), D ** -0.5),
        'b_f': jax.random.uniform(ks[11], (B_HEADS,), jnp.float32, 1.0, 5.0),
        'b_w_q': nrm(ks[12], (n_b, D, b_width), D ** -0.5),
        'b_w_o': nrm(ks[13], (n_b, b_width, D), b_width ** -0.5),
        'router_w': nrm(ks[14], (DEPTH, D, N_EXPERTS), D ** -0.5),
        'router_b': nrm(ks[15], (DEPTH, N_EXPERTS), 0.01),
        'w_gu': nrm(ks[16], (DEPTH, N_EXPERTS, D, 2 * D_FF), D ** -0.5),
        'b_gu': nrm(ks[17], (DEPTH, N_EXPERTS, 2 * D_FF), 0.01),
        'w_down': nrm(ks[18], (DEPTH, N_EXPERTS, D_FF, D), D_FF ** -0.5),
        'b_down': nrm(ks[19], (DEPTH, N_EXPERTS, D), 0.01),
        'final_norm_g': 1.0 + nrm(ks[20], (D,), 0.02),
    }


def reference(x, c, ada_w, ada_b, norm_mix_g, norm_ffn_g, a_w_qkv, a_w_o, rel_bias,
              kv_norm_g, w_kvf, b_f, b_w_q, b_w_o, router_w, router_b, w_gu, b_gu,
              w_down, b_down, final_norm_g):
    n_a = DEPTH // 2
    c_act = jax.nn.silu(c)
    k_sh = v_sh = f_sh = None
    for layer in range(DEPTH):
        ada = c_act @ ada_w[layer] + ada_b[layer]
        shift_m, scale_m, gate_m, shift_f, scale_f, gate_f = jnp.split(ada, 6, axis=-1)
        h = modulate(rms_norm(x, norm_mix_g[layer]), shift_m, scale_m)
        if layer < n_a:
            mix = dilated_mixer(h, a_w_qkv[layer], a_w_o[layer], rel_bias)
        else:
            if layer == n_a:
                k_sh, v_sh, f_sh = shared_kv(x, kv_norm_g, w_kvf, b_f)
            mix = forgetting_mixer(h, b_w_q[layer - n_a], b_w_o[layer - n_a], k_sh, v_sh, f_sh)
        x = x + gate_m[:, None, :] * mix
        h = modulate(rms_norm(x, norm_ffn_g[layer]), shift_f, scale_f)
        x = x + gate_f[:, None, :] * moe_ffn(h, router_w[layer], router_b[layer], w_gu[layer],
                                             b_gu[layer], w_down[layer], b_down[layer])
    return rms_norm(x, final_norm_g)
```

```python
import functools
import math

import numpy as np
import jax
import jax.numpy as jnp
from jax import lax
from jax.experimental import pallas as pl
from jax.experimental.pallas import tpu as pltpu

F32 = jnp.float32
BF16 = jnp.bfloat16

D_MODEL = 1024
HEAD_DIM = 64
N_HEADS = 16
DIL_PAIRS = ((128, 1), (512, 4), (2048, 16))
DIL_BLOCK = 128
NUM_BUCKETS = 32
MAX_DISTANCE = 2048
N_EXPERTS = 32
TOP_K = 4
SWIGLU_LIMIT = 7.0
SWIGLU_ALPHA = 1.702
RMS_EPS = 1e-6
NEG = -1e30

ROW_TILE = 512
MOE_TILE = 256
FOX_TILE = 512
AUG = 128
VMEM_LIMIT = 56 * 1024 * 1024


def _params(sem, vmem=VMEM_LIMIT):
    return pltpu.CompilerParams(dimension_semantics=sem, vmem_limit_bytes=vmem)


def _dot(a, b):
    return jnp.dot(a, b, preferred_element_type=F32)


def _dot_nt(a, b):
    return lax.dot_general(a, b, (((1,), (1,)), ((), ())), preferred_element_type=F32)


def _dot_tn(a, b):
    return lax.dot_general(a, b, (((0,), (0,)), ((), ())), preferred_element_type=F32)


def _norm_mod(x, g, shift, scale):
    ms = jnp.mean(x * x, axis=-1, keepdims=True)
    y = x * lax.rsqrt(ms + RMS_EPS) * g
    if scale is not None:
        y = y * (1.0 + scale) + shift
    return y


def _ada_kernel(c_ref, w_ref, b_ref, o_ref):
    c = c_ref[...]
    act = c * (1.0 / (1.0 + jnp.exp(-c)))
    o_ref[0] = jnp.dot(act, w_ref[0], preferred_element_type=F32,
                       precision=lax.Precision.HIGHEST) + b_ref[0]


def _ada(c, ada_w, ada_b):
    depth, d, n = ada_w.shape
    bsz = c.shape[0]
    tn = 1536
    return pl.pallas_call(
        _ada_kernel,
        out_shape=jax.ShapeDtypeStruct((depth, bsz, n), F32),
        grid=(depth, n // tn),
        in_specs=[
            pl.BlockSpec((bsz, d), lambda l, j: (0, 0)),
            pl.BlockSpec((1, d, tn), lambda l, j: (l, 0, j)),
            pl.BlockSpec((1, 1, tn), lambda l, j: (l, 0, j)),
        ],
        out_specs=pl.BlockSpec((1, bsz, tn), lambda l, j: (l, 0, j)),
        compiler_params=_params(("parallel", "parallel")),
        name="ada",
    )(c, ada_w, ada_b.reshape(depth, 1, n))


def _proj_kernel(x_ref, g_ref, sh_ref, sc_ref, w_ref, o_ref, h_scr):
    @pl.when(pl.program_id(1) == 0)
    def _():
        h_scr[...] = _norm_mod(x_ref[...], g_ref[...], sh_ref[0], sc_ref[0]).astype(BF16)

    o_ref[...] = _dot(h_scr[...], w_ref[...]).astype(o_ref.dtype)


def _proj(x, g, shift, scale, w, seq, tn):
    t, d = x.shape
    n = w.shape[1]
    tpb = seq // ROW_TILE
    return pl.pallas_call(
        _proj_kernel,
        out_shape=jax.ShapeDtypeStruct((t, n), BF16),
        grid=(t // ROW_TILE, n // tn),
        in_specs=[
            pl.BlockSpec((ROW_TILE, d), lambda i, j: (i, 0)),
            pl.BlockSpec((1, d), lambda i, j: (0, 0)),
            pl.BlockSpec((1, 1, d), lambda i, j: (i // tpb, 0, 0)),
            pl.BlockSpec((1, 1, d), lambda i, j: (i // tpb, 0, 0)),
            pl.BlockSpec((d, tn), lambda i, j: (0, j)),
        ],
        out_specs=pl.BlockSpec((ROW_TILE, tn), lambda i, j: (i, j)),
        scratch_shapes=[pltpu.VMEM((ROW_TILE, d), BF16)],
        compiler_params=_params(("parallel", "arbitrary")),
        name="proj",
    )(x, g.reshape(1, d), shift, scale, w)


def _kvf_kernel(x_ref, g_ref, w_ref, wf_ref, bf_ref, tri_ref, kv_ref, f_ref, carry, *, tpb):
    h =_norm_mod(x_ref[...], g_ref[...], None, None).astype(BF16)
    kv_ref[...] = _dot(h, w_ref[...]).astype(kv_ref.dtype)
    z = _dot(h, wf_ref[...]) + bf_ref[...]
    lf = jnp.minimum(z, 0.0) - jnp.log(1.0 + jnp.exp(-jnp.abs(z)))
    hi = lf.astype(BF16)
    r1 = lf - hi.astype(F32)
    mid = r1.astype(BF16)
    lo = (r1 - mid.astype(F32)).astype(BF16)
    tri = tri_ref[...]
    cs = _dot(tri, hi) + _dot(tri, mid) + _dot(tri, lo)

    @pl.when(pl.program_id(0) % tpb == 0)
    def _():
        carry[...] = jnp.zeros_like(carry)

    cs = cs + carry[...]
    f_ref[...] = cs
    carry[...] = cs[ROW_TILE - 1:ROW_TILE, :]


def _kvf(x, g, w_kv, w_f, b_f, n_batch):
    t, d = x.shape
    n = w_kv.shape[1]
    tri = jnp.asarray(np.tril(np.ones((ROW_TILE, ROW_TILE), np.float32)), BF16)
    assert (t // ROW_TILE) % n_batch == 0
    return pl.pallas_call(
        functools.partial(_kvf_kernel, tpb=t // ROW_TILE // n_batch),
        out_shape=(jax.ShapeDtypeStruct((t, n), BF16), jax.ShapeDtypeStruct((t, 128), F32)),
        grid=(t // ROW_TILE,),
        in_specs=[
            pl.BlockSpec((ROW_TILE, d), lambda i: (i, 0)),
            pl.BlockSpec((1, d), lambda i: (0, 0)),
            pl.BlockSpec((d, n), lambda i: (0, 0)),
            pl.BlockSpec((d, 128), lambda i: (0, 0)),
            pl.BlockSpec((1, 128), lambda i: (0, 0)),
            pl.BlockSpec((ROW_TILE, ROW_TILE), lambda i: (0, 0)),
        ],
        out_specs=(pl.BlockSpec((ROW_TILE, n), lambda i: (i, 0)),
                   pl.BlockSpec((ROW_TILE, 128), lambda i: (i, 0))),
        scratch_shapes=[pltpu.VMEM((1, 128), F32)],
        compiler_params=_params(("arbitrary",)),
        name="kvf",
    )(x, g.reshape(1, d), w_kv, w_f, b_f, tri)


def _dil_kernel(q_ref, kp_ref, kc_ref, vp_ref, vc_ref, bias_ref, o_ref, lse_ref):
    n = pl.program_id(2)
    row = lax.broadcasted_iota(jnp.int32, (DIL_BLOCK, DIL_BLOCK), 0)
    col = lax.broadcasted_iota(jnp.int32, (DIL_BLOCK, DIL_BLOCK), 1)
    mask_cur = col <= row
    mask_prev = jnp.logical_and(col >= row, n > 0)
    for h in range(N_HEADS):
        sl = slice(h * HEAD_DIM, (h + 1) * HEAD_DIM)
        q = q_ref[0, :, sl]
        s_c = _dot_nt(q, kc_ref[0, :, sl]) + bias_ref[h, :, DIL_BLOCK:]
        s_p = _dot_nt(q, kp_ref[0, :, sl]) + bias_ref[h, :, :DIL_BLOCK]
        s_c = jnp.where(mask_cur, s_c, NEG)
        s_p = jnp.where(mask_prev, s_p, NEG)
        m = jnp.maximum(jnp.max(s_c, axis=-1, keepdims=True), jnp.max(s_p, axis=-1, keepdims=True))
        p_c = jnp.exp(s_c - m)
        p_p = jnp.exp(s_p - m)
        den = jnp.sum(p_c, axis=-1, keepdims=True) + jnp.sum(p_p, axis=-1, keepdims=True)
        o = _dot(p_c.astype(BF16), vc_ref[0, :, sl]) + _dot(p_p.astype(BF16), vp_ref[0, :, sl])
        o_ref[0, :, sl] = (o / den).astype(o_ref.dtype)
        lse_ref[0, 0, :, h:h + 1] = m + jnp.log(den)


def _dilated_group(qkv, bias, group, dilation, n_batch, seq):
    width = N_HEADS * HEAD_DIM
    ncol = qkv.shape[1] // width
    sub = seq // dilation
    nb = sub // DIL_BLOCK
    view = qkv.reshape(n_batch, sub, dilation * qkv.shape[1])
    c0 = group * 3

    def spec(part, prev):
        if prev:
            return pl.BlockSpec((1, DIL_BLOCK, width),
                                lambda b, r, n: (b, jnp.maximum(n - 1, 0), r * ncol + c0 + part))
        return pl.BlockSpec((1, DIL_BLOCK, width), lambda b, r, n: (b, n, r * ncol + c0 + part))

    o, lse = pl.pallas_call(
        _dil_kernel,
        out_shape=(jax.ShapeDtypeStruct((n_batch, sub, dilation * width), BF16),
                   jax.ShapeDtypeStruct((n_batch, dilation, sub, N_HEADS), F32)),
        grid=(n_batch, dilation, nb),
        in_specs=[spec(0, False), spec(1, True), spec(1, False), spec(2, True), spec(2, False),
                  pl.BlockSpec((N_HEADS, DIL_BLOCK, 2 * DIL_BLOCK), lambda b, r, n: (0, 0, 0))],
        out_specs=(pl.BlockSpec((1, DIL_BLOCK, width), lambda b, r, n: (b, n, r)),
                   pl.BlockSpec((1, 1, DIL_BLOCK, N_HEADS), lambda b, r, n: (b, r, n, 0))),
        compiler_params=_params(("parallel", "parallel", "arbitrary")),
        name=f"dilated{group}",
    )(view, view, view, view, view, bias)
    o = o.reshape(n_batch * seq, width)
    lse = lse.transpose(0, 2, 1, 3).reshape(n_batch * seq, N_HEADS)
    return o, lse


def _t5_bucket_np(n):
    max_exact = NUM_BUCKETS // 2
    nf = np.maximum(n, 1).astype(np.float32)
    large = max_exact + (np.log(nf / np.float32(max_exact)) / np.float32(math.log(MAX_DISTANCE / max_exact))
                         * np.float32(NUM_BUCKETS - max_exact)).astype(np.int32)
    large = np.minimum(large, NUM_BUCKETS - 1)
    return np.where(n < max_exact, n, large)


def _mix_tail(mix_in, wo_ref, x_ref, gm_ref, g_ref, sh_ref, sc_ref, wr_ref, br_ref,
              x_out, h_out, lg_out, transposed):
    if transposed:
        mix = _dot_tn(mix_in, wo_ref[...])
    else:
        mix = _dot(mix_in, wo_ref[...])
    x_new = x_ref[...] + gm_ref[0] * mix
    x_out[...] = x_new
    h = _norm_mod(x_new, g_ref[...], sh_ref[0], sc_ref[0])
    h_out[...] = h.astype(BF16)
    lg_out[...] = jnp.dot(h, wr_ref[...], preferred_element_type=F32,
                          precision=lax.Precision.HIGHEST) + br_ref[...]


def _merge_oproj_kernel(o0_ref, o1_ref, o2_ref, lse_ref, ex_ref, wo_ref, x_ref, gm_ref, g_ref, sh_ref,
                        sc_ref, wr_ref, br_ref, x_out, h_out, lg_out):
    lse = lse_ref[...]
    l0, l1, l2 = lse[:, 0:16], lse[:, 16:32], lse[:, 32:48]
    m = jnp.maximum(jnp.maximum(l0, l1), l2)
    e0, e1, e2 = jnp.exp(l0 - m), jnp.exp(l1 - m), jnp.exp(l2 - m)
    inv = 1.0 / (e0 + e1 + e2)
    ex = ex_ref[...]
    merged = (_dot((e0 * inv).astype(BF16), ex) * o0_ref[...].astype(F32)
              + _dot((e1 * inv).astype(BF16), ex) * o1_ref[...].astype(F32)
              + _dot((e2 * inv).astype(BF16), ex) * o2_ref[...].astype(F32))
    _mix_tail(merged.astype(BF16), wo_ref, x_ref, gm_ref, g_ref, sh_ref, sc_ref, wr_ref, br_ref,
              x_out, h_out, lg_out, False)


def _oproj_t_kernel(ot_ref, wo_ref, x_ref, gm_ref, g_ref, sh_ref, sc_ref, wr_ref, br_ref,
                    x_out, h_out, lg_out):
    _mix_tail(ot_ref[0], wo_ref, x_ref, gm_ref, g_ref, sh_ref, sc_ref, wr_ref, br_ref,
              x_out, h_out, lg_out, True)


def _mixer_tail(mix_inputs, w_o, x, gate_m, g_ffn, shift_f, scale_f, w_r, b_r, seq, merged):
    t, d = x.shape
    tpb = seq // ROW_TILE
    row = lambda i: (i, 0)
    full = lambda i: (0, 0)
    per_b = lambda i: (i // tpb, 0, 0)
    tail_specs = [
        pl.BlockSpec((d, d), full),
        pl.BlockSpec((ROW_TILE, d), row),
        pl.BlockSpec((1, 1, d), per_b),
        pl.BlockSpec((1, d), full),
        pl.BlockSpec((1, 1, d), per_b),
        pl.BlockSpec((1, 1, d), per_b),
        pl.BlockSpec((d, N_EXPERTS), full),
        pl.BlockSpec((1, N_EXPERTS), full),
    ]
    tail_args = (w_o, x, gate_m, g_ffn.reshape(1, d), shift_f, scale_f, w_r, b_r.reshape(1, N_EXPERTS))
    if merged:
        o0, o1, o2, lse = mix_inputs
        expand = jnp.asarray(np.kron(np.eye(N_HEADS, dtype=np.float32),
                                     np.ones((1, HEAD_DIM), np.float32)), BF16)
        kern = _merge_oproj_kernel
        specs = [pl.BlockSpec((ROW_TILE, d), row)] * 3 + [
            pl.BlockSpec((ROW_TILE, 3 * N_HEADS), row), pl.BlockSpec((N_HEADS, d), full)]
        args = (o0, o1, o2, lse, expand)
    else:
        (ot,) = mix_inputs
        kern = _oproj_t_kernel
        specs = [pl.BlockSpec((1, d, ROW_TILE), lambda i: (i // tpb, 0, i % tpb))]
        args = (ot,)
    return pl.pallas_call(
        kern,
        out_shape=(jax.ShapeDtypeStruct((t, d), F32), jax.ShapeDtypeStruct((t, d), BF16),
                   jax.ShapeDtypeStruct((t, N_EXPERTS), F32)),
        grid=(t // ROW_TILE,),
        in_specs=specs + tail_specs,
        out_specs=(pl.BlockSpec((ROW_TILE, d), row), pl.BlockSpec((ROW_TILE, d), row),
                   pl.BlockSpec((ROW_TILE, N_EXPERTS), row)),
        compiler_params=_params(("parallel",)),
        name="mixer_tail_merge" if merged else "mixer_tail_t",
    )(*args, *tail_args)


def _fox_kernel(q_ref, k_ref, v_ref, o_ref):
    qi = pl.program_id(2)
    qt = q_ref[0, 0]

    def step(kj, carry, masked):
        m, l, acc = carry
        k = k_ref[0, 0, pl.ds(pl.multiple_of(kj * FOX_TILE, FOX_TILE), FOX_TILE), :]
        s = _dot(k, qt)
        if masked:
            kpos = lax.broadcasted_iota(jnp.int32, (FOX_TILE, FOX_TILE), 0)
            qpos = lax.broadcasted_iota(jnp.int32, (FOX_TILE, FOX_TILE), 1)
            s = jnp.where(kpos <= qpos, s, NEG)
        m_new = jnp.maximum(m, jnp.max(s, axis=0, keepdims=True))
        alpha = jnp.exp(m - m_new)
        p = jnp.exp(s - m_new)
        l = alpha * l + jnp.sum(p, axis=0, keepdims=True)
        acc = alpha * acc + _dot(v_ref[0, 0, kj], p.astype(BF16))
        return m_new, l, acc

    init = (jnp.full((1, FOX_TILE), NEG, F32), jnp.zeros((1, FOX_TILE), F32),
            jnp.zeros((HEAD_DIM, FOX_TILE), F32))
    carry = lax.fori_loop(0, qi, lambda kj, c: step(kj, c, False), init)
    _, l, acc = step(qi, carry, True)
    o_ref[0, 0] = (acc / l).astype(o_ref.dtype)


def _fox(qt_aug, k_aug, vt):
    b, h, _, s = qt_aug.shape
    nk = s // FOX_TILE
    return pl.pallas_call(
        _fox_kernel,
        out_shape=jax.ShapeDtypeStruct((b, h, HEAD_DIM, s), BF16),
        grid=(b, h, nk),
        in_specs=[
            pl.BlockSpec((1, 1, AUG, FOX_TILE), lambda bi, hi, qi: (bi, hi, 0, qi)),
            pl.BlockSpec((1, 1, s, AUG), lambda bi, hi, qi: (bi, hi, 0, 0)),
            pl.BlockSpec((1, 1, nk, HEAD_DIM, FOX_TILE), lambda bi, hi, qi: (bi, hi, 0, 0, 0)),
        ],
        out_specs=pl.BlockSpec((1, 1, HEAD_DIM, FOX_TILE), lambda bi, hi, qi: (bi, hi, 0, qi)),
        compiler_params=_params(("parallel", "parallel", "arbitrary")),
        name="fox",
    )(qt_aug, k_aug, vt)


def _split3(f):
    def top(v):
        return lax.bitcast_convert_type(lax.bitcast_convert_type(v, jnp.uint32) & jnp.uint32(0xFFFF0000), F32)
    hi = top(f)
    r1 = f - hi
    mid = top(r1)
    lo = r1 - mid
    return hi.astype(BF16), mid.astype(BF16), lo.astype(BF16)


def _expert_kernel(be_ref, nu_ref, x_ref, wgu_ref, bgu_ref, wd_ref, bd_ref, gate_ref, o_ref):
    i = pl.program_id(0)
    d_ff = wd_ref.shape[1]

    @pl.when(i < nu_ref[0])
    def _():
        gu = _dot(x_ref[...], wgu_ref[0]) + bgu_ref[0]
        g = jnp.minimum(gu[:, :d_ff], SWIGLU_LIMIT)
        u = jnp.clip(gu[:, d_ff:], -SWIGLU_LIMIT, SWIGLU_LIMIT)
        act = (u + 1.0) * g * (1.0 / (1.0 + jnp.exp(-SWIGLU_ALPHA * g)))
        y = _dot(act.astype(BF16), wd_ref[0]) + bd_ref[0]
        o_ref[...] = (y * gate_ref[...]).astype(o_ref.dtype)

    @pl.when(i >= nu_ref[0])
    def _():
        o_ref[...] = jnp.zeros_like(o_ref)


def _experts(xbuf, w_gu, b_gu, w_down, b_down, gate_rows, block_expert, n_used):
    n_rows, d = xbuf.shape
    e, _, n_gu = w_gu.shape
    d_ff = w_down.shape[1]
    n_blocks = n_rows // MOE_TILE
    grid_spec = pltpu.PrefetchScalarGridSpec(
        num_scalar_prefetch=2,
        grid=(n_blocks,),
        in_specs=[
            pl.BlockSpec((MOE_TILE, d), lambda i, be, nu: (i, 0)),
            pl.BlockSpec((1, d, n_gu), lambda i, be, nu: (be[i], 0, 0)),
            pl.BlockSpec((1, 1, n_gu), lambda i, be, nu: (be[i], 0, 0)),
            pl.BlockSpec((1, d_ff, d), lambda i, be, nu: (be[i], 0, 0)),
            pl.BlockSpec((1, 1, d), lambda i, be, nu: (be[i], 0, 0)),
            pl.BlockSpec((MOE_TILE, 1), lambda i, be, nu: (i, 0)),
        ],
        out_specs=pl.BlockSpec((MOE_TILE, d), lambda i, be, nu: (i, 0)),
    )
    return pl.pallas_call(
        _expert_kernel,
        out_shape=jax.ShapeDtypeStruct((n_rows, d), BF16),
        grid_spec=grid_spec,
        compiler_params=_params(("arbitrary",)),
        name="experts",
    )(block_expert, n_used, xbuf, w_gu, b_gu.reshape(e, 1, n_gu), w_down, b_down.reshape(e, 1, d), gate_rows)


def _route(logits):
    t = logits.shape[0]
    n_slots = t * TOP_K
    n_rows = n_slots + N_EXPERTS * MOE_TILE
    top_val, top_idx = lax.top_k(logits, TOP_K)
    gates = jax.nn.softmax(top_val, axis=-1)
    e_flat = top_idx.reshape(-1).astype(jnp.int32)
    onehot = (e_flat[:, None] == jnp.arange(N_EXPERTS, dtype=jnp.int32)[None, :]).astype(jnp.int32)
    csum = jnp.cumsum(onehot, axis=0)
    rank = jnp.sum((csum - onehot) * onehot, axis=1)
    counts = csum[-1]
    padded = (counts + MOE_TILE - 1) // MOE_TILE * MOE_TILE
    pad_ends = jnp.cumsum(padded)
    pad_starts = pad_ends - padded
    dest = pad_starts[e_flat] + rank
    tok_flat = jnp.arange(n_slots, dtype=jnp.int32) // TOP_K
    tok_rows = jnp.zeros((n_rows,), jnp.int32).at[dest].set(tok_flat)
    gate_rows = jnp.zeros((n_rows,), F32).at[dest].set(gates.reshape(-1))
    block_start = jnp.arange(n_rows // MOE_TILE, dtype=jnp.int32) * MOE_TILE
    block_expert = jnp.minimum(jnp.searchsorted(pad_ends, block_start, side='right'),
                               N_EXPERTS - 1).astype(jnp.int32)
    n_used = (pad_ends[-1] // MOE_TILE).astype(jnp.int32).reshape(1)
    return dest, tok_rows, gate_rows.reshape(n_rows, 1), block_expert, n_used


def _combine_kernel(x_ref, y_ref, gf_ref, g_ref, o_ref, *, final):
    d = x_ref.shape[1]
    y = y_ref[...]
    moe = (y[:, 0:d].astype(F32) + y[:, d:2 * d].astype(F32)
           + y[:, 2 * d:3 * d].astype(F32) + y[:, 3 * d:4 * d].astype(F32))
    x_new = x_ref[...] + gf_ref[0] * moe
    if final:
        x_new = _norm_mod(x_new, g_ref[...], None, None)
    o_ref[...] = x_new


def _combine(x, y_slots, gate_f, g_final, seq, final):
    t, d = x.shape
    tpb = seq // ROW_TILE
    return pl.pallas_call(
        functools.partial(_combine_kernel, final=final),
        out_shape=jax.ShapeDtypeStruct((t, d), F32),
        grid=(t // ROW_TILE,),
        in_specs=[
            pl.BlockSpec((ROW_TILE, d), lambda i: (i, 0)),
            pl.BlockSpec((ROW_TILE, TOP_K * d), lambda i: (i, 0)),
            pl.BlockSpec((1, 1, d), lambda i: (i // tpb, 0, 0)),
            pl.BlockSpec((1, d), lambda i: (0, 0)),
        ],
        out_specs=pl.BlockSpec((ROW_TILE, d), lambda i: (i, 0)),
        compiler_params=_params(("parallel",)),
        name="combine_final" if final else "combine",
    )(x, y_slots, gate_f, g_final.reshape(1, d))


def _moe(x, h, logits, gate_f, w_gu, b_gu, w_down, b_down, g_final, seq, final):
    t, d = x.shape
    dest, tok_rows, gate_rows, block_expert, n_used = _route(logits)
    xbuf = jnp.take(h, tok_rows, axis=0)
    ybuf = _experts(xbuf, w_gu.astype(BF16), b_gu, w_down.astype(BF16), b_down, gate_rows,
                    block_expert, n_used)
    y_slots = jnp.take(ybuf, dest, axis=0).reshape(t, TOP_K * d)
    return _combine(x, y_slots, gate_f, g_final, seq, final)


def kernel(x, c, ada_w, ada_b, norm_mix_g, norm_ffn_g, a_w_qkv, a_w_o, rel_bias, kv_norm_g, w_kvf, b_f,
           b_w_q, b_w_o, router_w, router_b, w_gu, b_gu, w_down, b_down, final_norm_g):
    n_batch, seq, d = x.shape
    t = n_batch * seq
    width = N_HEADS * HEAD_DIM
    xf = x.reshape(t, d)

    ada = _ada(c, ada_w, ada_b)
    mods = [[ada[l, :, i * d:(i + 1) * d].reshape(n_batch, 1, d) for i in range(6)] for l in range(2)]

    shift_m, scale_m, gate_m, shift_f, scale_f, gate_f = mods[0]
    qscale = np.ones((3, 3, 1, 1), np.float32)
    qscale[:, 0] = HEAD_DIM ** -0.5
    w_qkv = (a_w_qkv[0].reshape(d, 3, 3, N_HEADS, HEAD_DIM) * qscale).reshape(d, 9 * width).astype(BF16)
    qkv = _proj(xf, norm_mix_g[0], shift_m, scale_m, w_qkv, seq, 1536)

    qi = np.arange(DIL_BLOCK, dtype=np.int32)[:, None]
    kj = np.arange(2 * DIL_BLOCK, dtype=np.int32)[None, :]
    delta = np.clip(qi + DIL_BLOCK - kj, 0, None)
    outs, lses = [], []
    for g, (window, dilation) in enumerate(DIL_PAIRS):
        assert window // dilation == DIL_BLOCK
        bucket = _t5_bucket_np(delta * dilation)
        tab = rel_bias[:, g * N_HEADS:(g + 1) * N_HEADS].astype(F32)
        bias = jnp.transpose(tab[bucket], (2, 0, 1))
        o, lse = _dilated_group(qkv, bias, g, dilation, n_batch, seq)
        outs.append(o)
        lses.append(lse)
    lse_all = jnp.concatenate(lses, axis=1)
    x1, h1, logits1 = _mixer_tail((outs[0], outs[1], outs[2], lse_all), a_w_o[0].astype(BF16), xf, gate_m,
                                  norm_ffn_g[0], shift_f, scale_f, router_w[0], router_b[0], seq, True)
    x2 = _moe(x1, h1, logits1, gate_f, w_gu[0], b_gu[0], w_down[0], b_down[0], final_norm_g, seq, False)

    nkv = 2 * width
    w_f = jnp.zeros((d, 128), F32).at[:, :N_HEADS].set(w_kvf[:, nkv:]).astype(BF16)
    b_fp = jnp.zeros((1, 128), F32).at[0, :N_HEADS].set(b_f)
    kv, fcum = _kvf(x2, kv_norm_g, w_kvf[:, :nkv].astype(BF16), w_f, b_fp, n_batch)

    shift_m, scale_m, gate_m, shift_f, scale_f, gate_f = mods[1]
    q1 = _proj(x2, norm_mix_g[1], shift_m, scale_m, (b_w_q[0] * (HEAD_DIM ** -0.5)).astype(BF16), seq, width)

    nk = seq // FOX_TILE
    f = fcum[:, :N_HEADS].reshape(n_batch, seq, N_HEADS)
    f_hs = jnp.transpose(f, (0, 2, 1))
    hi, mid, lo = _split3(f_hs)
    ones = jnp.ones_like(hi)
    pad = AUG - HEAD_DIM - 6
    kh = jnp.transpose(kv[:, :width].reshape(n_batch, seq, N_HEADS, HEAD_DIM), (0, 2, 1, 3))
    k_extra = jnp.stack([hi, mid, lo, ones, ones, ones], axis=-1)
    k_aug = jnp.concatenate([kh, k_extra, jnp.zeros((n_batch, N_HEADS, seq, pad), BF16)], axis=-1)
    qh = jnp.transpose(q1.reshape(n_batch, seq, N_HEADS, HEAD_DIM), (0, 2, 3, 1))
    q_extra = jnp.stack([-ones, -ones, -ones, hi, mid, lo], axis=2)
    qt_aug = jnp.concatenate([qh, q_extra, jnp.zeros((n_batch, N_HEADS, pad, seq), BF16)], axis=2)
    vt = jnp.transpose(kv[:, width:].reshape(n_batch, nk, FOX_TILE, N_HEADS, HEAD_DIM), (0, 3, 1, 4, 2))
    ot = _fox(qt_aug, k_aug, vt).reshape(n_batch, width, seq)

    x3, h3, logits3 = _mixer_tail((ot,), b_w_o[0].astype(BF16), x2, gate_m, norm_ffn_g[1], shift_f, scale_f,
                                  router_w[1], router_b[1], seq, False)
    out = _moe(x3, h3, logits3, gate_f, w_gu[1], b_gu[1], w_down[1], b_down[1], final_norm_g, seq, True)
    return out.reshape(n_batch, seq, d)
```

```python
import functools
import math

import numpy as np
import jax
import jax.numpy as jnp
from jax import lax
from jax.experimental import pallas as pl
from jax.experimental.pallas import tpu as pltpu

F32 = jnp.float32
BF16 = jnp.bfloat16

D_MODEL = 1024
HEAD_DIM = 64
N_HEADS = 16
DIL_PAIRS = ((128, 1), (512, 4), (2048, 16))
DIL_BLOCK = 128
NUM_BUCKETS = 32
MAX_DISTANCE = 2048
N_EXPERTS = 32
TOP_K = 4
SWIGLU_LIMIT = 7.0
SWIGLU_ALPHA = 1.702
RMS_EPS = 1e-6
NEG = -1e30

ROW_TILE = 512
MOE_TILE = 256
FOX_TILE = 512
AUG = 128
VMEM_LIMIT = 56 * 1024 * 1024


def _params(sem, vmem=VMEM_LIMIT):
    return pltpu.CompilerParams(dimension_semantics=sem, vmem_limit_bytes=vmem)


def _dot(a, b):
    return jnp.dot(a, b, preferred_element_type=F32)


def _dot_nt(a, b):
    return lax.dot_general(a, b, (((1,), (1,)), ((), ())), preferred_element_type=F32)


def _dot_tn(a, b):
    return lax.dot_general(a, b, (((0,), (0,)), ((), ())), preferred_element_type=F32)


def _norm_mod(x, g, shift, scale):
    ms = jnp.mean(x * x, axis=-1, keepdims=True)
    y = x * lax.rsqrt(ms + RMS_EPS) * g
    if scale is not None:
        y = y * (1.0 + scale) + shift
    return y


def _ada_kernel(c_ref, w_ref, b_ref, o_ref):
    c = c_ref[...]
    act = c * (1.0 / (1.0 + jnp.exp(-c)))
    o_ref[0] = jnp.dot(act, w_ref[0], preferred_element_type=F32,
                       precision=lax.Precision.HIGHEST) + b_ref[0]


def _ada(c, ada_w, ada_b):
    depth, d, n = ada_w.shape
    bsz = c.shape[0]
    tn = 1536
    return pl.pallas_call(
        _ada_kernel,
        out_shape=jax.ShapeDtypeStruct((depth, bsz, n), F32),
        grid=(depth, n // tn),
        in_specs=[
            pl.BlockSpec((bsz, d), lambda l, j: (0, 0)),
            pl.BlockSpec((1, d, tn), lambda l, j: (l, 0, j)),
            pl.BlockSpec((1, 1, tn), lambda l, j: (l, 0, j)),
        ],
        out_specs=pl.BlockSpec((1, bsz, tn), lambda l, j: (l, 0, j)),
        compiler_params=_params(("parallel", "parallel")),
        name="ada",
    )(c, ada_w, ada_b.reshape(depth, 1, n))


def _proj_kernel(x_ref, g_ref, sh_ref, sc_ref, w_ref, o_ref, h_scr):
    @pl.when(pl.program_id(1) == 0)
    def _():
        h_scr[...] = _norm_mod(x_ref[...], g_ref[...], sh_ref[0], sc_ref[0]).astype(BF16)

    o_ref[...] = _dot(h_scr[...], w_ref[...]).astype(o_ref.dtype)


def _proj(x, g, shift, scale, w, seq, tn):
    t, d = x.shape
    n = w.shape[1]
    tpb = seq // ROW_TILE
    return pl.pallas_call(
        _proj_kernel,
        out_shape=jax.ShapeDtypeStruct((t, n), BF16),
        grid=(t // ROW_TILE, n // tn),
        in_specs=[
            pl.BlockSpec((ROW_TILE, d), lambda i, j: (i, 0)),
            pl.BlockSpec((1, d), lambda i, j: (0, 0)),
            pl.BlockSpec((1, 1, d), lambda i, j: (i // tpb, 0, 0)),
            pl.BlockSpec((1, 1, d), lambda i, j: (i // tpb, 0, 0)),
            pl.BlockSpec((d, tn), lambda i, j: (0, j)),
        ],
        out_specs=pl.BlockSpec((ROW_TILE, tn), lambda i, j: (i, j)),
        scratch_shapes=[pltpu.VMEM((ROW_TILE, d), BF16)],
        compiler_params=_params(("parallel", "arbitrary")),
        name="proj",
    )(x, g.reshape(1, d), shift, scale, w)


def _split3(f):
    def top(v):
        return lax.bitcast_convert_type(lax.bitcast_convert_type(v, jnp.uint32) & jnp.uint32(0xFFFF0000), F32)
    hi = top(f)
    r1 = f - hi
    mid = top(r1)
    return hi, mid, r1 - mid


def _forget_pieces(fc):
    hi, mid, lo = _split3(fc)
    lane = lax.broadcasted_iota(jnp.int32, fc.shape, 1)
    x = jnp.where(lane < N_HEADS, hi, jnp.where(lane < 2 * N_HEADS, mid, jnp.where(lane < 3 * N_HEADS, lo, 0.0)))
    return x.astype(BF16)


def _aug_tables(key_side):
    place = np.zeros((128, N_HEADS * AUG), np.float32)
    const = np.zeros((1, N_HEADS * AUG), np.float32)
    for h in range(N_HEADS):
        base = h * AUG + HEAD_DIM
        for piece in range(3):
            if key_side:
                place[piece * N_HEADS + h, base + piece] = 1.0
                const[0, base + 3 + piece] = 1.0
            else:
                place[piece * N_HEADS + h, base + 3 + piece] = 1.0
                const[0, base + piece] = -1.0
    return jnp.asarray(place, BF16), jnp.asarray(const, F32)


def _aug_weight(w):
    d = w.shape[0]
    w = jnp.pad(w.reshape(d, N_HEADS, HEAD_DIM), ((0, 0), (0, 0), (0, AUG - HEAD_DIM)))
    return w.reshape(d, N_HEADS * AUG).astype(BF16)


def _kvf_kernel(x_ref, g_ref, wk_ref, wv_ref, wf_ref, bf_ref, tri_ref, pl_ref, cr_ref,
                k_ref, v_ref, f_ref, carry, *, tpb):
    h = _norm_mod(x_ref[...], g_ref[...], None, None).astype(BF16)
    v_ref[...] = _dot(h, wv_ref[...]).astype(v_ref.dtype)
    z = _dot(h, wf_ref[...]) + bf_ref[...]
    lf = jnp.minimum(z, 0.0) - jnp.log(1.0 + jnp.exp(-jnp.abs(z)))
    hi, mid, lo = _split3(lf)
    tri = tri_ref[...]
    cs = _dot(tri, hi.astype(BF16)) + _dot(tri, mid.astype(BF16)) + _dot(tri, lo.astype(BF16))

    @pl.when(pl.program_id(0) % tpb == 0)
    def _():
        carry[...] = jnp.zeros_like(carry)

    cs = cs + carry[...]
    f_ref[...] = cs
    carry[...] = cs[ROW_TILE - 1:ROW_TILE, :]
    k_ref[...] = (_dot(h, wk_ref[...]) + _dot(_forget_pieces(cs), pl_ref[...]) + cr_ref[...]).astype(k_ref.dtype)


def _kvf(x, g, w_k, w_v, w_f, b_f, n_batch):
    t, d = x.shape
    n_aug = N_HEADS * AUG
    tri = jnp.asarray(np.tril(np.ones((ROW_TILE, ROW_TILE), np.float32)), BF16)
    place, const = _aug_tables(True)
    assert (t // ROW_TILE) % n_batch == 0
    full = lambda i: (0, 0)
    row = lambda i: (i, 0)
    return pl.pallas_call(
        functools.partial(_kvf_kernel, tpb=t // ROW_TILE // n_batch),
        out_shape=(jax.ShapeDtypeStruct((t, n_aug), BF16), jax.ShapeDtypeStruct((t, d), BF16),
                   jax.ShapeDtypeStruct((t, 128), F32)),
        grid=(t // ROW_TILE,),
        in_specs=[
            pl.BlockSpec((ROW_TILE, d), row),
            pl.BlockSpec((1, d), full),
            pl.BlockSpec((d, n_aug), full),
            pl.BlockSpec((d, d), full),
            pl.BlockSpec((d, 128), full),
            pl.BlockSpec((1, 128), full),
            pl.BlockSpec((ROW_TILE, ROW_TILE), full),
            pl.BlockSpec((128, n_aug), full),
            pl.BlockSpec((1, n_aug), full),
        ],
        out_specs=(pl.BlockSpec((ROW_TILE, n_aug), row), pl.BlockSpec((ROW_TILE, d), row),
                   pl.BlockSpec((ROW_TILE, 128), row)),
        scratch_shapes=[pltpu.VMEM((1, 128), F32)],
        compiler_params=_params(("arbitrary",)),
        name="kvf",
    )(x, g.reshape(1, d), w_k, w_v, w_f, b_f, tri, place, const)


def _qaug_kernel(x_ref, g_ref, sh_ref, sc_ref, w_ref, f_ref, pl_ref, cr_ref, o_ref):
    h = _norm_mod(x_ref[...], g_ref[...], sh_ref[0], sc_ref[0]).astype(BF16)
    o_ref[...] = (_dot(h, w_ref[...]) + _dot(_forget_pieces(f_ref[...]), pl_ref[...])
                  + cr_ref[...]).astype(o_ref.dtype)


def _qaug(x, g, shift, scale, w, fcum, seq):
    t, d = x.shape
    n_aug = N_HEADS * AUG
    tpb = seq // ROW_TILE
    place, const = _aug_tables(False)
    full = lambda i: (0, 0)
    row = lambda i: (i, 0)
    per_b = lambda i: (i // tpb, 0, 0)
    return pl.pallas_call(
        _qaug_kernel,
        out_shape=jax.ShapeDtypeStruct((t, n_aug), BF16),
        grid=(t // ROW_TILE,),
        in_specs=[
            pl.BlockSpec((ROW_TILE, d), row),
            pl.BlockSpec((1, d), full),
            pl.BlockSpec((1, 1, d), per_b),
            pl.BlockSpec((1, 1, d), per_b),
            pl.BlockSpec((d, n_aug), full),
            pl.BlockSpec((ROW_TILE, 128), row),
            pl.BlockSpec((128, n_aug), full),
            pl.BlockSpec((1, n_aug), full),
        ],
        out_specs=pl.BlockSpec((ROW_TILE, n_aug), row),
        compiler_params=_params(("parallel",)),
        name="qaug",
    )(x, g.reshape(1, d), shift, scale, w, fcum, place, const)


def _dil_kernel(q_ref, kp_ref, kc_ref, vp_ref, vc_ref, bias_ref, o_ref, lse_ref):
    n = pl.program_id(2)
    row = lax.broadcasted_iota(jnp.int32, (DIL_BLOCK, DIL_BLOCK), 0)
    col = lax.broadcasted_iota(jnp.int32, (DIL_BLOCK, DIL_BLOCK), 1)
    mask_cur = col <= row
    mask_prev = jnp.logical_and(col >= row, n > 0)
    for h in range(N_HEADS):
        sl = slice(h * HEAD_DIM, (h + 1) * HEAD_DIM)
        q = q_ref[0, :, sl]
        s_c = _dot_nt(q, kc_ref[0, :, sl]) + bias_ref[h, :, DIL_BLOCK:]
        s_p = _dot_nt(q, kp_ref[0, :, sl]) + bias_ref[h, :, :DIL_BLOCK]
        s_c = jnp.where(mask_cur, s_c, NEG)
        s_p = jnp.where(mask_prev, s_p, NEG)
        m = jnp.maximum(jnp.max(s_c, axis=-1, keepdims=True), jnp.max(s_p, axis=-1, keepdims=True))
        p_c = jnp.exp(s_c - m)
        p_p = jnp.exp(s_p - m)
        den = jnp.sum(p_c, axis=-1, keepdims=True) + jnp.sum(p_p, axis=-1, keepdims=True)
        o = _dot(p_c.astype(BF16), vc_ref[0, :, sl]) + _dot(p_p.astype(BF16), vp_ref[0, :, sl])
        o_ref[0, :, sl] = (o / den).astype(o_ref.dtype)
        lse_ref[0, 0, :, h:h + 1] = m + jnp.log(den)


def _dilated_group(qkv, bias, group, dilation, n_batch, seq):
    width = N_HEADS * HEAD_DIM
    ncol = qkv.shape[1] // width
    sub = seq // dilation
    nb = sub // DIL_BLOCK
    view = qkv.reshape(n_batch, sub, dilation * qkv.shape[1])
    c0 = group * 3

    def spec(part, prev):
        if prev:
            return pl.BlockSpec((1, DIL_BLOCK, width),
                                lambda b, r, n: (b, jnp.maximum(n - 1, 0), r * ncol + c0 + part))
        return pl.BlockSpec((1, DIL_BLOCK, width), lambda b, r, n: (b, n, r * ncol + c0 + part))

    o, lse = pl.pallas_call(
        _dil_kernel,
        out_shape=(jax.ShapeDtypeStruct((n_batch, sub, dilation * width), BF16),
                   jax.ShapeDtypeStruct((n_batch, dilation, sub, N_HEADS), F32)),
        grid=(n_batch, dilation, nb),
        in_specs=[spec(0, False), spec(1, True), spec(1, False), spec(2, True), spec(2, False),
                  pl.BlockSpec((N_HEADS, DIL_BLOCK, 2 * DIL_BLOCK), lambda b, r, n: (0, 0, 0))],
        out_specs=(pl.BlockSpec((1, DIL_BLOCK, width), lambda b, r, n: (b, n, r)),
                   pl.BlockSpec((1, 1, DIL_BLOCK, N_HEADS), lambda b, r, n: (b, r, n, 0))),
        compiler_params=_params(("parallel", "parallel", "arbitrary")),
        name=f"dilated{group}",
    )(view, view, view, view, view, bias)
    o = o.reshape(n_batch * seq, width)
    lse = lse.transpose(0, 2, 1, 3).reshape(n_batch * seq, N_HEADS)
    return o, lse


def _t5_bucket_np(n):
    max_exact = NUM_BUCKETS // 2
    nf = np.maximum(n, 1).astype(np.float32)
    large = max_exact + (np.log(nf / np.float32(max_exact)) / np.float32(math.log(MAX_DISTANCE / max_exact))
                         * np.float32(NUM_BUCKETS - max_exact)).astype(np.int32)
    large = np.minimum(large, NUM_BUCKETS - 1)
    return np.where(n < max_exact, n, large)


def _mix_tail(mix_in, wo_ref, x_ref, gm_ref, g_ref, sh_ref, sc_ref, wr_ref, br_ref,
              x_out, h_out, lg_out):
    mix = _dot(mix_in, wo_ref[...])
    x_new = x_ref[...] + gm_ref[0] * mix
    x_out[...] = x_new
    h = _norm_mod(x_new, g_ref[...], sh_ref[0], sc_ref[0])
    h_out[...] = h.astype(BF16)
    lg_out[...] = jnp.dot(h, wr_ref[...], preferred_element_type=F32,
                          precision=lax.Precision.HIGHEST) + br_ref[...]


def _merge_oproj_kernel(o0_ref, o1_ref, o2_ref, lse_ref, ex_ref, wo_ref, x_ref, gm_ref, g_ref, sh_ref,
                        sc_ref, wr_ref, br_ref, x_out, h_out, lg_out):
    lse = lse_ref[...]
    l0, l1, l2 = lse[:, 0:16], lse[:, 16:32], lse[:, 32:48]
    m = jnp.maximum(jnp.maximum(l0, l1), l2)
    e0, e1, e2 = jnp.exp(l0 - m), jnp.exp(l1 - m), jnp.exp(l2 - m)
    inv = 1.0 / (e0 + e1 + e2)
    ex = ex_ref[...]
    merged = (_dot((e0 * inv).astype(BF16), ex) * o0_ref[...].astype(F32)
              + _dot((e1 * inv).astype(BF16), ex) * o1_ref[...].astype(F32)
              + _dot((e2 * inv).astype(BF16), ex) * o2_ref[...].astype(F32))
    _mix_tail(merged.astype(BF16), wo_ref, x_ref, gm_ref, g_ref, sh_ref, sc_ref, wr_ref, br_ref,
              x_out, h_out, lg_out)


def _oproj_kernel(o_ref, wo_ref, x_ref, gm_ref, g_ref, sh_ref, sc_ref, wr_ref, br_ref,
                  x_out, h_out, lg_out):
    _mix_tail(o_ref[...], wo_ref, x_ref, gm_ref, g_ref, sh_ref, sc_ref, wr_ref, br_ref,
              x_out, h_out, lg_out)


def _mixer_tail(mix_inputs, w_o, x, gate_m, g_ffn, shift_f, scale_f, w_r, b_r, seq, merged):
    t, d = x.shape
    tpb = seq // ROW_TILE
    row = lambda i: (i, 0)
    full = lambda i: (0, 0)
    per_b = lambda i: (i // tpb, 0, 0)
    tail_specs = [
        pl.BlockSpec((d, d), full),
        pl.BlockSpec((ROW_TILE, d), row),
        pl.BlockSpec((1, 1, d), per_b),
        pl.BlockSpec((1, d), full),
        pl.BlockSpec((1, 1, d), per_b),
        pl.BlockSpec((1, 1, d), per_b),
        pl.BlockSpec((d, N_EXPERTS), full),
        pl.BlockSpec((1, N_EXPERTS), full),
    ]
    tail_args = (w_o, x, gate_m, g_ffn.reshape(1, d), shift_f, scale_f, w_r, b_r.reshape(1, N_EXPERTS))
    if merged:
        o0, o1, o2, lse = mix_inputs
        expand = jnp.asarray(np.kron(np.eye(N_HEADS, dtype=np.float32),
                                     np.ones((1, HEAD_DIM), np.float32)), BF16)
        kern = _merge_oproj_kernel
        specs = [pl.BlockSpec((ROW_TILE, d), row)] * 3 + [
            pl.BlockSpec((ROW_TILE, 3 * N_HEADS), row), pl.BlockSpec((N_HEADS, d), full)]
        args = (o0, o1, o2, lse, expand)
    else:
        kern = _oproj_kernel
        specs = [pl.BlockSpec((ROW_TILE, d), row)]
        args = mix_inputs
    return pl.pallas_call(
        kern,
        out_shape=(jax.ShapeDtypeStruct((t, d), F32), jax.ShapeDtypeStruct((t, d), BF16),
                   jax.ShapeDtypeStruct((t, N_EXPERTS), F32)),
        grid=(t // ROW_TILE,),
        in_specs=specs + tail_specs,
        out_specs=(pl.BlockSpec((ROW_TILE, d), row), pl.BlockSpec((ROW_TILE, d), row),
                   pl.BlockSpec((ROW_TILE, N_EXPERTS), row)),
        compiler_params=_params(("parallel",)),
        name="mixer_tail_merge" if merged else "mixer_tail",
    )(*args, *tail_args)


def _fox_kernel(q_ref, k_ref, v_ref, o_ref, vt_scr):
    qi = pl.program_id(2)
    seq = k_ref.shape[0]
    nk = seq // FOX_TILE
    half = HEAD_DIM

    @pl.when(qi == 0)
    def _():
        lane = lax.broadcasted_iota(jnp.int32, (FOX_TILE, 2 * half), 1)
        for j in range(nk):
            vj = v_ref[j * FOX_TILE:(j + 1) * FOX_TILE, :].astype(F32)
            vt_scr[j, :, :FOX_TILE] = jnp.where(lane < half, vj, 0.0).T.astype(BF16)
            vt_scr[j, :, FOX_TILE:] = jnp.where(lane >= half, vj, 0.0).T.astype(BF16)

    q0 = q_ref[:, :AUG]
    q1 = q_ref[:, AUG:]

    def softmax_step(s, m, l):
        m_new = jnp.maximum(m, jnp.max(s, axis=0, keepdims=True))
        alpha = jnp.exp(m - m_new)
        p = jnp.exp(s - m_new)
        return m_new, alpha * l + jnp.sum(p, axis=0, keepdims=True), alpha, p.astype(BF16)

    def step(kj, carry, masked):
        m0, l0, m1, l1, acc = carry
        rows = pl.ds(pl.multiple_of(kj * FOX_TILE, FOX_TILE), FOX_TILE)
        s0 = _dot_nt(k_ref[rows, :AUG], q0)
        s1 = _dot_nt(k_ref[rows, AUG:], q1)
        if masked:
            kpos = lax.broadcasted_iota(jnp.int32, (FOX_TILE, FOX_TILE), 0)
            qpos = lax.broadcasted_iota(jnp.int32, (FOX_TILE, FOX_TILE), 1)
            keep = kpos <= qpos
            s0 = jnp.where(keep, s0, NEG)
            s1 = jnp.where(keep, s1, NEG)
        m0, l0, a0, p0 = softmax_step(s0, m0, l0)
        m1, l1, a1, p1 = softmax_step(s1, m1, l1)
        pv = _dot(vt_scr[kj], jnp.concatenate([p0, p1], axis=0))
        alpha = jnp.concatenate([jnp.broadcast_to(a0, (half, FOX_TILE)),
                                 jnp.broadcast_to(a1, (half, FOX_TILE))], axis=0)
        return m0, l0, m1, l1, alpha * acc + pv

    neg = jnp.full((1, FOX_TILE), NEG, F32)
    zero = jnp.zeros((1, FOX_TILE), F32)
    init = (neg, zero, neg, zero, jnp.zeros((2 * half, FOX_TILE), F32))
    carry = lax.fori_loop(0, qi, lambda kj, c: step(kj, c, False), init)
    _, l0, _, l1, acc = step(qi, carry, True)
    inv = jnp.concatenate([jnp.broadcast_to(1.0 / l0, (half, FOX_TILE)),
                           jnp.broadcast_to(1.0 / l1, (half, FOX_TILE))], axis=0)
    o_ref[...] = (acc * inv).T.astype(o_ref.dtype)


def _fox(q_aug, k_aug, v, n_batch, seq):
    t = q_aug.shape[0]
    nq = seq // FOX_TILE
    pairs = N_HEADS // 2
    return pl.pallas_call(
        _fox_kernel,
        out_shape=jax.ShapeDtypeStruct((t, N_HEADS * HEAD_DIM), BF16),
        grid=(n_batch, pairs, nq),
        in_specs=[
            pl.BlockSpec((FOX_TILE, 2 * AUG), lambda b, p, qi: (b * nq + qi, p)),
            pl.BlockSpec((seq, 2 * AUG), lambda b, p, qi: (b, p)),
            pl.BlockSpec((seq, 2 * HEAD_DIM), lambda b, p, qi: (b, p)),
        ],
        out_specs=pl.BlockSpec((FOX_TILE, 2 * HEAD_DIM), lambda b, p, qi: (b * nq + qi, p)),
        scratch_shapes=[pltpu.VMEM((nq, 2 * HEAD_DIM, 2 * FOX_TILE), BF16)],
        compiler_params=_params(("parallel", "parallel", "arbitrary")),
        name="fox",
    )(q_aug, k_aug, v)


def _expert_kernel(be_ref, nu_ref, x_ref, wgu_ref, bgu_ref, wd_ref, bd_ref, gate_ref, o_ref, wgu_bf, wd_bf):
    i = pl.program_id(0)
    d_ff = wd_ref.shape[1]

    @pl.when(jnp.logical_or(i == 0, be_ref[i] != be_ref[jnp.maximum(i - 1, 0)]))
    def _():
        wgu_bf[...] = wgu_ref[0].astype(BF16)
        wd_bf[...] = wd_ref[0].astype(BF16)

    @pl.when(i < nu_ref[0])
    def _():
        gu = _dot(x_ref[...], wgu_bf[...]) + bgu_ref[0]
        g = jnp.minimum(gu[:, :d_ff], SWIGLU_LIMIT)
        u = jnp.clip(gu[:, d_ff:], -SWIGLU_LIMIT, SWIGLU_LIMIT)
        act = (u + 1.0) * g * (1.0 / (1.0 + jnp.exp(-SWIGLU_ALPHA * g)))
        y = _dot(act.astype(BF16), wd_bf[...]) + bd_ref[0]
        o_ref[...] = (y * gate_ref[...]).astype(o_ref.dtype)

    @pl.when(i >= nu_ref[0])
    def _():
        o_ref[...] = jnp.zeros_like(o_ref)


def _experts(xbuf, w_gu, b_gu, w_down, b_down, gate_rows, block_expert, n_used):
    n_rows, d = xbuf.shape
    e, _, n_gu = w_gu.shape
    d_ff = w_down.shape[1]
    n_blocks = n_rows // MOE_TILE
    grid_spec = pltpu.PrefetchScalarGridSpec(
        num_scalar_prefetch=2,
        grid=(n_blocks,),
        in_specs=[
            pl.BlockSpec((MOE_TILE, d), lambda i, be, nu: (i, 0)),
            pl.BlockSpec((1, d, n_gu), lambda i, be, nu: (be[i], 0, 0)),
            pl.BlockSpec((1, 1, n_gu), lambda i, be, nu: (be[i], 0, 0)),
            pl.BlockSpec((1, d_ff, d), lambda i, be, nu: (be[i], 0, 0)),
            pl.BlockSpec((1, 1, d), lambda i, be, nu: (be[i], 0, 0)),
            pl.BlockSpec((MOE_TILE, 1), lambda i, be, nu: (i, 0)),
        ],
        out_specs=pl.BlockSpec((MOE_TILE, d), lambda i, be, nu: (i, 0)),
        scratch_shapes=[pltpu.VMEM((d, n_gu), BF16), pltpu.VMEM((d_ff, d), BF16)],
    )
    return pl.pallas_call(
        _expert_kernel,
        out_shape=jax.ShapeDtypeStruct((n_rows, d), BF16),
        grid_spec=grid_spec,
        compiler_params=_params(("arbitrary",)),
        name="experts",
    )(block_expert, n_used, xbuf, w_gu, b_gu.reshape(e, 1, n_gu), w_down, b_down.reshape(e, 1, d), gate_rows)


def _route(logits):
    t = logits.shape[0]
    n_slots = t * TOP_K
    n_rows = n_slots + N_EXPERTS * MOE_TILE
    top_val, top_idx = lax.top_k(logits, TOP_K)
    gates = jax.nn.softmax(top_val, axis=-1)
    e_flat = top_idx.reshape(-1).astype(jnp.int32)
    onehot = (e_flat[:, None] == jnp.arange(N_EXPERTS, dtype=jnp.int32)[None, :]).astype(jnp.int32)
    csum = jnp.cumsum(onehot, axis=0)
    rank = jnp.sum((csum - onehot) * onehot, axis=1)
    counts = csum[-1]
    padded = (counts + MOE_TILE - 1) // MOE_TILE * MOE_TILE
    pad_ends = jnp.cumsum(padded)
    pad_starts = pad_ends - padded
    dest = pad_starts[e_flat] + rank
    tok_flat = jnp.arange(n_slots, dtype=jnp.int32) // TOP_K
    tok_rows = jnp.zeros((n_rows,), jnp.int32).at[dest].set(tok_flat)
    gate_rows = jnp.zeros((n_rows,), F32).at[dest].set(gates.reshape(-1))
    block_start = jnp.arange(n_rows // MOE_TILE, dtype=jnp.int32) * MOE_TILE
    block_expert = jnp.minimum(jnp.searchsorted(pad_ends, block_start, side='right'),
                               N_EXPERTS - 1).astype(jnp.int32)
    n_used = (pad_ends[-1] // MOE_TILE).astype(jnp.int32).reshape(1)
    return dest, tok_rows, gate_rows.reshape(n_rows, 1), block_expert, n_used


def _combine_kernel(x_ref, y_ref, gf_ref, g_ref, o_ref, *, final):
    d = x_ref.shape[1]
    y = y_ref[...]
    moe = (y[:, 0:d].astype(F32) + y[:, d:2 * d].astype(F32)
           + y[:, 2 * d:3 * d].astype(F32) + y[:, 3 * d:4 * d].astype(F32))
    x_new = x_ref[...] + gf_ref[0] * moe
    if final:
        x_new = _norm_mod(x_new, g_ref[...], None, None)
    o_ref[...] = x_new


def _combine(x, y_slots, gate_f, g_final, seq, final):
    t, d = x.shape
    tpb = seq // ROW_TILE
    return pl.pallas_call(
        functools.partial(_combine_kernel, final=final),
        out_shape=jax.ShapeDtypeStruct((t, d), F32),
        grid=(t // ROW_TILE,),
        in_specs=[
            pl.BlockSpec((ROW_TILE, d), lambda i: (i, 0)),
            pl.BlockSpec((ROW_TILE, TOP_K * d), lambda i: (i, 0)),
            pl.BlockSpec((1, 1, d), lambda i: (i // tpb, 0, 0)),
            pl.BlockSpec((1, d), lambda i: (0, 0)),
        ],
        out_specs=pl.BlockSpec((ROW_TILE, d), lambda i: (i, 0)),
        compiler_params=_params(("parallel",)),
        name="combine_final" if final else "combine",
    )(x, y_slots, gate_f, g_final.reshape(1, d))


def _moe(x, h, logits, gate_f, w_gu, b_gu, w_down, b_down, g_final, seq, final):
    t, d = x.shape
    dest, tok_rows, gate_rows, block_expert, n_used = _route(logits)
    xbuf = jnp.take(h, tok_rows, axis=0)
    ybuf = _experts(xbuf, w_gu, b_gu, w_down, b_down, gate_rows, block_expert, n_used)
    y_slots = jnp.take(ybuf, dest, axis=0).reshape(t, TOP_K * d)
    return _combine(x, y_slots, gate_f, g_final, seq, final)


def kernel(x, c, ada_w, ada_b, norm_mix_g, norm_ffn_g, a_w_qkv, a_w_o, rel_bias, kv_norm_g, w_kvf, b_f,
           b_w_q, b_w_o, router_w, router_b, w_gu, b_gu, w_down, b_down, final_norm_g):
    n_batch, seq, d = x.shape
    t = n_batch * seq
    width = N_HEADS * HEAD_DIM
    xf = x.reshape(t, d)

    ada = _ada(c, ada_w, ada_b)
    mods = [[ada[l, :, i * d:(i + 1) * d].reshape(n_batch, 1, d) for i in range(6)] for l in range(2)]

    shift_m, scale_m, gate_m, shift_f, scale_f, gate_f = mods[0]
    qscale = np.ones((3, 3, 1, 1), np.float32)
    qscale[:, 0] = HEAD_DIM ** -0.5
    w_qkv = (a_w_qkv[0].reshape(d, 3, 3, N_HEADS, HEAD_DIM) * qscale).reshape(d, 9 * width).astype(BF16)
    qkv = _proj(xf, norm_mix_g[0], shift_m, scale_m, w_qkv, seq, 1536)

    qi = np.arange(DIL_BLOCK, dtype=np.int32)[:, None]
    kj = np.arange(2 * DIL_BLOCK, dtype=np.int32)[None, :]
    delta = np.clip(qi + DIL_BLOCK - kj, 0, None)
    outs, lses = [], []
    for g, (window, dilation) in enumerate(DIL_PAIRS):
        assert window // dilation == DIL_BLOCK
        bucket = _t5_bucket_np(delta * dilation)
        tab = rel_bias[:, g * N_HEADS:(g + 1) * N_HEADS].astype(F32)
        bias = jnp.transpose(tab[bucket], (2, 0, 1))
        o, lse = _dilated_group(qkv, bias, g, dilation, n_batch, seq)
        outs.append(o)
        lses.append(lse)
    lse_all = jnp.concatenate(lses, axis=1)
    x1, h1, logits1 = _mixer_tail((outs[0], outs[1], outs[2], lse_all), a_w_o[0].astype(BF16), xf, gate_m,
                                  norm_ffn_g[0], shift_f, scale_f, router_w[0], router_b[0], seq, True)
    x2 = _moe(x1, h1, logits1, gate_f, w_gu[0], b_gu[0], w_down[0], b_down[0], final_norm_g, seq, False)

    w_f = jnp.pad(jnp.tile(w_kvf[:, 2 * width:], (1, 3)), ((0, 0), (0, 128 - 3 * N_HEADS))).astype(BF16)
    b_fp = jnp.pad(jnp.tile(b_f, 3), (0, 128 - 3 * N_HEADS)).reshape(1, 128)
    k_aug, v_sh, fcum = _kvf(x2, kv_norm_g, _aug_weight(w_kvf[:, :width]), w_kvf[:, width:2 * width].astype(BF16),
                             w_f, b_fp, n_batch)

    shift_m, scale_m, gate_m, shift_f, scale_f, gate_f = mods[1]
    q_aug = _qaug(x2, norm_mix_g[1], shift_m, scale_m, _aug_weight(b_w_q[0] * (HEAD_DIM ** -0.5)), fcum, seq)
    o1 = _fox(q_aug, k_aug, v_sh, n_batch, seq)

    x3, h3, logits3 = _mixer_tail((o1,), b_w_o[0].astype(BF16), x2, gate_m, norm_ffn_g[1], shift_f, scale_f,
                                  router_w[1], router_b[1], seq, False)
    out = _moe(x3, h3, logits3, gate_f, w_gu[1], b_gu[1], w_down[1], b_down[1], final_norm_g, seq, True)
    return out.reshape(n_batch, seq, d)
```

```python
import functools
import math

import numpy as np
import jax
import jax.numpy as jnp
from jax import lax
from jax.experimental import pallas as pl
from jax.experimental.pallas import tpu as pltpu

F32 = jnp.float32
BF16 = jnp.bfloat16

D_MODEL = 1024
HEAD_DIM = 64
N_HEADS = 16
DIL_PAIRS = ((128, 1), (512, 4), (2048, 16))
DIL_BLOCK = 128
NUM_BUCKETS = 32
MAX_DISTANCE = 2048
N_EXPERTS = 32
TOP_K = 4
SWIGLU_LIMIT = 7.0
SWIGLU_ALPHA = 1.702
RMS_EPS = 1e-6
NEG = -1e30

ROW_TILE = 512
MOE_TILE = 256
FOX_TILE = 512
AUG = 128
VMEM_LIMIT = 56 * 1024 * 1024


def _params(sem, vmem=VMEM_LIMIT):
    return pltpu.CompilerParams(dimension_semantics=sem, vmem_limit_bytes=vmem)


def _dot(a, b):
    return jnp.dot(a, b, preferred_element_type=F32)


def _dot_nt(a, b):
    return lax.dot_general(a, b, (((1,), (1,)), ((), ())), preferred_element_type=F32)


def _dot_tn(a, b):
    return lax.dot_general(a, b, (((0,), (0,)), ((), ())), preferred_element_type=F32)


def _norm_mod(x, g, shift, scale):
    ms = jnp.mean(x * x, axis=-1, keepdims=True)
    y = x * lax.rsqrt(ms + RMS_EPS) * g
    if scale is not None:
        y = y * (1.0 + scale) + shift
    return y


def _ada_kernel(c_ref, w_ref, b_ref, o_ref):
    c = c_ref[...]
    act = c * (1.0 / (1.0 + jnp.exp(-c)))
    o_ref[0] = jnp.dot(act, w_ref[0], preferred_element_type=F32,
                       precision=lax.Precision.HIGHEST) + b_ref[0]


def _ada(c, ada_w, ada_b):
    depth, d, n = ada_w.shape
    bsz = c.shape[0]
    tn = 1536
    return pl.pallas_call(
        _ada_kernel,
        out_shape=jax.ShapeDtypeStruct((depth, bsz, n), F32),
        grid=(depth, n // tn),
        in_specs=[
            pl.BlockSpec((bsz, d), lambda l, j: (0, 0)),
            pl.BlockSpec((1, d, tn), lambda l, j: (l, 0, j)),
            pl.BlockSpec((1, 1, tn), lambda l, j: (l, 0, j)),
        ],
        out_specs=pl.BlockSpec((1, bsz, tn), lambda l, j: (l, 0, j)),
        compiler_params=_params(("parallel", "parallel")),
        name="ada",
    )(c, ada_w, ada_b.reshape(depth, 1, n))


def _perm_tile(dilation):
    return max(ROW_TILE, DIL_BLOCK * dilation)


def _proj_kernel(x_ref, g_ref, sh_ref, sc_ref, w_ref, o_ref, h_scr, *xs_scr, dilation):
    @pl.when(pl.program_id(1) == 0)
    def _():
        if dilation == 1:
            h_scr[...] = _norm_mod(x_ref[...], g_ref[...], sh_ref[0], sc_ref[0]).astype(BF16)
        else:
            (xs,) = xs_scr
            n_lane = xs.shape[0]
            for c in range(n_lane):
                xs[c] = x_ref[:, c * 128:(c + 1) * 128]
            chunk = x_ref.shape[0] // dilation
            for r in range(dilation):
                xr = jnp.concatenate([xs[c, pl.ds(r, chunk, stride=dilation), :] for c in range(n_lane)], axis=1)
                h_scr[r * chunk:(r + 1) * chunk, :] = _norm_mod(xr, g_ref[...], sh_ref[0], sc_ref[0]).astype(BF16)

    o_ref[...] = _dot(h_scr[...], w_ref[...]).astype(o_ref.dtype)


def _proj(x, g, shift, scale, w, seq, dilation, group):
    t, d = x.shape
    n = w.shape[1]
    tm = _perm_tile(dilation)
    tn = d
    tpb = seq // tm
    return pl.pallas_call(
        functools.partial(_proj_kernel, dilation=dilation),
        out_shape=jax.ShapeDtypeStruct((t, n), BF16),
        grid=(t // tm, n // tn),
        in_specs=[
            pl.BlockSpec((tm, d), lambda i, j: (i, 0)),
            pl.BlockSpec((1, d), lambda i, j: (0, 0)),
            pl.BlockSpec((1, 1, d), lambda i, j: (i // tpb, 0, 0)),
            pl.BlockSpec((1, 1, d), lambda i, j: (i // tpb, 0, 0)),
            pl.BlockSpec((d, tn), lambda i, j: (0, j)),
        ],
        out_specs=pl.BlockSpec((tm, tn), lambda i, j: (i, j)),
        scratch_shapes=[pltpu.VMEM((tm, d), BF16)] + (
            [pltpu.VMEM((d // 128, tm, 128), F32)] if dilation > 1 else []),
        compiler_params=_params(("parallel", "arbitrary")),
        name=f"proj{group}",
    )(x, g.reshape(1, d), shift, scale, w)


def _split3(f):
    def top(v):
        return lax.bitcast_convert_type(lax.bitcast_convert_type(v, jnp.uint32) & jnp.uint32(0xFFFF0000), F32)
    hi = top(f)
    r1 = f - hi
    mid = top(r1)
    return hi, mid, r1 - mid


def _forget_pieces(fc):
    hi, mid, lo = _split3(fc)
    lane = lax.broadcasted_iota(jnp.int32, fc.shape, 1)
    x = jnp.where(lane < N_HEADS, hi, jnp.where(lane < 2 * N_HEADS, mid, jnp.where(lane < 3 * N_HEADS, lo, 0.0)))
    return x.astype(BF16)


def _aug_tables(key_side):
    place = np.zeros((128, N_HEADS * AUG), np.float32)
    const = np.zeros((1, N_HEADS * AUG), np.float32)
    for h in range(N_HEADS):
        base = h * AUG + HEAD_DIM
        for piece in range(3):
            if key_side:
                place[piece * N_HEADS + h, base + piece] = 1.0
                const[0, base + 3 + piece] = 1.0
            else:
                place[piece * N_HEADS + h, base + 3 + piece] = 1.0
                const[0, base + piece] = -1.0
    return jnp.asarray(place, BF16), jnp.asarray(const, F32)


def _aug_weight(w):
    d = w.shape[0]
    w = jnp.pad(w.reshape(d, N_HEADS, HEAD_DIM), ((0, 0), (0, 0), (0, AUG - HEAD_DIM)))
    return w.reshape(d, N_HEADS * AUG).astype(BF16)


def _kvf_kernel(x_ref, g_ref, wk_ref, wv_ref, wf_ref, bf_ref, tri_ref, pl_ref, cr_ref,
                k_ref, v_ref, f_ref, carry, *, tpb):
    h = _norm_mod(x_ref[...], g_ref[...], None, None).astype(BF16)
    v_ref[...] = _dot(h, wv_ref[...]).astype(v_ref.dtype)
    z = _dot(h, wf_ref[...]) + bf_ref[...]
    lf = jnp.minimum(z, 0.0) - jnp.log(1.0 + jnp.exp(-jnp.abs(z)))
    hi, mid, lo = _split3(lf)
    tri = tri_ref[...]
    cs = _dot(tri, hi.astype(BF16)) + _dot(tri, mid.astype(BF16)) + _dot(tri, lo.astype(BF16))

    @pl.when(pl.program_id(0) % tpb == 0)
    def _():
        carry[...] = jnp.zeros_like(carry)

    cs = cs + carry[...]
    f_ref[...] = cs
    carry[...] = cs[ROW_TILE - 1:ROW_TILE, :]
    k_ref[...] = (_dot(h, wk_ref[...]) + _dot(_forget_pieces(cs), pl_ref[...]) + cr_ref[...]).astype(k_ref.dtype)


def _kvf(x, g, w_k, w_v, w_f, b_f, n_batch):
    t, d = x.shape
    n_aug = N_HEADS * AUG
    tri = jnp.asarray(np.tril(np.ones((ROW_TILE, ROW_TILE), np.float32)), BF16)
    place, const = _aug_tables(True)
    assert (t // ROW_TILE) % n_batch == 0
    full = lambda i: (0, 0)
    row = lambda i: (i, 0)
    return pl.pallas_call(
        functools.partial(_kvf_kernel, tpb=t // ROW_TILE // n_batch),
        out_shape=(jax.ShapeDtypeStruct((t, n_aug), BF16), jax.ShapeDtypeStruct((t, d), BF16),
                   jax.ShapeDtypeStruct((t, 128), F32)),
        grid=(t // ROW_TILE,),
        in_specs=[
            pl.BlockSpec((ROW_TILE, d), row),
            pl.BlockSpec((1, d), full),
            pl.BlockSpec((d, n_aug), full),
            pl.BlockSpec((d, d), full),
            pl.BlockSpec((d, 128), full),
            pl.BlockSpec((1, 128), full),
            pl.BlockSpec((ROW_TILE, ROW_TILE), full),
            pl.BlockSpec((128, n_aug), full),
            pl.BlockSpec((1, n_aug), full),
        ],
        out_specs=(pl.BlockSpec((ROW_TILE, n_aug), row), pl.BlockSpec((ROW_TILE, d), row),
                   pl.BlockSpec((ROW_TILE, 128), row)),
        scratch_shapes=[pltpu.VMEM((1, 128), F32)],
        compiler_params=_params(("arbitrary",)),
        name="kvf",
    )(x, g.reshape(1, d), w_k, w_v, w_f, b_f, tri, place, const)


def _qaug_kernel(x_ref, g_ref, sh_ref, sc_ref, w_ref, f_ref, pl_ref, cr_ref, o_ref):
    h = _norm_mod(x_ref[...], g_ref[...], sh_ref[0], sc_ref[0]).astype(BF16)
    o_ref[...] = (_dot(h, w_ref[...]) + _dot(_forget_pieces(f_ref[...]), pl_ref[...])
                  + cr_ref[...]).astype(o_ref.dtype)


def _qaug(x, g, shift, scale, w, fcum, seq):
    t, d = x.shape
    n_aug = N_HEADS * AUG
    tpb = seq // ROW_TILE
    place, const = _aug_tables(False)
    full = lambda i: (0, 0)
    row = lambda i: (i, 0)
    per_b = lambda i: (i // tpb, 0, 0)
    return pl.pallas_call(
        _qaug_kernel,
        out_shape=jax.ShapeDtypeStruct((t, n_aug), BF16),
        grid=(t // ROW_TILE,),
        in_specs=[
            pl.BlockSpec((ROW_TILE, d), row),
            pl.BlockSpec((1, d), full),
            pl.BlockSpec((1, 1, d), per_b),
            pl.BlockSpec((1, 1, d), per_b),
            pl.BlockSpec((d, n_aug), full),
            pl.BlockSpec((ROW_TILE, 128), row),
            pl.BlockSpec((128, n_aug), full),
            pl.BlockSpec((1, n_aug), full),
        ],
        out_specs=pl.BlockSpec((ROW_TILE, n_aug), row),
        compiler_params=_params(("parallel",)),
        name="qaug",
    )(x, g.reshape(1, d), shift, scale, w, fcum, place, const)


def _dil_kernel(q_ref, kp_ref, kc_ref, vp_ref, vc_ref, bias_ref, o_ref, lse_ref):
    table = jnp.minimum(pl.program_id(2), 1)
    pair_w = 2 * HEAD_DIM
    lane = lax.broadcasted_iota(jnp.int32, (DIL_BLOCK, pair_w), 1)
    first = lane < HEAD_DIM
    first2 = lax.broadcasted_iota(jnp.int32, (2 * DIL_BLOCK, pair_w), 1) < HEAD_DIM
    lse_tile = jnp.zeros((DIL_BLOCK, pair_w), F32)
    zero = jnp.zeros((), BF16)
    for pair in range(N_HEADS // 2):
        sl = slice(pair * pair_w, (pair + 1) * pair_w)
        q = q_ref[:, sl]
        k2 = jnp.concatenate([kp_ref[:, sl], kc_ref[:, sl]], axis=0)
        v2 = jnp.concatenate([vp_ref[:, sl], vc_ref[:, sl]], axis=0)
        probs, invs = [], []
        for which in range(2):
            h = 2 * pair + which
            qh = jnp.where(first, q, zero) if which == 0 else jnp.where(first, zero, q)
            s = _dot_nt(qh, k2) + bias_ref[table, h]
            m = jnp.max(s, axis=-1, keepdims=True)
            p = jnp.exp(s - m)
            den = jnp.sum(p, axis=-1, keepdims=True)
            probs.append(p.astype(BF16))
            invs.append(1.0 / den)
            lse_tile = jnp.where(lane == h, m + jnp.log(den), lse_tile)
        vcat = jnp.concatenate([jnp.where(first2, v2, zero), jnp.where(first2, zero, v2)], axis=0)
        o = _dot(jnp.concatenate(probs, axis=1), vcat)
        o_ref[:, sl] = (o * jnp.where(first, invs[0], invs[1])).astype(o_ref.dtype)
    lse_ref[...] = lse_tile


def _dilated_group(qkv, bias, group, dilation, n_batch, seq):
    t = qkv.shape[0]
    width = N_HEADS * HEAD_DIM
    nb = seq // dilation // DIL_BLOCK
    bpb = seq // DIL_BLOCK

    def blk(b, r, n):
        return b * bpb + n * dilation + r

    def spec(part, prev):
        if prev:
            return pl.BlockSpec((DIL_BLOCK, width), lambda b, r, n: (blk(b, r, jnp.maximum(n - 1, 0)), part))
        return pl.BlockSpec((DIL_BLOCK, width), lambda b, r, n: (blk(b, r, n), part))

    return pl.pallas_call(
        _dil_kernel,
        out_shape=(jax.ShapeDtypeStruct((t, width), BF16), jax.ShapeDtypeStruct((t, 2 * HEAD_DIM), F32)),
        grid=(n_batch, dilation, nb),
        in_specs=[spec(0, False), spec(1, True), spec(1, False), spec(2, True), spec(2, False),
                  pl.BlockSpec((2, N_HEADS, DIL_BLOCK, 2 * DIL_BLOCK), lambda b, r, n: (0, 0, 0, 0))],
        out_specs=(pl.BlockSpec((DIL_BLOCK, width), lambda b, r, n: (blk(b, r, n), 0)),
                   pl.BlockSpec((DIL_BLOCK, 2 * HEAD_DIM), lambda b, r, n: (blk(b, r, n), 0))),
        compiler_params=_params(("parallel", "parallel", "arbitrary")),
        name=f"dilated{group}",
    )(qkv, qkv, qkv, qkv, qkv, bias)


def _unpermute_rows(a, dilation):
    if dilation == 1:
        return a
    t, c = a.shape
    tm = _perm_tile(dilation)
    return a.reshape(t // tm, dilation, tm // dilation, c).transpose(0, 2, 1, 3).reshape(t, c)


def _perm_matrix(rows, dilation):
    per = rows // dilation
    p = np.zeros((rows, rows), np.float32)
    for r in range(dilation):
        for n in range(per):
            p[n * dilation + r, r * per + n] = 1.0
    return jnp.asarray(p, BF16)


def _t5_bucket_np(n):
    max_exact = NUM_BUCKETS // 2
    nf = np.maximum(n, 1).astype(np.float32)
    large = max_exact + (np.log(nf / np.float32(max_exact)) / np.float32(math.log(MAX_DISTANCE / max_exact))
                         * np.float32(NUM_BUCKETS - max_exact)).astype(np.int32)
    large = np.minimum(large, NUM_BUCKETS - 1)
    return np.where(n < max_exact, n, large)


def _mix_tail(mix_in, wo_ref, x_ref, gm_ref, g_ref, sh_ref, sc_ref, wr_ref, br_ref,
              x_out, h_out, lg_out):
    mix = _dot(mix_in, wo_ref[...])
    x_new = x_ref[...] + gm_ref[0] * mix
    x_out[...] = x_new
    h = _norm_mod(x_new, g_ref[...], sh_ref[0], sc_ref[0])
    h_out[...] = h.astype(BF16)
    lg_out[...] = jnp.dot(h, wr_ref[...], preferred_element_type=F32,
                          precision=lax.Precision.HIGHEST) + br_ref[...]


def _merge_oproj_kernel(o0_ref, o1_ref, o2_ref, p1_ref, p2_ref, lse_ref, ex_ref, wo_ref, x_ref, gm_ref, g_ref,
                        sh_ref, sc_ref, wr_ref, br_ref, x_out, h_out, lg_out):
    lse = lse_ref[...]
    l0, l1, l2 = lse[:, 0:16], lse[:, 16:32], lse[:, 32:48]
    m = jnp.maximum(jnp.maximum(l0, l1), l2)
    e0, e1, e2 = jnp.exp(l0 - m), jnp.exp(l1 - m), jnp.exp(l2 - m)
    inv = 1.0 / (e0 + e1 + e2)
    ex = ex_ref[...]
    o1 = _dot(p1_ref[...], o1_ref[...])
    o2 = _dot(p2_ref[...], o2_ref[0].reshape(ROW_TILE, o2_ref.shape[-1]))
    merged = (_dot((e0 * inv).astype(BF16), ex) * o0_ref[...].astype(F32)
              + _dot((e1 * inv).astype(BF16), ex) * o1
              + _dot((e2 * inv).astype(BF16), ex) * o2)
    _mix_tail(merged.astype(BF16), wo_ref, x_ref, gm_ref, g_ref, sh_ref, sc_ref, wr_ref, br_ref,
              x_out, h_out, lg_out)


def _oproj_kernel(o_ref, wo_ref, x_ref, gm_ref, g_ref, sh_ref, sc_ref, wr_ref, br_ref,
                  x_out, h_out, lg_out):
    _mix_tail(o_ref[...], wo_ref, x_ref, gm_ref, g_ref, sh_ref, sc_ref, wr_ref, br_ref,
              x_out, h_out, lg_out)


def _mixer_tail(mix_inputs, w_o, x, gate_m, g_ffn, shift_f, scale_f, w_r, b_r, seq, merged):
    t, d = x.shape
    tpb = seq // ROW_TILE
    row = lambda i: (i, 0)
    full = lambda i: (0, 0)
    per_b = lambda i: (i // tpb, 0, 0)
    tail_specs = [
        pl.BlockSpec((d, d), full),
        pl.BlockSpec((ROW_TILE, d), row),
        pl.BlockSpec((1, 1, d), per_b),
        pl.BlockSpec((1, d), full),
        pl.BlockSpec((1, 1, d), per_b),
        pl.BlockSpec((1, 1, d), per_b),
        pl.BlockSpec((d, N_EXPERTS), full),
        pl.BlockSpec((1, N_EXPERTS), full),
    ]
    tail_args = (w_o, x, gate_m, g_ffn.reshape(1, d), shift_f, scale_f, w_r, b_r.reshape(1, N_EXPERTS))
    if merged:
        o0, o1, o2, lse = mix_inputs
        expand = jnp.asarray(np.kron(np.eye(N_HEADS, dtype=np.float32),
                                     np.ones((1, HEAD_DIM), np.float32)), BF16)
        d1, d2 = DIL_PAIRS[1][1], DIL_PAIRS[2][1]
        assert _perm_tile(d1) == ROW_TILE
        t2 = _perm_tile(d2)
        sub2 = ROW_TILE // d2
        o2v = o2.reshape(t // t2, d2, t2 // d2, d)
        per2 = t2 // ROW_TILE
        kern = _merge_oproj_kernel
        specs = [pl.BlockSpec((ROW_TILE, d), row), pl.BlockSpec((ROW_TILE, d), row),
                 pl.BlockSpec((1, d2, sub2, d), lambda i: (i // per2, 0, i % per2, 0)),
                 pl.BlockSpec((ROW_TILE, ROW_TILE), full), pl.BlockSpec((ROW_TILE, ROW_TILE), full),
                 pl.BlockSpec((ROW_TILE, 3 * N_HEADS), row), pl.BlockSpec((N_HEADS, d), full)]
        args = (o0, o1, o2v, _perm_matrix(ROW_TILE, d1), _perm_matrix(ROW_TILE, d2), lse, expand)
    else:
        kern = _oproj_kernel
        specs = [pl.BlockSpec((ROW_TILE, d), row)]
        args = mix_inputs
    return pl.pallas_call(
        kern,
        out_shape=(jax.ShapeDtypeStruct((t, d), F32), jax.ShapeDtypeStruct((t, d), BF16),
                   jax.ShapeDtypeStruct((t, N_EXPERTS), F32)),
        grid=(t // ROW_TILE,),
        in_specs=specs + tail_specs,
        out_specs=(pl.BlockSpec((ROW_TILE, d), row), pl.BlockSpec((ROW_TILE, d), row),
                   pl.BlockSpec((ROW_TILE, N_EXPERTS), row)),
        compiler_params=_params(("parallel",)),
        name="mixer_tail_merge" if merged else "mixer_tail",
    )(*args, *tail_args)


def _fox_kernel(q_ref, k_ref, v_ref, o_ref, vt_scr):
    qi = pl.program_id(2)
    seq = k_ref.shape[0]
    nk = seq // FOX_TILE
    half = HEAD_DIM

    @pl.when(qi == 0)
    def _():
        lane = lax.broadcasted_iota(jnp.int32, (FOX_TILE, 2 * half), 1)
        for j in range(nk):
            vj = v_ref[j * FOX_TILE:(j + 1) * FOX_TILE, :].astype(F32)
            vt_scr[j, :, :FOX_TILE] = jnp.where(lane < half, vj, 0.0).T.astype(BF16)
            vt_scr[j, :, FOX_TILE:] = jnp.where(lane >= half, vj, 0.0).T.astype(BF16)

    q0 = q_ref[:, :AUG]
    q1 = q_ref[:, AUG:]

    def softmax_step(s, m, l):
        m_new = jnp.maximum(m, jnp.max(s, axis=0, keepdims=True))
        alpha = jnp.exp(m - m_new)
        p = jnp.exp(s - m_new)
        return m_new, alpha * l + jnp.sum(p, axis=0, keepdims=True), alpha, p.astype(BF16)

    def step(kj, carry, masked):
        m0, l0, m1, l1, acc = carry
        rows = pl.ds(pl.multiple_of(kj * FOX_TILE, FOX_TILE), FOX_TILE)
        s0 = _dot_nt(k_ref[rows, :AUG], q0)
        s1 = _dot_nt(k_ref[rows, AUG:], q1)
        if masked:
            kpos = lax.broadcasted_iota(jnp.int32, (FOX_TILE, FOX_TILE), 0)
            qpos = lax.broadcasted_iota(jnp.int32, (FOX_TILE, FOX_TILE), 1)
            keep = kpos <= qpos
            s0 = jnp.where(keep, s0, NEG)
            s1 = jnp.where(keep, s1, NEG)
        m0, l0, a0, p0 = softmax_step(s0, m0, l0)
        m1, l1, a1, p1 = softmax_step(s1, m1, l1)
        pv = _dot(vt_scr[kj], jnp.concatenate([p0, p1], axis=0))
        alpha = jnp.concatenate([jnp.broadcast_to(a0, (half, FOX_TILE)),
                                 jnp.broadcast_to(a1, (half, FOX_TILE))], axis=0)
        return m0, l0, m1, l1, alpha * acc + pv

    neg = jnp.full((1, FOX_TILE), NEG, F32)
    zero = jnp.zeros((1, FOX_TILE), F32)
    init = (neg, zero, neg, zero, jnp.zeros((2 * half, FOX_TILE), F32))
    carry = lax.fori_loop(0, qi, lambda kj, c: step(kj, c, False), init)
    _, l0, _, l1, acc = step(qi, carry, True)
    inv = jnp.concatenate([jnp.broadcast_to(1.0 / l0, (half, FOX_TILE)),
                           jnp.broadcast_to(1.0 / l1, (half, FOX_TILE))], axis=0)
    o_ref[...] = (acc * inv).T.astype(o_ref.dtype)


def _fox(q_aug, k_aug, v, n_batch, seq):
    t = q_aug.shape[0]
    nq = seq // FOX_TILE
    pairs = N_HEADS // 2
    return pl.pallas_call(
        _fox_kernel,
        out_shape=jax.ShapeDtypeStruct((t, N_HEADS * HEAD_DIM), BF16),
        grid=(n_batch, pairs, nq),
        in_specs=[
            pl.BlockSpec((FOX_TILE, 2 * AUG), lambda b, p, qi: (b * nq + qi, p)),
            pl.BlockSpec((seq, 2 * AUG), lambda b, p, qi: (b, p)),
            pl.BlockSpec((seq, 2 * HEAD_DIM), lambda b, p, qi: (b, p)),
        ],
        out_specs=pl.BlockSpec((FOX_TILE, 2 * HEAD_DIM), lambda b, p, qi: (b * nq + qi, p)),
        scratch_shapes=[pltpu.VMEM((nq, 2 * HEAD_DIM, 2 * FOX_TILE), BF16)],
        compiler_params=_params(("parallel", "parallel", "arbitrary")),
        name="fox",
    )(q_aug, k_aug, v)


def _expert_kernel(be_ref, nu_ref, x_ref, wgu_ref, bgu_ref, wd_ref, bd_ref, gate_ref, o_ref, wgu_bf, wd_bf):
    i = pl.program_id(0)
    d_ff = wd_ref.shape[1]

    @pl.when(jnp.logical_or(i == 0, be_ref[i] != be_ref[jnp.maximum(i - 1, 0)]))
    def _():
        wgu_bf[...] = wgu_ref[0].astype(BF16)
        wd_bf[...] = wd_ref[0].astype(BF16)

    @pl.when(i < nu_ref[0])
    def _():
        gu = _dot(x_ref[...], wgu_bf[...]) + bgu_ref[0]
        g = jnp.minimum(gu[:, :d_ff], SWIGLU_LIMIT)
        u = jnp.clip(gu[:, d_ff:], -SWIGLU_LIMIT, SWIGLU_LIMIT)
        act = (u + 1.0) * g * (1.0 / (1.0 + jnp.exp(-SWIGLU_ALPHA * g)))
        y = _dot(act.astype(BF16), wd_bf[...]) + bd_ref[0]
        o_ref[...] = (y * gate_ref[...]).astype(o_ref.dtype)

    @pl.when(i >= nu_ref[0])
    def _():
        o_ref[...] = jnp.zeros_like(o_ref)


def _experts(xbuf, w_gu, b_gu, w_down, b_down, gate_rows, block_expert, n_used, layer):
    n_rows, d = xbuf.shape
    depth, e, _, n_gu = w_gu.shape
    d_ff = w_down.shape[2]
    n_blocks = n_rows // MOE_TILE
    grid_spec = pltpu.PrefetchScalarGridSpec(
        num_scalar_prefetch=2,
        grid=(n_blocks,),
        in_specs=[
            pl.BlockSpec((MOE_TILE, d), lambda i, be, nu: (i, 0)),
            pl.BlockSpec((None, 1, d, n_gu), lambda i, be, nu: (layer, be[i], 0, 0)),
            pl.BlockSpec((None, 1, 1, n_gu), lambda i, be, nu: (layer, be[i], 0, 0)),
            pl.BlockSpec((None, 1, d_ff, d), lambda i, be, nu: (layer, be[i], 0, 0)),
            pl.BlockSpec((None, 1, 1, d), lambda i, be, nu: (layer, be[i], 0, 0)),
            pl.BlockSpec((MOE_TILE, 1), lambda i, be, nu: (i, 0)),
        ],
        out_specs=pl.BlockSpec((MOE_TILE, d), lambda i, be, nu: (i, 0)),
        scratch_shapes=[pltpu.VMEM((d, n_gu), BF16), pltpu.VMEM((d_ff, d), BF16)],
    )
    return pl.pallas_call(
        _expert_kernel,
        out_shape=jax.ShapeDtypeStruct((n_rows, d), BF16),
        grid_spec=grid_spec,
        compiler_params=_params(("arbitrary",)),
        name="experts",
    )(block_expert, n_used, xbuf, w_gu, b_gu.reshape(depth, e, 1, n_gu), w_down, b_down.reshape(depth, e, 1, d),
      gate_rows)


def _route(logits):
    t = logits.shape[0]
    n_slots = t * TOP_K
    n_rows = n_slots + N_EXPERTS * MOE_TILE
    top_val, top_idx = lax.top_k(logits, TOP_K)
    gates = jax.nn.softmax(top_val, axis=-1)
    e_flat = top_idx.reshape(-1).astype(jnp.int32)
    onehot = (e_flat[:, None] == jnp.arange(N_EXPERTS, dtype=jnp.int32)[None, :]).astype(jnp.int32)
    csum = jnp.cumsum(onehot, axis=0)
    rank = jnp.sum((csum - onehot) * onehot, axis=1)
    counts = csum[-1]
    padded = (counts + MOE_TILE - 1) // MOE_TILE * MOE_TILE
    pad_ends = jnp.cumsum(padded)
    pad_starts = pad_ends - padded
    dest = pad_starts[e_flat] + rank
    tok_flat = jnp.arange(n_slots, dtype=jnp.int32) // TOP_K
    tok_rows = jnp.zeros((n_rows,), jnp.int32).at[dest].set(tok_flat)
    gate_rows = jnp.zeros((n_rows,), F32).at[dest].set(gates.reshape(-1))
    block_start = jnp.arange(n_rows // MOE_TILE, dtype=jnp.int32) * MOE_TILE
    block_expert = jnp.minimum(jnp.searchsorted(pad_ends, block_start, side='right'),
                               N_EXPERTS - 1).astype(jnp.int32)
    n_used = (pad_ends[-1] // MOE_TILE).astype(jnp.int32).reshape(1)
    return dest, tok_rows, gate_rows.reshape(n_rows, 1), block_expert, n_used


def _combine_kernel(x_ref, y_ref, gf_ref, g_ref, o_ref, *, final):
    d = x_ref.shape[1]
    y = y_ref[...]
    moe = (y[:, 0:d].astype(F32) + y[:, d:2 * d].astype(F32)
           + y[:, 2 * d:3 * d].astype(F32) + y[:, 3 * d:4 * d].astype(F32))
    x_new = x_ref[...] + gf_ref[0] * moe
    if final:
        x_new = _norm_mod(x_new, g_ref[...], None, None)
    o_ref[...] = x_new


def _combine(x, y_slots, gate_f, g_final, seq, final):
    t, d = x.shape
    tpb = seq // ROW_TILE
    return pl.pallas_call(
        functools.partial(_combine_kernel, final=final),
        out_shape=jax.ShapeDtypeStruct((t, d), F32),
        grid=(t // ROW_TILE,),
        in_specs=[
            pl.BlockSpec((ROW_TILE, d), lambda i: (i, 0)),
            pl.BlockSpec((ROW_TILE, TOP_K * d), lambda i: (i, 0)),
            pl.BlockSpec((1, 1, d), lambda i: (i // tpb, 0, 0)),
            pl.BlockSpec((1, d), lambda i: (0, 0)),
        ],
        out_specs=pl.BlockSpec((ROW_TILE, d), lambda i: (i, 0)),
        compiler_params=_params(("parallel",)),
        name="combine_final" if final else "combine",
    )(x, y_slots, gate_f, g_final.reshape(1, d))


def _moe(x, h, logits, gate_f, w_gu, b_gu, w_down, b_down, g_final, seq, layer, final):
    t, d = x.shape
    dest, tok_rows, gate_rows, block_expert, n_used = _route(logits)
    xbuf = jnp.take(h, tok_rows, axis=0)
    ybuf = _experts(xbuf, w_gu, b_gu, w_down, b_down, gate_rows, block_expert, n_used, layer)
    y_slots = jnp.take(ybuf, dest, axis=0).reshape(t, TOP_K * d)
    return _combine(x, y_slots, gate_f, g_final, seq, final)


def kernel(x, c, ada_w, ada_b, norm_mix_g, norm_ffn_g, a_w_qkv, a_w_o, rel_bias, kv_norm_g, w_kvf, b_f,
           b_w_q, b_w_o, router_w, router_b, w_gu, b_gu, w_down, b_down, final_norm_g):
    n_batch, seq, d = x.shape
    t = n_batch * seq
    width = N_HEADS * HEAD_DIM
    xf = x.reshape(t, d)

    ada = _ada(c, ada_w, ada_b)
    mods = [[ada[l, :, i * d:(i + 1) * d].reshape(n_batch, 1, d) for i in range(6)] for l in range(2)]

    shift_m, scale_m, gate_m, shift_f, scale_f, gate_f = mods[0]
    qscale = np.ones((3, 3, 1), np.float32)
    qscale[:, 0] = HEAD_DIM ** -0.5
    w_qkv = (a_w_qkv[0].reshape(d, 3, 3, width) * qscale).astype(BF16)

    qi = np.arange(DIL_BLOCK, dtype=np.int32)[:, None]
    kj = np.arange(2 * DIL_BLOCK, dtype=np.int32)[None, :]
    delta = qi + DIL_BLOCK - kj
    in_band = (delta >= 0) & (delta <= DIL_BLOCK)
    outs, lses = [], []
    for g, (window, dilation) in enumerate(DIL_PAIRS):
        assert window // dilation == DIL_BLOCK
        qkv = _proj(xf, norm_mix_g[0], shift_m, scale_m, w_qkv[:, g].reshape(d, 3 * width), seq, dilation, g)
        bucket = _t5_bucket_np(np.clip(delta, 0, None) * dilation)
        tab = rel_bias[:, g * N_HEADS:(g + 1) * N_HEADS].astype(F32)
        bias = jnp.where(in_band, jnp.transpose(tab[bucket], (2, 0, 1)), NEG)
        bias_first = jnp.where(kj >= DIL_BLOCK, bias, NEG)
        o, lse = _dilated_group(qkv, jnp.stack([bias_first, bias]), g, dilation, n_batch, seq)
        outs.append(o)
        lses.append(_unpermute_rows(lse, dilation)[:, :N_HEADS])
    lse_all = jnp.concatenate(lses, axis=1)
    x1, h1, logits1 = _mixer_tail((outs[0], outs[1], outs[2], lse_all), a_w_o[0].astype(BF16), xf, gate_m,
                                  norm_ffn_g[0], shift_f, scale_f, router_w[0], router_b[0], seq, True)
    x2 = _moe(x1, h1, logits1, gate_f, w_gu, b_gu, w_down, b_down, final_norm_g, seq, 0, False)

    w_f = jnp.pad(jnp.tile(w_kvf[:, 2 * width:], (1, 3)), ((0, 0), (0, 128 - 3 * N_HEADS))).astype(BF16)
    b_fp = jnp.pad(jnp.tile(b_f, 3), (0, 128 - 3 * N_HEADS)).reshape(1, 128)
    k_aug, v_sh, fcum = _kvf(x2, kv_norm_g, _aug_weight(w_kvf[:, :width]), w_kvf[:, width:2 * width].astype(BF16),
                             w_f, b_fp, n_batch)

    shift_m, scale_m, gate_m, shift_f, scale_f, gate_f = mods[1]
    q_aug = _qaug(x2, norm_mix_g[1], shift_m, scale_m, _aug_weight(b_w_q[0] * (HEAD_DIM ** -0.5)), fcum, seq)
    o1 = _fox(q_aug, k_aug, v_sh, n_batch, seq)

    x3, h3, logits3 = _mixer_tail((o1,), b_w_o[0].astype(BF16), x2, gate_m, norm_ffn_g[1], shift_f, scale_f,
                                  router_w[1], router_b[1], seq, False)
    out = _moe(x3, h3, logits3, gate_f, w_gu, b_gu, w_down, b_down, final_norm_g, seq, 1, True)
    return out.reshape(n_batch, seq, d)
```

```python
import functools
import math

import numpy as np
import jax
import jax.numpy as jnp
from jax import lax
from jax.experimental import pallas as pl
from jax.experimental.pallas import tpu as pltpu

F32 = jnp.float32
BF16 = jnp.bfloat16

D_MODEL = 1024
HEAD_DIM = 64
N_HEADS = 16
DIL_PAIRS = ((128, 1), (512, 4), (2048, 16))
DIL_BLOCK = 128
NUM_BUCKETS = 32
MAX_DISTANCE = 2048
N_EXPERTS = 32
TOP_K = 4
SWIGLU_LIMIT = 7.0
SWIGLU_ALPHA = 1.702
RMS_EPS = 1e-6
NEG = -1e30
LOG2E = math.log2(math.e)

ROW_TILE = 512
MOE_TILE = 256
FOX_TILE = 512
AUG = 128
FOX_SUM_ROWS = 8
VMEM_LIMIT = 56 * 1024 * 1024


def _params(sem, vmem=VMEM_LIMIT):
    return pltpu.CompilerParams(dimension_semantics=sem, vmem_limit_bytes=vmem)


def _dot(a, b):
    return jnp.dot(a, b, preferred_element_type=F32)


def _dot_nt(a, b):
    return lax.dot_general(a, b, (((1,), (1,)), ((), ())), preferred_element_type=F32)


def _dot_tn(a, b):
    return lax.dot_general(a, b, (((0,), (0,)), ((), ())), preferred_element_type=F32)


def _norm_mod(x, g, shift, scale):
    ms = jnp.mean(x * x, axis=-1, keepdims=True)
    y = x * lax.rsqrt(ms + RMS_EPS) * g
    if scale is not None:
        y = y * (1.0 + scale) + shift
    return y


def _ada_kernel(c_ref, w_ref, b_ref, o_ref):
    c = c_ref[...]
    act = c * (1.0 / (1.0 + jnp.exp(-c)))
    o_ref[0] = jnp.dot(act, w_ref[0], preferred_element_type=F32,
                       precision=lax.Precision.HIGHEST) + b_ref[0]


def _ada(c, ada_w, ada_b):
    depth, d, n = ada_w.shape
    bsz = c.shape[0]
    tn = 1536
    return pl.pallas_call(
        _ada_kernel,
        out_shape=jax.ShapeDtypeStruct((depth, bsz, n), F32),
        grid=(depth, n // tn),
        in_specs=[
            pl.BlockSpec((bsz, d), lambda l, j: (0, 0)),
            pl.BlockSpec((1, d, tn), lambda l, j: (l, 0, j)),
            pl.BlockSpec((1, 1, tn), lambda l, j: (l, 0, j)),
        ],
        out_specs=pl.BlockSpec((1, bsz, tn), lambda l, j: (l, 0, j)),
        compiler_params=_params(("parallel", "parallel")),
        name="ada",
    )(c, ada_w, ada_b.reshape(depth, 1, n))


def _perm_tile(dilation):
    return max(ROW_TILE, DIL_BLOCK * dilation)


def _proj_kernel(x_ref, g_ref, sh_ref, sc_ref, w_ref, o_ref, h_scr, *xs_scr, dilation):
    @pl.when(pl.program_id(1) == 0)
    def _():
        if dilation == 1:
            h_scr[...] = _norm_mod(x_ref[...], g_ref[...], sh_ref[0], sc_ref[0]).astype(BF16)
        else:
            (xs,) = xs_scr
            n_lane = xs.shape[0]
            for c in range(n_lane):
                xs[c] = x_ref[:, c * 128:(c + 1) * 128]
            chunk = x_ref.shape[0] // dilation
            for r in range(dilation):
                xr = jnp.concatenate([xs[c, pl.ds(r, chunk, stride=dilation), :] for c in range(n_lane)], axis=1)
                h_scr[r * chunk:(r + 1) * chunk, :] = _norm_mod(xr, g_ref[...], sh_ref[0], sc_ref[0]).astype(BF16)

    o_ref[...] = _dot(h_scr[...], w_ref[...]).astype(o_ref.dtype)


def _proj(x, g, shift, scale, w, seq, dilation, group):
    t, d = x.shape
    n = w.shape[1]
    tm = _perm_tile(dilation)
    tn = d
    tpb = seq // tm
    return pl.pallas_call(
        functools.partial(_proj_kernel, dilation=dilation),
        out_shape=jax.ShapeDtypeStruct((t, n), BF16),
        grid=(t // tm, n // tn),
        in_specs=[
            pl.BlockSpec((tm, d), lambda i, j: (i, 0)),
            pl.BlockSpec((1, d), lambda i, j: (0, 0)),
            pl.BlockSpec((1, 1, d), lambda i, j: (i // tpb, 0, 0)),
            pl.BlockSpec((1, 1, d), lambda i, j: (i // tpb, 0, 0)),
            pl.BlockSpec((d, tn), lambda i, j: (0, j)),
        ],
        out_specs=pl.BlockSpec((tm, tn), lambda i, j: (i, j)),
        scratch_shapes=[pltpu.VMEM((tm, d), BF16)] + (
            [pltpu.VMEM((d // 128, tm, 128), F32)] if dilation > 1 else []),
        compiler_params=_params(("parallel", "arbitrary")),
        name=f"proj{group}",
    )(x, g.reshape(1, d), shift, scale, w)


def _split3(f):
    def top(v):
        return lax.bitcast_convert_type(lax.bitcast_convert_type(v, jnp.uint32) & jnp.uint32(0xFFFF0000), F32)
    hi = top(f)
    r1 = f - hi
    mid = top(r1)
    return hi, mid, r1 - mid


def _forget_pieces(fc):
    hi, mid, lo = _split3(fc)
    lane = lax.broadcasted_iota(jnp.int32, fc.shape, 1)
    x = jnp.where(lane < N_HEADS, hi, jnp.where(lane < 2 * N_HEADS, mid, jnp.where(lane < 3 * N_HEADS, lo, 0.0)))
    return x.astype(BF16)


def _aug_tables(key_side):
    place = np.zeros((128, N_HEADS * AUG), np.float32)
    const = np.zeros((1, N_HEADS * AUG), np.float32)
    for h in range(N_HEADS):
        base = h * AUG + HEAD_DIM
        for piece in range(3):
            if key_side:
                place[piece * N_HEADS + h, base + piece] = 1.0
                const[0, base + 3 + piece] = 1.0
            else:
                place[piece * N_HEADS + h, base + 3 + piece] = 1.0
                const[0, base + piece] = -1.0
    return jnp.asarray(place, BF16), jnp.asarray(const, F32)


def _aug_weight(w):
    d = w.shape[0]
    w = jnp.pad(w.reshape(d, N_HEADS, HEAD_DIM), ((0, 0), (0, 0), (0, AUG - HEAD_DIM)))
    return w.reshape(d, N_HEADS * AUG).astype(BF16)


def _kvf_kernel(x_ref, g_ref, wk_ref, wv_ref, wf_ref, bf_ref, tri_ref, pl_ref, cr_ref,
                k_ref, v_ref, f_ref, carry, *, tpb):
    h = _norm_mod(x_ref[...], g_ref[...], None, None).astype(BF16)
    v_ref[...] = _dot(h, wv_ref[...]).astype(v_ref.dtype)
    z = _dot(h, wf_ref[...]) + bf_ref[...]
    lf = jnp.minimum(z, 0.0) - jnp.log(1.0 + jnp.exp(-jnp.abs(z)))
    hi, mid, lo = _split3(lf)
    tri = tri_ref[...]
    cs = _dot(tri, hi.astype(BF16)) + _dot(tri, mid.astype(BF16)) + _dot(tri, lo.astype(BF16))

    @pl.when(pl.program_id(0) % tpb == 0)
    def _():
        carry[...] = jnp.zeros_like(carry)

    cs = cs + carry[...]
    carry[...] = cs[ROW_TILE - 1:ROW_TILE, :]
    f2 = cs * LOG2E
    f_ref[...] = f2
    k_ref[...] = (_dot(h, wk_ref[...]) + _dot(_forget_pieces(f2), pl_ref[...]) + cr_ref[...]).astype(k_ref.dtype)


def _kvf(x, g, w_k, w_v, w_f, b_f, n_batch):
    t, d = x.shape
    n_aug = N_HEADS * AUG
    tri = jnp.asarray(np.tril(np.ones((ROW_TILE, ROW_TILE), np.float32)), BF16)
    place, const = _aug_tables(True)
    assert (t // ROW_TILE) % n_batch == 0
    full = lambda i: (0, 0)
    row = lambda i: (i, 0)
    return pl.pallas_call(
        functools.partial(_kvf_kernel, tpb=t // ROW_TILE // n_batch),
        out_shape=(jax.ShapeDtypeStruct((t, n_aug), BF16), jax.ShapeDtypeStruct((t, d), BF16),
                   jax.ShapeDtypeStruct((t, 128), F32)),
        grid=(t // ROW_TILE,),
        in_specs=[
            pl.BlockSpec((ROW_TILE, d), row),
            pl.BlockSpec((1, d), full),
            pl.BlockSpec((d, n_aug), full),
            pl.BlockSpec((d, d), full),
            pl.BlockSpec((d, 128), full),
            pl.BlockSpec((1, 128), full),
            pl.BlockSpec((ROW_TILE, ROW_TILE), full),
            pl.BlockSpec((128, n_aug), full),
            pl.BlockSpec((1, n_aug), full),
        ],
        out_specs=(pl.BlockSpec((ROW_TILE, n_aug), row), pl.BlockSpec((ROW_TILE, d), row),
                   pl.BlockSpec((ROW_TILE, 128), row)),
        scratch_shapes=[pltpu.VMEM((1, 128), F32)],
        compiler_params=_params(("arbitrary",)),
        name="kvf",
    )(x, g.reshape(1, d), w_k, w_v, w_f, b_f, tri, place, const)


def _qaug_kernel(x_ref, g_ref, sh_ref, sc_ref, w_ref, f_ref, pl_ref, cr_ref, o_ref):
    h = _norm_mod(x_ref[...], g_ref[...], sh_ref[0], sc_ref[0]).astype(BF16)
    o_ref[...] = (_dot(h, w_ref[...]) + _dot(_forget_pieces(f_ref[...]), pl_ref[...])
                  + cr_ref[...]).astype(o_ref.dtype)


def _qaug(x, g, shift, scale, w, fcum, seq):
    t, d = x.shape
    n_aug = N_HEADS * AUG
    tpb = seq // ROW_TILE
    place, const = _aug_tables(False)
    full = lambda i: (0, 0)
    row = lambda i: (i, 0)
    per_b = lambda i: (i // tpb, 0, 0)
    return pl.pallas_call(
        _qaug_kernel,
        out_shape=jax.ShapeDtypeStruct((t, n_aug), BF16),
        grid=(t // ROW_TILE,),
        in_specs=[
            pl.BlockSpec((ROW_TILE, d), row),
            pl.BlockSpec((1, d), full),
            pl.BlockSpec((1, 1, d), per_b),
            pl.BlockSpec((1, 1, d), per_b),
            pl.BlockSpec((d, n_aug), full),
            pl.BlockSpec((ROW_TILE, 128), row),
            pl.BlockSpec((128, n_aug), full),
            pl.BlockSpec((1, n_aug), full),
        ],
        out_specs=pl.BlockSpec((ROW_TILE, n_aug), row),
        compiler_params=_params(("parallel",)),
        name="qaug",
    )(x, g.reshape(1, d), shift, scale, w, fcum, place, const)


def _dil_kernel(q_ref, kp_ref, kc_ref, vp_ref, vc_ref, bias_ref, o_ref, lse_ref):
    table = jnp.minimum(pl.program_id(2), 1)
    pair_w = 2 * HEAD_DIM
    lane = lax.broadcasted_iota(jnp.int32, (DIL_BLOCK, pair_w), 1)
    first = lane < HEAD_DIM
    first2 = lax.broadcasted_iota(jnp.int32, (2 * DIL_BLOCK, pair_w), 1) < HEAD_DIM
    lse_tile = jnp.zeros((DIL_BLOCK, pair_w), F32)
    zero = jnp.zeros((), BF16)
    for pair in range(N_HEADS // 2):
        sl = slice(pair * pair_w, (pair + 1) * pair_w)
        q = q_ref[:, sl]
        k2 = jnp.concatenate([kp_ref[:, sl], kc_ref[:, sl]], axis=0)
        v2 = jnp.concatenate([vp_ref[:, sl], vc_ref[:, sl]], axis=0)
        probs, invs = [], []
        for which in range(2):
            h = 2 * pair + which
            qh = jnp.where(first, q, zero) if which == 0 else jnp.where(first, zero, q)
            s = _dot_nt(qh, k2) + bias_ref[table, h]
            m = jnp.max(s, axis=-1, keepdims=True)
            p = jnp.exp(s - m)
            den = jnp.sum(p, axis=-1, keepdims=True)
            probs.append(p.astype(BF16))
            invs.append(1.0 / den)
            lse_tile = jnp.where(lane == h, m + jnp.log(den), lse_tile)
        vcat = jnp.concatenate([jnp.where(first2, v2, zero), jnp.where(first2, zero, v2)], axis=0)
        o = _dot(jnp.concatenate(probs, axis=1), vcat)
        o_ref[:, sl] = (o * jnp.where(first, invs[0], invs[1])).astype(o_ref.dtype)
    lse_ref[...] = lse_tile


def _dilated_group(qkv, bias, group, dilation, n_batch, seq):
    t = qkv.shape[0]
    width = N_HEADS * HEAD_DIM
    nb = seq // dilation // DIL_BLOCK
    bpb = seq // DIL_BLOCK

    def blk(b, r, n):
        return b * bpb + n * dilation + r

    def spec(part, prev):
        if prev:
            return pl.BlockSpec((DIL_BLOCK, width), lambda b, r, n: (blk(b, r, jnp.maximum(n - 1, 0)), part))
        return pl.BlockSpec((DIL_BLOCK, width), lambda b, r, n: (blk(b, r, n), part))

    return pl.pallas_call(
        _dil_kernel,
        out_shape=(jax.ShapeDtypeStruct((t, width), BF16), jax.ShapeDtypeStruct((t, 2 * HEAD_DIM), F32)),
        grid=(n_batch, dilation, nb),
        in_specs=[spec(0, False), spec(1, True), spec(1, False), spec(2, True), spec(2, False),
                  pl.BlockSpec((2, N_HEADS, DIL_BLOCK, 2 * DIL_BLOCK), lambda b, r, n: (0, 0, 0, 0))],
        out_specs=(pl.BlockSpec((DIL_BLOCK, width), lambda b, r, n: (blk(b, r, n), 0)),
                   pl.BlockSpec((DIL_BLOCK, 2 * HEAD_DIM), lambda b, r, n: (blk(b, r, n), 0))),
        compiler_params=_params(("parallel", "parallel", "arbitrary")),
        name=f"dilated{group}",
    )(qkv, qkv, qkv, qkv, qkv, bias)


def _unpermute_rows(a, dilation):
    if dilation == 1:
        return a
    t, c = a.shape
    tm = _perm_tile(dilation)
    return a.reshape(t // tm, dilation, tm // dilation, c).transpose(0, 2, 1, 3).reshape(t, c)


def _perm_matrix(rows, dilation):
    per = rows // dilation
    p = np.zeros((rows, rows), np.float32)
    for r in range(dilation):
        for n in range(per):
            p[n * dilation + r, r * per + n] = 1.0
    return jnp.asarray(p, BF16)


def _t5_bucket_np(n):
    max_exact = NUM_BUCKETS // 2
    nf = np.maximum(n, 1).astype(np.float32)
    large = max_exact + (np.log(nf / np.float32(max_exact)) / np.float32(math.log(MAX_DISTANCE / max_exact))
                         * np.float32(NUM_BUCKETS - max_exact)).astype(np.int32)
    large = np.minimum(large, NUM_BUCKETS - 1)
    return np.where(n < max_exact, n, large)


def _mix_tail(mix_in, wo_ref, x_ref, gm_ref, g_ref, sh_ref, sc_ref, wr_ref, br_ref,
              x_out, h_out, lg_out):
    mix = _dot(mix_in, wo_ref[...])
    x_new = x_ref[...] + gm_ref[0] * mix
    x_out[...] = x_new
    h = _norm_mod(x_new, g_ref[...], sh_ref[0], sc_ref[0])
    h_out[...] = h.astype(BF16)
    lg_out[...] = jnp.dot(h, wr_ref[...], preferred_element_type=F32,
                          precision=lax.Precision.HIGHEST) + br_ref[...]


def _merge_oproj_kernel(o0_ref, o1_ref, o2_ref, p1_ref, p2_ref, lse_ref, ex_ref, wo_ref, x_ref, gm_ref, g_ref,
                        sh_ref, sc_ref, wr_ref, br_ref, x_out, h_out, lg_out):
    lse = lse_ref[...]
    l0, l1, l2 = lse[:, 0:16], lse[:, 16:32], lse[:, 32:48]
    m = jnp.maximum(jnp.maximum(l0, l1), l2)
    e0, e1, e2 = jnp.exp(l0 - m), jnp.exp(l1 - m), jnp.exp(l2 - m)
    inv = 1.0 / (e0 + e1 + e2)
    ex = ex_ref[...]
    o1 = _dot(p1_ref[...], o1_ref[...])
    o2 = _dot(p2_ref[...], o2_ref[0].reshape(ROW_TILE, o2_ref.shape[-1]))
    merged = (_dot((e0 * inv).astype(BF16), ex) * o0_ref[...].astype(F32)
              + _dot((e1 * inv).astype(BF16), ex) * o1
              + _dot((e2 * inv).astype(BF16), ex) * o2)
    _mix_tail(merged.astype(BF16), wo_ref, x_ref, gm_ref, g_ref, sh_ref, sc_ref, wr_ref, br_ref,
              x_out, h_out, lg_out)


def _oproj_kernel(o_ref, wo_ref, x_ref, gm_ref, g_ref, sh_ref, sc_ref, wr_ref, br_ref,
                  x_out, h_out, lg_out):
    _mix_tail(o_ref[...], wo_ref, x_ref, gm_ref, g_ref, sh_ref, sc_ref, wr_ref, br_ref,
              x_out, h_out, lg_out)


def _mixer_tail(mix_inputs, w_o, x, gate_m, g_ffn, shift_f, scale_f, w_r, b_r, seq, merged):
    t, d = x.shape
    tpb = seq // ROW_TILE
    row = lambda i: (i, 0)
    full = lambda i: (0, 0)
    per_b = lambda i: (i // tpb, 0, 0)
    tail_specs = [
        pl.BlockSpec((d, d), full),
        pl.BlockSpec((ROW_TILE, d), row),
        pl.BlockSpec((1, 1, d), per_b),
        pl.BlockSpec((1, d), full),
        pl.BlockSpec((1, 1, d), per_b),
        pl.BlockSpec((1, 1, d), per_b),
        pl.BlockSpec((d, N_EXPERTS), full),
        pl.BlockSpec((1, N_EXPERTS), full),
    ]
    tail_args = (w_o, x, gate_m, g_ffn.reshape(1, d), shift_f, scale_f, w_r, b_r.reshape(1, N_EXPERTS))
    if merged:
        o0, o1, o2, lse = mix_inputs
        expand = jnp.asarray(np.kron(np.eye(N_HEADS, dtype=np.float32),
                                     np.ones((1, HEAD_DIM), np.float32)), BF16)
        d1, d2 = DIL_PAIRS[1][1], DIL_PAIRS[2][1]
        assert _perm_tile(d1) == ROW_TILE
        t2 = _perm_tile(d2)
        sub2 = ROW_TILE // d2
        o2v = o2.reshape(t // t2, d2, t2 // d2, d)
        per2 = t2 // ROW_TILE
        kern = _merge_oproj_kernel
        specs = [pl.BlockSpec((ROW_TILE, d), row), pl.BlockSpec((ROW_TILE, d), row),
                 pl.BlockSpec((1, d2, sub2, d), lambda i: (i // per2, 0, i % per2, 0)),
                 pl.BlockSpec((ROW_TILE, ROW_TILE), full), pl.BlockSpec((ROW_TILE, ROW_TILE), full),
                 pl.BlockSpec((ROW_TILE, 3 * N_HEADS), row), pl.BlockSpec((N_HEADS, d), full)]
        args = (o0, o1, o2v, _perm_matrix(ROW_TILE, d1), _perm_matrix(ROW_TILE, d2), lse, expand)
    else:
        kern = _oproj_kernel
        specs = [pl.BlockSpec((ROW_TILE, d), row)]
        args = mix_inputs
    return pl.pallas_call(
        kern,
        out_shape=(jax.ShapeDtypeStruct((t, d), F32), jax.ShapeDtypeStruct((t, d), BF16),
                   jax.ShapeDtypeStruct((t, N_EXPERTS), F32)),
        grid=(t // ROW_TILE,),
        in_specs=specs + tail_specs,
        out_specs=(pl.BlockSpec((ROW_TILE, d), row), pl.BlockSpec((ROW_TILE, d), row),
                   pl.BlockSpec((ROW_TILE, N_EXPERTS), row)),
        compiler_params=_params(("parallel",)),
        name="mixer_tail_merge" if merged else "mixer_tail",
    )(*args, *tail_args)


def _fox_kernel(q_ref, k_ref, v_ref, o_ref, vt_scr):
    qi = pl.program_id(2)
    seq = k_ref.shape[0]
    nk = seq // FOX_TILE
    half = HEAD_DIM

    ext = FOX_SUM_ROWS
    @pl.when(qi == 0)
    def _():
        lane = lax.broadcasted_iota(jnp.int32, (FOX_TILE, 2 * half), 1)
        erow = lax.broadcasted_iota(jnp.int32, (2 * ext, 2 * FOX_TILE), 0)
        ecol = lax.broadcasted_iota(jnp.int32, (2 * ext, 2 * FOX_TILE), 1)
        sum_rows = jnp.where((erow < ext) == (ecol < FOX_TILE), 1.0, 0.0).astype(BF16)
        for j in range(nk):
            vj = v_ref[j * FOX_TILE:(j + 1) * FOX_TILE, :].astype(F32)
            vt_scr[j, :2 * half, :FOX_TILE] = jnp.where(lane < half, vj, 0.0).T.astype(BF16)
            vt_scr[j, :2 * half, FOX_TILE:] = jnp.where(lane >= half, vj, 0.0).T.astype(BF16)
            vt_scr[j, 2 * half:, :] = sum_rows

    q0 = q_ref[:, :AUG]
    q1 = q_ref[:, AUG:]

    def softmax_step(s, m):
        m_new = jnp.maximum(m, jnp.max(s, axis=0, keepdims=True))
        return m_new, jnp.exp2(m - m_new), jnp.exp2((s - m_new).astype(BF16))

    def step(kj, carry, masked):
        m0, m1, acc = carry
        rows = pl.ds(pl.multiple_of(kj * FOX_TILE, FOX_TILE), FOX_TILE)
        s0 = _dot_nt(k_ref[rows, :AUG], q0)
        s1 = _dot_nt(k_ref[rows, AUG:], q1)
        if masked:
            kpos = lax.broadcasted_iota(jnp.int32, (FOX_TILE, FOX_TILE), 0)
            qpos = lax.broadcasted_iota(jnp.int32, (FOX_TILE, FOX_TILE), 1)
            keep = kpos <= qpos
            s0 = jnp.where(keep, s0, NEG)
            s1 = jnp.where(keep, s1, NEG)
        m0, a0, p0 = softmax_step(s0, m0)
        m1, a1, p1 = softmax_step(s1, m1)
        pv = _dot(vt_scr[kj], jnp.concatenate([p0, p1], axis=0))
        alpha = jnp.concatenate([jnp.broadcast_to(a0, (half, FOX_TILE)), jnp.broadcast_to(a1, (half, FOX_TILE)),
                                 jnp.broadcast_to(a0, (ext, FOX_TILE)), jnp.broadcast_to(a1, (ext, FOX_TILE))],
                                axis=0)
        return m0, m1, alpha * acc + pv

    neg = jnp.full((1, FOX_TILE), NEG, F32)
    init = (neg, neg, jnp.zeros((2 * half + 2 * ext, FOX_TILE), F32))
    carry = lax.fori_loop(0, qi, lambda kj, c: step(kj, c, False), init)
    _, _, acc = step(qi, carry, True)
    inv0 = 1.0 / acc[2 * half:2 * half + 1, :]
    inv1 = 1.0 / acc[2 * half + ext:2 * half + ext + 1, :]
    inv = jnp.concatenate([jnp.broadcast_to(inv0, (half, FOX_TILE)),
                           jnp.broadcast_to(inv1, (half, FOX_TILE))], axis=0)
    o_ref[...] = (acc[:2 * half, :] * inv).T.astype(o_ref.dtype)


def _fox(q_aug, k_aug, v, n_batch, seq):
    t = q_aug.shape[0]
    nq = seq // FOX_TILE
    pairs = N_HEADS // 2
    return pl.pallas_call(
        _fox_kernel,
        out_shape=jax.ShapeDtypeStruct((t, N_HEADS * HEAD_DIM), BF16),
        grid=(n_batch, pairs, nq),
        in_specs=[
            pl.BlockSpec((FOX_TILE, 2 * AUG), lambda b, p, qi: (b * nq + qi, p)),
            pl.BlockSpec((seq, 2 * AUG), lambda b, p, qi: (b, p)),
            pl.BlockSpec((seq, 2 * HEAD_DIM), lambda b, p, qi: (b, p)),
        ],
        out_specs=pl.BlockSpec((FOX_TILE, 2 * HEAD_DIM), lambda b, p, qi: (b * nq + qi, p)),
        scratch_shapes=[pltpu.VMEM((nq, 2 * HEAD_DIM + 2 * FOX_SUM_ROWS, 2 * FOX_TILE), BF16)],
        compiler_params=_params(("parallel", "parallel", "arbitrary")),
        name="fox",
    )(q_aug, k_aug, v)


def _expert_kernel(be_ref, nu_ref, x_ref, wgu_ref, bgu_ref, wd_ref, bd_ref, gate_ref, o_ref, wgu_bf, wd_bf):
    i = pl.program_id(0)
    d_ff = wd_ref.shape[1]

    @pl.when(jnp.logical_or(i == 0, be_ref[i] != be_ref[jnp.maximum(i - 1, 0)]))
    def _():
        wgu_bf[...] = wgu_ref[0].astype(BF16)
        wd_bf[...] = wd_ref[0].astype(BF16)

    @pl.when(i < nu_ref[0])
    def _():
        gu = _dot(x_ref[...], wgu_bf[...]) + bgu_ref[0]
        g = jnp.minimum(gu[:, :d_ff], SWIGLU_LIMIT)
        u = jnp.clip(gu[:, d_ff:], -SWIGLU_LIMIT, SWIGLU_LIMIT)
        act = (u + 1.0) * g * (1.0 / (1.0 + jnp.exp(-SWIGLU_ALPHA * g)))
        y = _dot(act.astype(BF16), wd_bf[...]) + bd_ref[0]
        o_ref[...] = (y * gate_ref[...]).astype(o_ref.dtype)

    @pl.when(i >= nu_ref[0])
    def _():
        o_ref[...] = jnp.zeros_like(o_ref)


def _experts(xbuf, w_gu, b_gu, w_down, b_down, gate_rows, block_expert, n_used, layer):
    n_rows, d = xbuf.shape
    depth, e, _, n_gu = w_gu.shape
    d_ff = w_down.shape[2]
    n_blocks = n_rows // MOE_TILE
    grid_spec = pltpu.PrefetchScalarGridSpec(
        num_scalar_prefetch=2,
        grid=(n_blocks,),
        in_specs=[
            pl.BlockSpec((MOE_TILE, d), lambda i, be, nu: (i, 0)),
            pl.BlockSpec((None, 1, d, n_gu), lambda i, be, nu: (layer, be[i], 0, 0)),
            pl.BlockSpec((None, 1, 1, n_gu), lambda i, be, nu: (layer, be[i], 0, 0)),
            pl.BlockSpec((None, 1, d_ff, d), lambda i, be, nu: (layer, be[i], 0, 0)),
            pl.BlockSpec((None, 1, 1, d), lambda i, be, nu: (layer, be[i], 0, 0)),
            pl.BlockSpec((MOE_TILE, 1), lambda i, be, nu: (i, 0)),
        ],
        out_specs=pl.BlockSpec((MOE_TILE, d), lambda i, be, nu: (i, 0)),
        scratch_shapes=[pltpu.VMEM((d, n_gu), BF16), pltpu.VMEM((d_ff, d), BF16)],
    )
    return pl.pallas_call(
        _expert_kernel,
        out_shape=jax.ShapeDtypeStruct((n_rows, d), BF16),
        grid_spec=grid_spec,
        compiler_params=_params(("arbitrary",)),
        name="experts",
    )(block_expert, n_used, xbuf, w_gu, b_gu.reshape(depth, e, 1, n_gu), w_down, b_down.reshape(depth, e, 1, d),
      gate_rows)


def _route(logits):
    t = logits.shape[0]
    n_slots = t * TOP_K
    n_rows = n_slots + N_EXPERTS * MOE_TILE
    n_blocks = n_rows // MOE_TILE
    top_val, top_idx = lax.top_k(logits, TOP_K)
    gates_flat = jax.nn.softmax(top_val, axis=-1).T.reshape(-1)
    e_flat = top_idx.T.reshape(-1).astype(jnp.int32)
    skey = jnp.sort(e_flat * n_slots + jnp.arange(n_slots, dtype=jnp.int32))
    slot_sorted = skey - (skey // n_slots) * n_slots
    bounds = jnp.searchsorted(skey, jnp.arange(N_EXPERTS + 1, dtype=jnp.int32) * n_slots, side='left')
    starts = bounds[:-1].astype(jnp.int32)
    counts = (bounds[1:] - bounds[:-1]).astype(jnp.int32)
    padded = (counts + MOE_TILE - 1) // MOE_TILE * MOE_TILE
    pad_ends = jnp.cumsum(padded)
    pad_starts = pad_ends - padded
    block_start = jnp.arange(n_blocks, dtype=jnp.int32) * MOE_TILE
    block_expert = jnp.minimum(jnp.searchsorted(pad_ends, block_start, side='right'),
                               N_EXPERTS - 1).astype(jnp.int32)
    n_used = (pad_ends[-1] // MOE_TILE).astype(jnp.int32).reshape(1)
    rows = jnp.arange(n_rows, dtype=jnp.int32)
    e_row = jnp.repeat(block_expert, MOE_TILE)
    off = rows - pad_starts[e_row]
    valid = off < counts[e_row]
    slot_rows = slot_sorted[jnp.clip(starts[e_row] + off, 0, n_slots - 1)]
    tok_rows = jnp.where(valid, slot_rows % t, 0)
    gate_rows = jnp.where(valid, gates_flat[slot_rows], 0.0)
    _, by_slot = lax.sort((jnp.where(valid, slot_rows, n_slots + rows), rows), num_keys=1)
    return by_slot[:n_slots], tok_rows, gate_rows.reshape(n_rows, 1), block_expert, n_used


def _combine_kernel(x_ref, y0_ref, y1_ref, y2_ref, y3_ref, gf_ref, g_ref, o_ref, *, final):
    moe = (y0_ref[...].astype(F32) + y1_ref[...].astype(F32)
           + y2_ref[...].astype(F32) + y3_ref[...].astype(F32))
    x_new = x_ref[...] + gf_ref[0] * moe
    if final:
        x_new = _norm_mod(x_new, g_ref[...], None, None)
    o_ref[...] = x_new


def _combine(x, y_slots, gate_f, g_final, seq, final):
    t, d = x.shape
    tpb = seq // ROW_TILE
    nt = t // ROW_TILE
    assert TOP_K == 4
    return pl.pallas_call(
        functools.partial(_combine_kernel, final=final),
        out_shape=jax.ShapeDtypeStruct((t, d), F32),
        grid=(nt,),
        in_specs=[pl.BlockSpec((ROW_TILE, d), lambda i: (i, 0))] + [
            pl.BlockSpec((ROW_TILE, d), functools.partial(lambda i, k: (k * nt + i, 0), k=k))
            for k in range(TOP_K)] + [
            pl.BlockSpec((1, 1, d), lambda i: (i // tpb, 0, 0)),
            pl.BlockSpec((1, d), lambda i: (0, 0)),
        ],
        out_specs=pl.BlockSpec((ROW_TILE, d), lambda i: (i, 0)),
        compiler_params=_params(("parallel",)),
        name="combine_final" if final else "combine",
    )(x, y_slots, y_slots, y_slots, y_slots, gate_f, g_final.reshape(1, d))


def _moe(x, h, logits, gate_f, w_gu, b_gu, w_down, b_down, g_final, seq, layer, final):
    t, d = x.shape
    dest, tok_rows, gate_rows, block_expert, n_used = _route(logits)
    xbuf = jnp.take(h, tok_rows, axis=0)
    ybuf = _experts(xbuf, w_gu, b_gu, w_down, b_down, gate_rows, block_expert, n_used, layer)
    y_slots = jnp.take(ybuf, dest, axis=0)
    return _combine(x, y_slots, gate_f, g_final, seq, final)


def kernel(x, c, ada_w, ada_b, norm_mix_g, norm_ffn_g, a_w_qkv, a_w_o, rel_bias, kv_norm_g, w_kvf, b_f,
           b_w_q, b_w_o, router_w, router_b, w_gu, b_gu, w_down, b_down, final_norm_g):
    n_batch, seq, d = x.shape
    t = n_batch * seq
    width = N_HEADS * HEAD_DIM
    xf = x.reshape(t, d)

    ada = _ada(c, ada_w, ada_b)
    mods = [[ada[l, :, i * d:(i + 1) * d].reshape(n_batch, 1, d) for i in range(6)] for l in range(2)]

    shift_m, scale_m, gate_m, shift_f, scale_f, gate_f = mods[0]
    qscale = np.ones((3, 3, 1), np.float32)
    qscale[:, 0] = HEAD_DIM ** -0.5
    w_qkv = (a_w_qkv[0].reshape(d, 3, 3, width) * qscale).astype(BF16)

    qi = np.arange(DIL_BLOCK, dtype=np.int32)[:, None]
    kj = np.arange(2 * DIL_BLOCK, dtype=np.int32)[None, :]
    delta = qi + DIL_BLOCK - kj
    in_band = (delta >= 0) & (delta <= DIL_BLOCK)
    outs, lses = [], []
    for g, (window, dilation) in enumerate(DIL_PAIRS):
        assert window // dilation == DIL_BLOCK
        qkv = _proj(xf, norm_mix_g[0], shift_m, scale_m, w_qkv[:, g].reshape(d, 3 * width), seq, dilation, g)
        bucket = _t5_bucket_np(np.clip(delta, 0, None) * dilation)
        tab = rel_bias[:, g * N_HEADS:(g + 1) * N_HEADS].astype(F32)
        bias = jnp.where(in_band, jnp.transpose(tab[bucket], (2, 0, 1)), NEG)
        bias_first = jnp.where(kj >= DIL_BLOCK, bias, NEG)
        o, lse = _dilated_group(qkv, jnp.stack([bias_first, bias]), g, dilation, n_batch, seq)
        outs.append(o)
        lses.append(_unpermute_rows(lse, dilation)[:, :N_HEADS])
    lse_all = jnp.concatenate(lses, axis=1)
    x1, h1, logits1 = _mixer_tail((outs[0], outs[1], outs[2], lse_all), a_w_o[0].astype(BF16), xf, gate_m,
                                  norm_ffn_g[0], shift_f, scale_f, router_w[0], router_b[0], seq, True)
    x2 = _moe(x1, h1, logits1, gate_f, w_gu, b_gu, w_down, b_down, final_norm_g, seq, 0, False)

    w_f = jnp.pad(jnp.tile(w_kvf[:, 2 * width:], (1, 3)), ((0, 0), (0, 128 - 3 * N_HEADS))).astype(BF16)
    b_fp = jnp.pad(jnp.tile(b_f, 3), (0, 128 - 3 * N_HEADS)).reshape(1, 128)
    k_aug, v_sh, fcum = _kvf(x2, kv_norm_g, _aug_weight(w_kvf[:, :width]), w_kvf[:, width:2 * width].astype(BF16),
                             w_f, b_fp, n_batch)

    shift_m, scale_m, gate_m, shift_f, scale_f, gate_f = mods[1]
    q_aug = _qaug(x2, norm_mix_g[1], shift_m, scale_m, _aug_weight(b_w_q[0] * (HEAD_DIM ** -0.5 * LOG2E)), fcum, seq)
    o1 = _fox(q_aug, k_aug, v_sh, n_batch, seq)

    x3, h3, logits3 = _mixer_tail((o1,), b_w_o[0].astype(BF16), x2, gate_m, norm_ffn_g[1], shift_f, scale_f,
                                  router_w[1], router_b[1], seq, False)
    out = _moe(x3, h3, logits3, gate_f, w_gu, b_gu, w_down, b_down, final_norm_g, seq, 1, True)
    return out.reshape(n_batch, seq, d)
```

```python
import functools
import math

import numpy as np
import jax
import jax.numpy as jnp
from jax import lax
from jax.experimental import pallas as pl
from jax.experimental.pallas import tpu as pltpu

F32 = jnp.float32
BF16 = jnp.bfloat16

D_MODEL = 1024
HEAD_DIM = 64
N_HEADS = 16
DIL_PAIRS = ((128, 1), (512, 4), (2048, 16))
DIL_BLOCK = 128
NUM_BUCKETS = 32
MAX_DISTANCE = 2048
N_EXPERTS = 32
TOP_K = 4
SWIGLU_LIMIT = 7.0
SWIGLU_ALPHA = 1.702
RMS_EPS = 1e-6
NEG = -1e30
LOG2E = math.log2(math.e)

ROW_TILE = 512
MOE_TILE = 256
FOX_TILE = 512
AUG = 128
FOX_SUM_ROWS = 8
VMEM_LIMIT = 56 * 1024 * 1024


def _params(sem, vmem=VMEM_LIMIT):
    return pltpu.CompilerParams(dimension_semantics=sem, vmem_limit_bytes=vmem)


def _dot(a, b):
    return jnp.dot(a, b, preferred_element_type=F32)


def _dot_nt(a, b):
    return lax.dot_general(a, b, (((1,), (1,)), ((), ())), preferred_element_type=F32)


def _dot_tn(a, b):
    return lax.dot_general(a, b, (((0,), (0,)), ((), ())), preferred_element_type=F32)


def _norm_mod(x, g, shift, scale):
    ms = jnp.mean(x * x, axis=-1, keepdims=True)
    y = x * lax.rsqrt(ms + RMS_EPS) * g
    if scale is not None:
        y = y * (1.0 + scale) + shift
    return y


def _ada_kernel(c_ref, w_ref, b_ref, o_ref):
    c = c_ref[...]
    act = c * (1.0 / (1.0 + jnp.exp(-c)))
    o_ref[0] = jnp.dot(act, w_ref[0], preferred_element_type=F32,
                       precision=lax.Precision.HIGHEST) + b_ref[0]


def _ada(c, ada_w, ada_b):
    depth, d, n = ada_w.shape
    bsz = c.shape[0]
    tn = 1536
    return pl.pallas_call(
        _ada_kernel,
        out_shape=jax.ShapeDtypeStruct((depth, bsz, n), F32),
        grid=(depth, n // tn),
        in_specs=[
            pl.BlockSpec((bsz, d), lambda l, j: (0, 0)),
            pl.BlockSpec((1, d, tn), lambda l, j: (l, 0, j)),
            pl.BlockSpec((1, 1, tn), lambda l, j: (l, 0, j)),
        ],
        out_specs=pl.BlockSpec((1, bsz, tn), lambda l, j: (l, 0, j)),
        compiler_params=_params(("parallel", "parallel")),
        name="ada",
    )(c, ada_w, ada_b.reshape(depth, 1, n))


def _perm_tile(dilation):
    return max(ROW_TILE, DIL_BLOCK * dilation)


def _proj_kernel(x_ref, g_ref, sh_ref, sc_ref, w_ref, o_ref, h_scr, *xs_scr, dilation):
    @pl.when(pl.program_id(1) == 0)
    def _():
        if dilation == 1:
            h_scr[...] = _norm_mod(x_ref[...], g_ref[...], sh_ref[0], sc_ref[0]).astype(BF16)
        else:
            (xs,) = xs_scr
            n_lane = xs.shape[0]
            for c in range(n_lane):
                xs[c] = x_ref[:, c * 128:(c + 1) * 128]
            chunk = x_ref.shape[0] // dilation
            for r in range(dilation):
                xr = jnp.concatenate([xs[c, pl.ds(r, chunk, stride=dilation), :] for c in range(n_lane)], axis=1)
                h_scr[r * chunk:(r + 1) * chunk, :] = _norm_mod(xr, g_ref[...], sh_ref[0], sc_ref[0]).astype(BF16)

    o_ref[...] = _dot(h_scr[...], w_ref[...]).astype(o_ref.dtype)


def _proj(x, g, shift, scale, w, seq, dilation, group):
    t, d = x.shape
    n = w.shape[1]
    tm = _perm_tile(dilation)
    tn = d
    tpb = seq // tm
    return pl.pallas_call(
        functools.partial(_proj_kernel, dilation=dilation),
        out_shape=jax.ShapeDtypeStruct((t, n), BF16),
        grid=(t // tm, n // tn),
        in_specs=[
            pl.BlockSpec((tm, d), lambda i, j: (i, 0)),
            pl.BlockSpec((1, d), lambda i, j: (0, 0)),
            pl.BlockSpec((1, 1, d), lambda i, j: (i // tpb, 0, 0)),
            pl.BlockSpec((1, 1, d), lambda i, j: (i // tpb, 0, 0)),
            pl.BlockSpec((d, tn), lambda i, j: (0, j)),
        ],
        out_specs=pl.BlockSpec((tm, tn), lambda i, j: (i, j)),
        scratch_shapes=[pltpu.VMEM((tm, d), BF16)] + (
            [pltpu.VMEM((d // 128, tm, 128), F32)] if dilation > 1 else []),
        compiler_params=_params(("parallel", "arbitrary")),
        name=f"proj{group}",
    )(x, g.reshape(1, d), shift, scale, w)


def _split3(f):
    def top(v):
        return lax.bitcast_convert_type(lax.bitcast_convert_type(v, jnp.uint32) & jnp.uint32(0xFFFF0000), F32)
    hi = top(f)
    r1 = f - hi
    mid = top(r1)
    return hi, mid, r1 - mid


def _forget_pieces(fc):
    hi, mid, lo = _split3(fc)
    lane = lax.broadcasted_iota(jnp.int32, fc.shape, 1)
    x = jnp.where(lane < N_HEADS, hi, jnp.where(lane < 2 * N_HEADS, mid, jnp.where(lane < 3 * N_HEADS, lo, 0.0)))
    return x.astype(BF16)


def _aug_tables(key_side):
    place = np.zeros((128, N_HEADS * AUG), np.float32)
    const = np.zeros((1, N_HEADS * AUG), np.float32)
    for h in range(N_HEADS):
        base = h * AUG + HEAD_DIM
        for piece in range(3):
            if key_side:
                place[piece * N_HEADS + h, base + piece] = 1.0
                const[0, base + 3 + piece] = 1.0
            else:
                place[piece * N_HEADS + h, base + 3 + piece] = 1.0
                const[0, base + piece] = -1.0
    return jnp.asarray(place, BF16), jnp.asarray(const, F32)


def _aug_weight(w):
    d = w.shape[0]
    w = jnp.pad(w.reshape(d, N_HEADS, HEAD_DIM), ((0, 0), (0, 0), (0, AUG - HEAD_DIM)))
    return w.reshape(d, N_HEADS * AUG).astype(BF16)


def _kvf_kernel(x_ref, g_ref, wk_ref, wv_ref, wf_ref, bf_ref, tri_ref, pl_ref, cr_ref,
                k_ref, v_ref, f_ref, carry, *, tpb):
    h = _norm_mod(x_ref[...], g_ref[...], None, None).astype(BF16)
    v_ref[...] = _dot(h, wv_ref[...]).astype(v_ref.dtype)
    z = _dot(h, wf_ref[...]) + bf_ref[...]
    lf = jnp.minimum(z, 0.0) - jnp.log(1.0 + jnp.exp(-jnp.abs(z)))
    hi, mid, lo = _split3(lf)
    tri = tri_ref[...]
    cs = _dot(tri, hi.astype(BF16)) + _dot(tri, mid.astype(BF16)) + _dot(tri, lo.astype(BF16))

    @pl.when(pl.program_id(0) % tpb == 0)
    def _():
        carry[...] = jnp.zeros_like(carry)

    cs = cs + carry[...]
    carry[...] = cs[ROW_TILE - 1:ROW_TILE, :]
    f2 = cs * LOG2E
    f_ref[...] = f2
    k_ref[...] = (_dot(h, wk_ref[...]) + _dot(_forget_pieces(f2), pl_ref[...]) + cr_ref[...]).astype(k_ref.dtype)


def _kvf(x, g, w_k, w_v, w_f, b_f, n_batch):
    t, d = x.shape
    n_aug = N_HEADS * AUG
    tri = jnp.asarray(np.tril(np.ones((ROW_TILE, ROW_TILE), np.float32)), BF16)
    place, const = _aug_tables(True)
    assert (t // ROW_TILE) % n_batch == 0
    full = lambda i: (0, 0)
    row = lambda i: (i, 0)
    return pl.pallas_call(
        functools.partial(_kvf_kernel, tpb=t // ROW_TILE // n_batch),
        out_shape=(jax.ShapeDtypeStruct((t, n_aug), BF16), jax.ShapeDtypeStruct((t, d), BF16),
                   jax.ShapeDtypeStruct((t, 128), F32)),
        grid=(t // ROW_TILE,),
        in_specs=[
            pl.BlockSpec((ROW_TILE, d), row),
            pl.BlockSpec((1, d), full),
            pl.BlockSpec((d, n_aug), full),
            pl.BlockSpec((d, d), full),
            pl.BlockSpec((d, 128), full),
            pl.BlockSpec((1, 128), full),
            pl.BlockSpec((ROW_TILE, ROW_TILE), full),
            pl.BlockSpec((128, n_aug), full),
            pl.BlockSpec((1, n_aug), full),
        ],
        out_specs=(pl.BlockSpec((ROW_TILE, n_aug), row), pl.BlockSpec((ROW_TILE, d), row),
                   pl.BlockSpec((ROW_TILE, 128), row)),
        scratch_shapes=[pltpu.VMEM((1, 128), F32)],
        compiler_params=_params(("arbitrary",)),
        name="kvf",
    )(x, g.reshape(1, d), w_k, w_v, w_f, b_f, tri, place, const)


def _qaug_kernel(x_ref, g_ref, sh_ref, sc_ref, w_ref, f_ref, pl_ref, cr_ref, o_ref):
    h = _norm_mod(x_ref[...], g_ref[...], sh_ref[0], sc_ref[0]).astype(BF16)
    o_ref[...] = (_dot(h, w_ref[...]) + _dot(_forget_pieces(f_ref[...]), pl_ref[...])
                  + cr_ref[...]).astype(o_ref.dtype)


def _qaug(x, g, shift, scale, w, fcum, seq):
    t, d = x.shape
    n_aug = N_HEADS * AUG
    tpb = seq // ROW_TILE
    place, const = _aug_tables(False)
    full = lambda i: (0, 0)
    row = lambda i: (i, 0)
    per_b = lambda i: (i // tpb, 0, 0)
    return pl.pallas_call(
        _qaug_kernel,
        out_shape=jax.ShapeDtypeStruct((t, n_aug), BF16),
        grid=(t // ROW_TILE,),
        in_specs=[
            pl.BlockSpec((ROW_TILE, d), row),
            pl.BlockSpec((1, d), full),
            pl.BlockSpec((1, 1, d), per_b),
            pl.BlockSpec((1, 1, d), per_b),
            pl.BlockSpec((d, n_aug), full),
            pl.BlockSpec((ROW_TILE, 128), row),
            pl.BlockSpec((128, n_aug), full),
            pl.BlockSpec((1, n_aug), full),
        ],
        out_specs=pl.BlockSpec((ROW_TILE, n_aug), row),
        compiler_params=_params(("parallel",)),
        name="qaug",
    )(x, g.reshape(1, d), shift, scale, w, fcum, place, const)


def _dil_kernel(q_ref, kp_ref, kc_ref, vp_ref, vc_ref, bias_ref, o_ref, lse_ref):
    table = jnp.minimum(pl.program_id(2), 1)
    pair_w = 2 * HEAD_DIM
    lane = lax.broadcasted_iota(jnp.int32, (DIL_BLOCK, pair_w), 1)
    first = lane < HEAD_DIM
    first2 = lax.broadcasted_iota(jnp.int32, (2 * DIL_BLOCK, pair_w), 1) < HEAD_DIM
    lse_tile = jnp.zeros((DIL_BLOCK, pair_w), F32)
    zero = jnp.zeros((), BF16)
    for pair in range(N_HEADS // 2):
        sl = slice(pair * pair_w, (pair + 1) * pair_w)
        q = q_ref[:, sl]
        k2 = jnp.concatenate([kp_ref[:, sl], kc_ref[:, sl]], axis=0)
        v2 = jnp.concatenate([vp_ref[:, sl], vc_ref[:, sl]], axis=0)
        probs, invs = [], []
        for which in range(2):
            h = 2 * pair + which
            qh = jnp.where(first, q, zero) if which == 0 else jnp.where(first, zero, q)
            s = _dot_nt(qh, k2) + bias_ref[table, h]
            m = jnp.max(s, axis=-1, keepdims=True)
            p = jnp.exp(s - m)
            den = jnp.sum(p, axis=-1, keepdims=True)
            probs.append(p.astype(BF16))
            invs.append(1.0 / den)
            lse_tile = jnp.where(lane == h, m + jnp.log(den), lse_tile)
        vcat = jnp.concatenate([jnp.where(first2, v2, zero), jnp.where(first2, zero, v2)], axis=0)
        o = _dot(jnp.concatenate(probs, axis=1), vcat)
        o_ref[:, sl] = (o * jnp.where(first, invs[0], invs[1])).astype(o_ref.dtype)
    lse_ref[...] = lse_tile


def _dilated_group(qkv, bias, group, dilation, n_batch, seq):
    t = qkv.shape[0]
    width = N_HEADS * HEAD_DIM
    nb = seq // dilation // DIL_BLOCK
    bpb = seq // DIL_BLOCK

    def blk(b, r, n):
        return b * bpb + n * dilation + r

    def spec(part, prev):
        if prev:
            return pl.BlockSpec((DIL_BLOCK, width), lambda b, r, n: (blk(b, r, jnp.maximum(n - 1, 0)), part))
        return pl.BlockSpec((DIL_BLOCK, width), lambda b, r, n: (blk(b, r, n), part))

    return pl.pallas_call(
        _dil_kernel,
        out_shape=(jax.ShapeDtypeStruct((t, width), BF16), jax.ShapeDtypeStruct((t, 2 * HEAD_DIM), F32)),
        grid=(n_batch, dilation, nb),
        in_specs=[spec(0, False), spec(1, True), spec(1, False), spec(2, True), spec(2, False),
                  pl.BlockSpec((2, N_HEADS, DIL_BLOCK, 2 * DIL_BLOCK), lambda b, r, n: (0, 0, 0, 0))],
        out_specs=(pl.BlockSpec((DIL_BLOCK, width), lambda b, r, n: (blk(b, r, n), 0)),
                   pl.BlockSpec((DIL_BLOCK, 2 * HEAD_DIM), lambda b, r, n: (blk(b, r, n), 0))),
        compiler_params=_params(("parallel", "parallel", "arbitrary")),
        name=f"dilated{group}",
    )(qkv, qkv, qkv, qkv, qkv, bias)


def _unpermute_rows(a, dilation):
    if dilation == 1:
        return a
    t, c = a.shape
    tm = _perm_tile(dilation)
    return a.reshape(t // tm, dilation, tm // dilation, c).transpose(0, 2, 1, 3).reshape(t, c)


def _perm_matrix(rows, dilation):
    per = rows // dilation
    p = np.zeros((rows, rows), np.float32)
    for r in range(dilation):
        for n in range(per):
            p[n * dilation + r, r * per + n] = 1.0
    return jnp.asarray(p, BF16)


def _t5_bucket_np(n):
    max_exact = NUM_BUCKETS // 2
    nf = np.maximum(n, 1).astype(np.float32)
    large = max_exact + (np.log(nf / np.float32(max_exact)) / np.float32(math.log(MAX_DISTANCE / max_exact))
                         * np.float32(NUM_BUCKETS - max_exact)).astype(np.int32)
    large = np.minimum(large, NUM_BUCKETS - 1)
    return np.where(n < max_exact, n, large)


def _mix_tail(mix_in, wo_ref, x_ref, gm_ref, g_ref, sh_ref, sc_ref, wr_ref, br_ref,
              x_out, h_out, lg_out):
    mix = _dot(mix_in, wo_ref[...])
    x_new = x_ref[...] + gm_ref[0] * mix
    x_out[...] = x_new
    h = _norm_mod(x_new, g_ref[...], sh_ref[0], sc_ref[0])
    h_out[...] = h.astype(BF16)
    lg_out[...] = jnp.dot(h, wr_ref[...], preferred_element_type=F32,
                          precision=lax.Precision.HIGHEST) + br_ref[...]


def _merge_oproj_kernel(o0_ref, o1_ref, o2_ref, p1_ref, p2_ref, lse_ref, ex_ref, wo_ref, x_ref, gm_ref, g_ref,
                        sh_ref, sc_ref, wr_ref, br_ref, x_out, h_out, lg_out):
    lse = lse_ref[...]
    l0, l1, l2 = lse[:, 0:16], lse[:, 16:32], lse[:, 32:48]
    m = jnp.maximum(jnp.maximum(l0, l1), l2)
    e0, e1, e2 = jnp.exp(l0 - m), jnp.exp(l1 - m), jnp.exp(l2 - m)
    inv = 1.0 / (e0 + e1 + e2)
    ex = ex_ref[...]
    o1 = _dot(p1_ref[...], o1_ref[...])
    o2 = _dot(p2_ref[...], o2_ref[0].reshape(ROW_TILE, o2_ref.shape[-1]))
    merged = (_dot((e0 * inv).astype(BF16), ex) * o0_ref[...].astype(F32)
              + _dot((e1 * inv).astype(BF16), ex) * o1
              + _dot((e2 * inv).astype(BF16), ex) * o2)
    _mix_tail(merged.astype(BF16), wo_ref, x_ref, gm_ref, g_ref, sh_ref, sc_ref, wr_ref, br_ref,
              x_out, h_out, lg_out)


def _oproj_kernel(o_ref, wo_ref, x_ref, gm_ref, g_ref, sh_ref, sc_ref, wr_ref, br_ref,
                  x_out, h_out, lg_out):
    _mix_tail(o_ref[...], wo_ref, x_ref, gm_ref, g_ref, sh_ref, sc_ref, wr_ref, br_ref,
              x_out, h_out, lg_out)


def _mixer_tail(mix_inputs, w_o, x, gate_m, g_ffn, shift_f, scale_f, w_r, b_r, seq, merged):
    t, d = x.shape
    tpb = seq // ROW_TILE
    row = lambda i: (i, 0)
    full = lambda i: (0, 0)
    per_b = lambda i: (i // tpb, 0, 0)
    tail_specs = [
        pl.BlockSpec((d, d), full),
        pl.BlockSpec((ROW_TILE, d), row),
        pl.BlockSpec((1, 1, d), per_b),
        pl.BlockSpec((1, d), full),
        pl.BlockSpec((1, 1, d), per_b),
        pl.BlockSpec((1, 1, d), per_b),
        pl.BlockSpec((d, N_EXPERTS), full),
        pl.BlockSpec((1, N_EXPERTS), full),
    ]
    tail_args = (w_o, x, gate_m, g_ffn.reshape(1, d), shift_f, scale_f, w_r, b_r.reshape(1, N_EXPERTS))
    if merged:
        o0, o1, o2, lse = mix_inputs
        expand = jnp.asarray(np.kron(np.eye(N_HEADS, dtype=np.float32),
                                     np.ones((1, HEAD_DIM), np.float32)), BF16)
        d1, d2 = DIL_PAIRS[1][1], DIL_PAIRS[2][1]
        assert _perm_tile(d1) == ROW_TILE
        t2 = _perm_tile(d2)
        sub2 = ROW_TILE // d2
        o2v = o2.reshape(t // t2, d2, t2 // d2, d)
        per2 = t2 // ROW_TILE
        kern = _merge_oproj_kernel
        specs = [pl.BlockSpec((ROW_TILE, d), row), pl.BlockSpec((ROW_TILE, d), row),
                 pl.BlockSpec((1, d2, sub2, d), lambda i: (i // per2, 0, i % per2, 0)),
                 pl.BlockSpec((ROW_TILE, ROW_TILE), full), pl.BlockSpec((ROW_TILE, ROW_TILE), full),
                 pl.BlockSpec((ROW_TILE, 3 * N_HEADS), row), pl.BlockSpec((N_HEADS, d), full)]
        args = (o0, o1, o2v, _perm_matrix(ROW_TILE, d1), _perm_matrix(ROW_TILE, d2), lse, expand)
    else:
        kern = _oproj_kernel
        specs = [pl.BlockSpec((ROW_TILE, d), row)]
        args = mix_inputs
    return pl.pallas_call(
        kern,
        out_shape=(jax.ShapeDtypeStruct((t, d), F32), jax.ShapeDtypeStruct((t, d), BF16),
                   jax.ShapeDtypeStruct((t, N_EXPERTS), F32)),
        grid=(t // ROW_TILE,),
        in_specs=specs + tail_specs,
        out_specs=(pl.BlockSpec((ROW_TILE, d), row), pl.BlockSpec((ROW_TILE, d), row),
                   pl.BlockSpec((ROW_TILE, N_EXPERTS), row)),
        compiler_params=_params(("parallel",)),
        name="mixer_tail_merge" if merged else "mixer_tail",
    )(*args, *tail_args)


def _fox_kernel(q_ref, k_ref, v_ref, o_ref, vt_scr):
    qi = pl.program_id(2)
    seq = k_ref.shape[0]
    nk = seq // FOX_TILE
    half = HEAD_DIM

    ext = FOX_SUM_ROWS
    @pl.when(qi == 0)
    def _():
        lane = lax.broadcasted_iota(jnp.int32, (FOX_TILE, 2 * half), 1)
        erow = lax.broadcasted_iota(jnp.int32, (2 * ext, 2 * FOX_TILE), 0)
        ecol = lax.broadcasted_iota(jnp.int32, (2 * ext, 2 * FOX_TILE), 1)
        sum_rows = jnp.where((erow < ext) == (ecol < FOX_TILE), 1.0, 0.0).astype(BF16)
        for j in range(nk):
            vj = v_ref[j * FOX_TILE:(j + 1) * FOX_TILE, :].astype(F32)
            vt_scr[j, :2 * half, :FOX_TILE] = jnp.where(lane < half, vj, 0.0).T.astype(BF16)
            vt_scr[j, :2 * half, FOX_TILE:] = jnp.where(lane >= half, vj, 0.0).T.astype(BF16)
            vt_scr[j, 2 * half:, :] = sum_rows

    q0 = q_ref[:, :AUG]
    q1 = q_ref[:, AUG:]

    def softmax_step(s, m):
        m_new = jnp.maximum(m, jnp.max(s, axis=0, keepdims=True))
        return m_new, jnp.exp2(m - m_new), jnp.exp2((s - m_new).astype(BF16))

    def step(kj, carry, masked):
        m0, m1, acc = carry
        rows = pl.ds(pl.multiple_of(kj * FOX_TILE, FOX_TILE), FOX_TILE)
        s0 = _dot_nt(k_ref[rows, :AUG], q0)
        s1 = _dot_nt(k_ref[rows, AUG:], q1)
        if masked:
            kpos = lax.broadcasted_iota(jnp.int32, (FOX_TILE, FOX_TILE), 0)
            qpos = lax.broadcasted_iota(jnp.int32, (FOX_TILE, FOX_TILE), 1)
            keep = kpos <= qpos
            s0 = jnp.where(keep, s0, NEG)
            s1 = jnp.where(keep, s1, NEG)
        m0, a0, p0 = softmax_step(s0, m0)
        m1, a1, p1 = softmax_step(s1, m1)
        pv = _dot(vt_scr[kj], jnp.concatenate([p0, p1], axis=0))
        alpha = jnp.concatenate([jnp.broadcast_to(a0, (half, FOX_TILE)), jnp.broadcast_to(a1, (half, FOX_TILE)),
                                 jnp.broadcast_to(a0, (ext, FOX_TILE)), jnp.broadcast_to(a1, (ext, FOX_TILE))],
                                axis=0)
        return m0, m1, alpha * acc + pv

    neg = jnp.full((1, FOX_TILE), NEG, F32)
    init = (neg, neg, jnp.zeros((2 * half + 2 * ext, FOX_TILE), F32))
    carry = lax.fori_loop(0, qi, lambda kj, c: step(kj, c, False), init)
    _, _, acc = step(qi, carry, True)
    inv0 = 1.0 / acc[2 * half:2 * half + 1, :]
    inv1 = 1.0 / acc[2 * half + ext:2 * half + ext + 1, :]
    inv = jnp.concatenate([jnp.broadcast_to(inv0, (half, FOX_TILE)),
                           jnp.broadcast_to(inv1, (half, FOX_TILE))], axis=0)
    o_ref[...] = (acc[:2 * half, :] * inv).T.astype(o_ref.dtype)


def _fox(q_aug, k_aug, v, n_batch, seq):
    t = q_aug.shape[0]
    nq = seq // FOX_TILE
    pairs = N_HEADS // 2
    return pl.pallas_call(
        _fox_kernel,
        out_shape=jax.ShapeDtypeStruct((t, N_HEADS * HEAD_DIM), BF16),
        grid=(n_batch, pairs, nq),
        in_specs=[
            pl.BlockSpec((FOX_TILE, 2 * AUG), lambda b, p, qi: (b * nq + qi, p)),
            pl.BlockSpec((seq, 2 * AUG), lambda b, p, qi: (b, p)),
            pl.BlockSpec((seq, 2 * HEAD_DIM), lambda b, p, qi: (b, p)),
        ],
        out_specs=pl.BlockSpec((FOX_TILE, 2 * HEAD_DIM), lambda b, p, qi: (b * nq + qi, p)),
        scratch_shapes=[pltpu.VMEM((nq, 2 * HEAD_DIM + 2 * FOX_SUM_ROWS, 2 * FOX_TILE), BF16)],
        compiler_params=_params(("parallel", "parallel", "arbitrary")),
        name="fox",
    )(q_aug, k_aug, v)


def _expert_kernel(be_ref, nu_ref, x_ref, wgu_ref, bgu_ref, wd_ref, bd_ref, o_ref, wgu_bf, wd_bf):
    i = pl.program_id(0)
    d_ff = wd_ref.shape[1]

    @pl.when(jnp.logical_or(i == 0, be_ref[i] != be_ref[jnp.maximum(i - 1, 0)]))
    def _():
        wgu_bf[...] = wgu_ref[0].astype(BF16)
        wd_bf[...] = wd_ref[0].astype(BF16)

    @pl.when(i < nu_ref[0])
    def _():
        gu = _dot(x_ref[...], wgu_bf[...]) + bgu_ref[0]
        g = jnp.minimum(gu[:, :d_ff], SWIGLU_LIMIT)
        u = jnp.clip(gu[:, d_ff:], -SWIGLU_LIMIT, SWIGLU_LIMIT)
        act = (u + 1.0) * g * (1.0 / (1.0 + jnp.exp(-SWIGLU_ALPHA * g)))
        y = _dot(act.astype(BF16), wd_bf[...]) + bd_ref[0]
        o_ref[...] = y.astype(o_ref.dtype)

    @pl.when(i >= nu_ref[0])
    def _():
        o_ref[...] = jnp.zeros_like(o_ref)


def _experts(xbuf, w_gu, b_gu, w_down, b_down, block_expert, n_used, layer):
    n_rows, d = xbuf.shape
    depth, e, _, n_gu = w_gu.shape
    d_ff = w_down.shape[2]
    n_blocks = n_rows // MOE_TILE
    grid_spec = pltpu.PrefetchScalarGridSpec(
        num_scalar_prefetch=2,
        grid=(n_blocks,),
        in_specs=[
            pl.BlockSpec((MOE_TILE, d), lambda i, be, nu: (i, 0)),
            pl.BlockSpec((None, 1, d, n_gu), lambda i, be, nu: (layer, be[i], 0, 0)),
            pl.BlockSpec((None, 1, 1, n_gu), lambda i, be, nu: (layer, be[i], 0, 0)),
            pl.BlockSpec((None, 1, d_ff, d), lambda i, be, nu: (layer, be[i], 0, 0)),
            pl.BlockSpec((None, 1, 1, d), lambda i, be, nu: (layer, be[i], 0, 0)),
        ],
        out_specs=pl.BlockSpec((MOE_TILE, d), lambda i, be, nu: (i, 0)),
        scratch_shapes=[pltpu.VMEM((d, n_gu), BF16), pltpu.VMEM((d_ff, d), BF16)],
    )
    return pl.pallas_call(
        _expert_kernel,
        out_shape=jax.ShapeDtypeStruct((n_rows, d), BF16),
        grid_spec=grid_spec,
        compiler_params=_params(("arbitrary",)),
        name="experts",
    )(block_expert, n_used, xbuf, w_gu, b_gu.reshape(depth, e, 1, n_gu), w_down, b_down.reshape(depth, e, 1, d))


def _router_kernel(lg_ref, tri_ref, upper_ref, dest_ref, gate_ref, cnt_ref, carry, base):
    phase = pl.program_id(0)
    i = pl.program_id(1)
    tm = lg_ref.shape[0]
    lane = lax.broadcasted_iota(jnp.int32, (tm, N_EXPERTS), 1).astype(F32)
    wide = lax.broadcasted_iota(jnp.int32, (tm, 128), 1)

    work = lg_ref[...]
    hots, vals = [], []
    for _ in range(TOP_K):
        m = jnp.max(work, axis=1, keepdims=True)
        idx = jnp.min(jnp.where(work == m, lane, float(N_EXPERTS)), axis=1, keepdims=True)
        hot = lane == idx
        hots.append(hot)
        vals.append(m)
        work = jnp.where(hot, -jnp.inf, work)
    exps = [jnp.exp(v - vals[0]) for v in vals]
    inv = 1.0 / (exps[0] + exps[1] + exps[2] + exps[3])
    gate_tile = jnp.zeros((tm, 128), F32)
    for k in range(TOP_K):
        gate_tile = jnp.where(wide == k, exps[k] * inv, gate_tile)
    gate_ref[...] = gate_tile

    chosen = jnp.zeros((tm, N_EXPERTS), F32)
    for hot in hots:
        chosen = chosen + hot.astype(F32)

    @pl.when(jnp.logical_and(phase == 0, i == 0))
    def _():
        carry[...] = jnp.zeros_like(carry)
        base[...] = jnp.zeros_like(base)

    @pl.when(jnp.logical_and(phase == 1, i == 0))
    def _():
        counts = carry[...]
        padded = jnp.floor((counts + (MOE_TILE - 1)) * (1.0 / MOE_TILE)) * MOE_TILE
        base[...] = jnp.dot(jnp.broadcast_to(padded, (8, N_EXPERTS)), upper_ref[...],
                            preferred_element_type=F32, precision=lax.Precision.HIGHEST)[0:1, :]
        carry[...] = jnp.zeros_like(carry)

    ahead = _dot(tri_ref[...], chosen.astype(BF16))
    pos = ahead + carry[...] + base[...]
    dest_tile = jnp.zeros((tm, 128), F32)
    for k in range(TOP_K):
        row = jnp.sum(jnp.where(hots[k], pos, 0.0), axis=1, keepdims=True)
        dest_tile = jnp.where(wide == k, row, dest_tile)
    dest_ref[...] = dest_tile.astype(jnp.int32)
    carry[...] = carry[...] + jnp.sum(chosen, axis=0, keepdims=True)
    cnt_ref[...] = carry[...]


def _router(logits):
    t = logits.shape[0]
    nt = t // ROW_TILE
    tri = jnp.asarray(np.tril(np.ones((ROW_TILE, ROW_TILE), np.float32), -1), BF16)
    upper = jnp.asarray(np.triu(np.ones((N_EXPERTS, N_EXPERTS), np.float32), 1), F32)
    return pl.pallas_call(
        _router_kernel,
        out_shape=(jax.ShapeDtypeStruct((t, 128), jnp.int32), jax.ShapeDtypeStruct((t, 128), F32),
                   jax.ShapeDtypeStruct((1, N_EXPERTS), F32)),
        grid=(2, nt),
        in_specs=[
            pl.BlockSpec((ROW_TILE, N_EXPERTS), lambda p, i: (i, 0)),
            pl.BlockSpec((ROW_TILE, ROW_TILE), lambda p, i: (0, 0)),
            pl.BlockSpec((N_EXPERTS, N_EXPERTS), lambda p, i: (0, 0)),
        ],
        out_specs=(pl.BlockSpec((ROW_TILE, 128), lambda p, i: (i, 0)),
                   pl.BlockSpec((ROW_TILE, 128), lambda p, i: (i, 0)),
                   pl.BlockSpec((1, N_EXPERTS), lambda p, i: (0, 0))),
        scratch_shapes=[pltpu.VMEM((1, N_EXPERTS), F32), pltpu.VMEM((1, N_EXPERTS), F32)],
        compiler_params=_params(("arbitrary", "arbitrary")),
        name="router",
    )(logits, tri, upper)


def _route(logits):
    t = logits.shape[0]
    n_slots = t * TOP_K
    n_rows = n_slots + N_EXPERTS * MOE_TILE
    n_blocks = n_rows // MOE_TILE
    dest_w, gates_w, counts = _router(logits)
    counts = counts.reshape(N_EXPERTS).astype(jnp.int32)
    pad_ends = jnp.cumsum((counts + MOE_TILE - 1) // MOE_TILE * MOE_TILE)
    block_start = jnp.arange(n_blocks, dtype=jnp.int32) * MOE_TILE
    block_expert = jnp.minimum(jnp.sum(block_start[:, None] >= pad_ends[None, :], axis=1),
                               N_EXPERTS - 1).astype(jnp.int32)
    n_used = (pad_ends[-1] // MOE_TILE).astype(jnp.int32).reshape(1)
    dest = dest_w[:, :TOP_K].T.reshape(-1)
    tok = jnp.tile(jnp.arange(t, dtype=jnp.int32), TOP_K)
    tok_rows = jnp.zeros((n_rows,), jnp.int32).at[dest].set(tok, unique_indices=True)
    return dest, tok_rows, gates_w, block_expert, n_used


def _combine_kernel(x_ref, y0_ref, y1_ref, y2_ref, y3_ref, gt_ref, gf_ref, g_ref, o_ref, *, final):
    gt = gt_ref[...]
    moe = (gt[:, 0:1] * y0_ref[...].astype(F32) + gt[:, 1:2] * y1_ref[...].astype(F32)
           + gt[:, 2:3] * y2_ref[...].astype(F32) + gt[:, 3:4] * y3_ref[...].astype(F32))
    x_new = x_ref[...] + gf_ref[0] * moe
    if final:
        x_new = _norm_mod(x_new, g_ref[...], None, None)
    o_ref[...] = x_new


def _combine(x, y_slots, gates, gate_f, g_final, seq, final):
    t, d = x.shape
    tpb = seq // ROW_TILE
    nt = t // ROW_TILE
    assert TOP_K == 4
    return pl.pallas_call(
        functools.partial(_combine_kernel, final=final),
        out_shape=jax.ShapeDtypeStruct((t, d), F32),
        grid=(nt,),
        in_specs=[pl.BlockSpec((ROW_TILE, d), lambda i: (i, 0))] + [
            pl.BlockSpec((ROW_TILE, d), functools.partial(lambda i, k: (k * nt + i, 0), k=k))
            for k in range(TOP_K)] + [
            pl.BlockSpec((ROW_TILE, 128), lambda i: (i, 0)),
            pl.BlockSpec((1, 1, d), lambda i: (i // tpb, 0, 0)),
            pl.BlockSpec((1, d), lambda i: (0, 0)),
        ],
        out_specs=pl.BlockSpec((ROW_TILE, d), lambda i: (i, 0)),
        compiler_params=_params(("parallel",)),
        name="combine_final" if final else "combine",
    )(x, y_slots, y_slots, y_slots, y_slots, gates, gate_f, g_final.reshape(1, d))


def _moe(x, h, logits, gate_f, w_gu, b_gu, w_down, b_down, g_final, seq, layer, final):
    t, d = x.shape
    dest, tok_rows, gates, block_expert, n_used = _route(logits)
    xbuf = jnp.take(h, tok_rows, axis=0)
    ybuf = _experts(xbuf, w_gu, b_gu, w_down, b_down, block_expert, n_used, layer)
    y_slots = jnp.take(ybuf, dest, axis=0)
    return _combine(x, y_slots, gates, gate_f, g_final, seq, final)


def kernel(x, c, ada_w, ada_b, norm_mix_g, norm_ffn_g, a_w_qkv, a_w_o, rel_bias, kv_norm_g, w_kvf, b_f,
           b_w_q, b_w_o, router_w, router_b, w_gu, b_gu, w_down, b_down, final_norm_g):
    n_batch, seq, d = x.shape
    t = n_batch * seq
    width = N_HEADS * HEAD_DIM
    xf = x.reshape(t, d)

    ada = _ada(c, ada_w, ada_b)
    mods = [[ada[l, :, i * d:(i + 1) * d].reshape(n_batch, 1, d) for i in range(6)] for l in range(2)]

    shift_m, scale_m, gate_m, shift_f, scale_f, gate_f = mods[0]
    qscale = np.ones((3, 3, 1), np.float32)
    qscale[:, 0] = HEAD_DIM ** -0.5
    w_qkv = (a_w_qkv[0].reshape(d, 3, 3, width) * qscale).astype(BF16)

    qi = np.arange(DIL_BLOCK, dtype=np.int32)[:, None]
    kj = np.arange(2 * DIL_BLOCK, dtype=np.int32)[None, :]
    delta = qi + DIL_BLOCK - kj
    in_band = (delta >= 0) & (delta <= DIL_BLOCK)
    outs, lses = [], []
    for g, (window, dilation) in enumerate(DIL_PAIRS):
        assert window // dilation == DIL_BLOCK
        qkv = _proj(xf, norm_mix_g[0], shift_m, scale_m, w_qkv[:, g].reshape(d, 3 * width), seq, dilation, g)
        bucket = _t5_bucket_np(np.clip(delta, 0, None) * dilation)
        tab = rel_bias[:, g * N_HEADS:(g + 1) * N_HEADS].astype(F32)
        bias = jnp.where(in_band, jnp.transpose(tab[bucket], (2, 0, 1)), NEG)
        bias_first = jnp.where(kj >= DIL_BLOCK, bias, NEG)
        o, lse = _dilated_group(qkv, jnp.stack([bias_first, bias]), g, dilation, n_batch, seq)
        outs.append(o)
        lses.append(_unpermute_rows(lse, dilation)[:, :N_HEADS])
    lse_all = jnp.concatenate(lses, axis=1)
    x1, h1, logits1 = _mixer_tail((outs[0], outs[1], outs[2], lse_all), a_w_o[0].astype(BF16), xf, gate_m,
                                  norm_ffn_g[0], shift_f, scale_f, router_w[0], router_b[0], seq, True)
    x2 = _moe(x1, h1, logits1, gate_f, w_gu, b_gu, w_down, b_down, final_norm_g, seq, 0, False)

    w_f = jnp.pad(jnp.tile(w_kvf[:, 2 * width:], (1, 3)), ((0, 0), (0, 128 - 3 * N_HEADS))).astype(BF16)
    b_fp = jnp.pad(jnp.tile(b_f, 3), (0, 128 - 3 * N_HEADS)).reshape(1, 128)
    k_aug, v_sh, fcum = _kvf(x2, kv_norm_g, _aug_weight(w_kvf[:, :width]), w_kvf[:, width:2 * width].astype(BF16),
                             w_f, b_fp, n_batch)

    shift_m, scale_m, gate_m, shift_f, scale_f, gate_f = mods[1]
    q_aug = _qaug(x2, norm_mix_g[1], shift_m, scale_m, _aug_weight(b_w_q[0] * (HEAD_DIM ** -0.5 * LOG2E)), fcum, seq)
    o1 = _fox(q_aug, k_aug, v_sh, n_batch, seq)

    x3, h3, logits3 = _mixer_tail((o1,), b_w_o[0].astype(BF16), x2, gate_m, norm_ffn_g[1], shift_f, scale_f,
                                  router_w[1], router_b[1], seq, False)
    out = _moe(x3, h3, logits3, gate_f, w_gu, b_gu, w_down, b_down, final_norm_g, seq, 1, True)
    return out.reshape(n_batch, seq, d)
```

```python
import functools
import math

import numpy as np
import jax
import jax.numpy as jnp
from jax import lax
from jax.experimental import pallas as pl
from jax.experimental.pallas import tpu as pltpu

F32 = jnp.float32
BF16 = jnp.bfloat16

D_MODEL = 1024
HEAD_DIM = 64
N_HEADS = 16
DIL_PAIRS = ((128, 1), (512, 4), (2048, 16))
DIL_BLOCK = 128
NUM_BUCKETS = 32
MAX_DISTANCE = 2048
N_EXPERTS = 32
TOP_K = 4
SWIGLU_LIMIT = 7.0
SWIGLU_ALPHA = 1.702
RMS_EPS = 1e-6
NEG = -1e30
LOG2E = math.log2(math.e)

ROW_TILE = 512
MOE_TILE = 512
FOX_TILE = 512
AUG = 128
FOX_SUM_ROWS = 8
VMEM_LIMIT = 56 * 1024 * 1024


def _params(sem, vmem=VMEM_LIMIT):
    return pltpu.CompilerParams(dimension_semantics=sem, vmem_limit_bytes=vmem)


def _dot(a, b):
    return jnp.dot(a, b, preferred_element_type=F32)


def _dot_nt(a, b):
    return lax.dot_general(a, b, (((1,), (1,)), ((), ())), preferred_element_type=F32)


def _dot_tn(a, b):
    return lax.dot_general(a, b, (((0,), (0,)), ((), ())), preferred_element_type=F32)


def _norm_mod(x, g, shift, scale):
    ms = jnp.mean(x * x, axis=-1, keepdims=True)
    y = x * lax.rsqrt(ms + RMS_EPS) * g
    if scale is not None:
        y = y * (1.0 + scale) + shift
    return y


def _ada_kernel(c_ref, w_ref, b_ref, o_ref):
    c = c_ref[...]
    act = c * (1.0 / (1.0 + jnp.exp(-c)))
    o_ref[0] = jnp.dot(act, w_ref[0], preferred_element_type=F32,
                       precision=lax.Precision.HIGHEST) + b_ref[0]


def _ada(c, ada_w, ada_b):
    depth, d, n = ada_w.shape
    bsz = c.shape[0]
    tn = 1536
    return pl.pallas_call(
        _ada_kernel,
        out_shape=jax.ShapeDtypeStruct((depth, bsz, n), F32),
        grid=(depth, n // tn),
        in_specs=[
            pl.BlockSpec((bsz, d), lambda l, j: (0, 0)),
            pl.BlockSpec((1, d, tn), lambda l, j: (l, 0, j)),
            pl.BlockSpec((1, 1, tn), lambda l, j: (l, 0, j)),
        ],
        out_specs=pl.BlockSpec((1, bsz, tn), lambda l, j: (l, 0, j)),
        compiler_params=_params(("parallel", "parallel")),
        name="ada",
    )(c, ada_w, ada_b.reshape(depth, 1, n))


def _perm_tile(dilation):
    return max(ROW_TILE, DIL_BLOCK * dilation)


def _proj_kernel(x_ref, g_ref, sh_ref, sc_ref, w_ref, o_ref, h_scr, *xs_scr, dilation):
    @pl.when(pl.program_id(1) == 0)
    def _():
        if dilation == 1:
            h_scr[...] = _norm_mod(x_ref[...], g_ref[...], sh_ref[0], sc_ref[0]).astype(BF16)
        else:
            (xs,) = xs_scr
            n_lane = xs.shape[0]
            for c in range(n_lane):
                xs[c] = x_ref[:, c * 128:(c + 1) * 128]
            chunk = x_ref.shape[0] // dilation
            for r in range(dilation):
                xr = jnp.concatenate([xs[c, pl.ds(r, chunk, stride=dilation), :] for c in range(n_lane)], axis=1)
                h_scr[r * chunk:(r + 1) * chunk, :] = _norm_mod(xr, g_ref[...], sh_ref[0], sc_ref[0]).astype(BF16)

    o_ref[...] = _dot(h_scr[...], w_ref[...]).astype(o_ref.dtype)


def _proj(x, g, shift, scale, w, seq, dilation, group):
    t, d = x.shape
    n = w.shape[1]
    tm = _perm_tile(dilation)
    tn = d
    tpb = seq // tm
    return pl.pallas_call(
        functools.partial(_proj_kernel, dilation=dilation),
        out_shape=jax.ShapeDtypeStruct((t, n), BF16),
        grid=(t // tm, n // tn),
        in_specs=[
            pl.BlockSpec((tm, d), lambda i, j: (i, 0)),
            pl.BlockSpec((1, d), lambda i, j: (0, 0)),
            pl.BlockSpec((1, 1, d), lambda i, j: (i // tpb, 0, 0)),
            pl.BlockSpec((1, 1, d), lambda i, j: (i // tpb, 0, 0)),
            pl.BlockSpec((d, tn), lambda i, j: (0, j)),
        ],
        out_specs=pl.BlockSpec((tm, tn), lambda i, j: (i, j)),
        scratch_shapes=[pltpu.VMEM((tm, d), BF16)] + (
            [pltpu.VMEM((d // 128, tm, 128), F32)] if dilation > 1 else []),
        compiler_params=_params(("parallel", "arbitrary")),
        name=f"proj{group}",
    )(x, g.reshape(1, d), shift, scale, w)


def _split3(f):
    def top(v):
        return lax.bitcast_convert_type(lax.bitcast_convert_type(v, jnp.uint32) & jnp.uint32(0xFFFF0000), F32)
    hi = top(f)
    r1 = f - hi
    mid = top(r1)
    return hi, mid, r1 - mid


def _forget_pieces(fc):
    hi, mid, lo = _split3(fc)
    lane = lax.broadcasted_iota(jnp.int32, fc.shape, 1)
    x = jnp.where(lane < N_HEADS, hi, jnp.where(lane < 2 * N_HEADS, mid, jnp.where(lane < 3 * N_HEADS, lo, 0.0)))
    return x.astype(BF16)


def _aug_tables(key_side):
    place = np.zeros((128, N_HEADS * AUG), np.float32)
    const = np.zeros((1, N_HEADS * AUG), np.float32)
    for h in range(N_HEADS):
        base = h * AUG + HEAD_DIM
        for piece in range(3):
            if key_side:
                place[piece * N_HEADS + h, base + piece] = 1.0
                const[0, base + 3 + piece] = 1.0
            else:
                place[piece * N_HEADS + h, base + 3 + piece] = 1.0
                const[0, base + piece] = -1.0
    return jnp.asarray(place, BF16), jnp.asarray(const, F32)


def _aug_weight(w):
    d = w.shape[0]
    w = jnp.pad(w.reshape(d, N_HEADS, HEAD_DIM), ((0, 0), (0, 0), (0, AUG - HEAD_DIM)))
    return w.reshape(d, N_HEADS * AUG).astype(BF16)


def _kvf_kernel(x_ref, g_ref, wk_ref, wv_ref, wf_ref, bf_ref, tri_ref, pl_ref, cr_ref,
                k_ref, v_ref, f_ref, carry, *, tpb):
    h = _norm_mod(x_ref[...], g_ref[...], None, None).astype(BF16)
    v_ref[...] = _dot(h, wv_ref[...]).astype(v_ref.dtype)
    z = _dot(h, wf_ref[...]) + bf_ref[...]
    lf = jnp.minimum(z, 0.0) - jnp.log(1.0 + jnp.exp(-jnp.abs(z)))
    hi, mid, lo = _split3(lf)
    tri = tri_ref[...]
    cs = _dot(tri, hi.astype(BF16)) + _dot(tri, mid.astype(BF16)) + _dot(tri, lo.astype(BF16))

    @pl.when(pl.program_id(0) % tpb == 0)
    def _():
        carry[...] = jnp.zeros_like(carry)

    cs = cs + carry[...]
    carry[...] = cs[ROW_TILE - 1:ROW_TILE, :]
    f2 = cs * LOG2E
    f_ref[...] = f2
    k_ref[...] = (_dot(h, wk_ref[...]) + _dot(_forget_pieces(f2), pl_ref[...]) + cr_ref[...]).astype(k_ref.dtype)


def _kvf(x, g, w_k, w_v, w_f, b_f, n_batch):
    t, d = x.shape
    n_aug = N_HEADS * AUG
    tri = jnp.asarray(np.tril(np.ones((ROW_TILE, ROW_TILE), np.float32)), BF16)
    place, const = _aug_tables(True)
    assert (t // ROW_TILE) % n_batch == 0
    full = lambda i: (0, 0)
    row = lambda i: (i, 0)
    return pl.pallas_call(
        functools.partial(_kvf_kernel, tpb=t // ROW_TILE // n_batch),
        out_shape=(jax.ShapeDtypeStruct((t, n_aug), BF16), jax.ShapeDtypeStruct((t, d), BF16),
                   jax.ShapeDtypeStruct((t, 128), F32)),
        grid=(t // ROW_TILE,),
        in_specs=[
            pl.BlockSpec((ROW_TILE, d), row),
            pl.BlockSpec((1, d), full),
            pl.BlockSpec((d, n_aug), full),
            pl.BlockSpec((d, d), full),
            pl.BlockSpec((d, 128), full),
            pl.BlockSpec((1, 128), full),
            pl.BlockSpec((ROW_TILE, ROW_TILE), full),
            pl.BlockSpec((128, n_aug), full),
            pl.BlockSpec((1, n_aug), full),
        ],
        out_specs=(pl.BlockSpec((ROW_TILE, n_aug), row), pl.BlockSpec((ROW_TILE, d), row),
                   pl.BlockSpec((ROW_TILE, 128), row)),
        scratch_shapes=[pltpu.VMEM((1, 128), F32)],
        compiler_params=_params(("arbitrary",)),
        name="kvf",
    )(x, g.reshape(1, d), w_k, w_v, w_f, b_f, tri, place, const)


def _qaug_kernel(x_ref, g_ref, sh_ref, sc_ref, w_ref, f_ref, pl_ref, cr_ref, o_ref):
    h = _norm_mod(x_ref[...], g_ref[...], sh_ref[0], sc_ref[0]).astype(BF16)
    o_ref[...] = (_dot(h, w_ref[...]) + _dot(_forget_pieces(f_ref[...]), pl_ref[...])
                  + cr_ref[...]).astype(o_ref.dtype)


def _qaug(x, g, shift, scale, w, fcum, seq):
    t, d = x.shape
    n_aug = N_HEADS * AUG
    tpb = seq // ROW_TILE
    place, const = _aug_tables(False)
    full = lambda i: (0, 0)
    row = lambda i: (i, 0)
    per_b = lambda i: (i // tpb, 0, 0)
    return pl.pallas_call(
        _qaug_kernel,
        out_shape=jax.ShapeDtypeStruct((t, n_aug), BF16),
        grid=(t // ROW_TILE,),
        in_specs=[
            pl.BlockSpec((ROW_TILE, d), row),
            pl.BlockSpec((1, d), full),
            pl.BlockSpec((1, 1, d), per_b),
            pl.BlockSpec((1, 1, d), per_b),
            pl.BlockSpec((d, n_aug), full),
            pl.BlockSpec((ROW_TILE, 128), row),
            pl.BlockSpec((128, n_aug), full),
            pl.BlockSpec((1, n_aug), full),
        ],
        out_specs=pl.BlockSpec((ROW_TILE, n_aug), row),
        compiler_params=_params(("parallel",)),
        name="qaug",
    )(x, g.reshape(1, d), shift, scale, w, fcum, place, const)


def _dil_kernel(q_ref, kp_ref, kc_ref, vp_ref, vc_ref, bias_ref, o_ref, lse_ref):
    table = jnp.minimum(pl.program_id(2), 1)
    pair_w = 2 * HEAD_DIM
    lane = lax.broadcasted_iota(jnp.int32, (DIL_BLOCK, pair_w), 1)
    first = lane < HEAD_DIM
    first2 = lax.broadcasted_iota(jnp.int32, (2 * DIL_BLOCK, pair_w), 1) < HEAD_DIM
    lse_tile = jnp.zeros((DIL_BLOCK, pair_w), F32)
    zero = jnp.zeros((), BF16)
    for pair in range(N_HEADS // 2):
        sl = slice(pair * pair_w, (pair + 1) * pair_w)
        q = q_ref[:, sl]
        k2 = jnp.concatenate([kp_ref[:, sl], kc_ref[:, sl]], axis=0)
        v2 = jnp.concatenate([vp_ref[:, sl], vc_ref[:, sl]], axis=0)
        probs, invs = [], []
        for which in range(2):
            h = 2 * pair + which
            qh = jnp.where(first, q, zero) if which == 0 else jnp.where(first, zero, q)
            s = _dot_nt(qh, k2) + bias_ref[table, h]
            m = jnp.max(s, axis=-1, keepdims=True)
            p = jnp.exp(s - m)
            den = jnp.sum(p, axis=-1, keepdims=True)
            probs.append(p.astype(BF16))
            invs.append(1.0 / den)
            lse_tile = jnp.where(lane == h, m + jnp.log(den), lse_tile)
        vcat = jnp.concatenate([jnp.where(first2, v2, zero), jnp.where(first2, zero, v2)], axis=0)
        o = _dot(jnp.concatenate(probs, axis=1), vcat)
        o_ref[:, sl] = (o * jnp.where(first, invs[0], invs[1])).astype(o_ref.dtype)
    lse_ref[...] = lse_tile


def _dilated_group(qkv, bias, group, dilation, n_batch, seq):
    t = qkv.shape[0]
    width = N_HEADS * HEAD_DIM
    nb = seq // dilation // DIL_BLOCK
    bpb = seq // DIL_BLOCK

    def blk(b, r, n):
        return b * bpb + n * dilation + r

    def spec(part, prev):
        if prev:
            return pl.BlockSpec((DIL_BLOCK, width), lambda b, r, n: (blk(b, r, jnp.maximum(n - 1, 0)), part))
        return pl.BlockSpec((DIL_BLOCK, width), lambda b, r, n: (blk(b, r, n), part))

    return pl.pallas_call(
        _dil_kernel,
        out_shape=(jax.ShapeDtypeStruct((t, width), BF16), jax.ShapeDtypeStruct((t, 2 * HEAD_DIM), F32)),
        grid=(n_batch, dilation, nb),
        in_specs=[spec(0, False), spec(1, True), spec(1, False), spec(2, True), spec(2, False),
                  pl.BlockSpec((2, N_HEADS, DIL_BLOCK, 2 * DIL_BLOCK), lambda b, r, n: (0, 0, 0, 0))],
        out_specs=(pl.BlockSpec((DIL_BLOCK, width), lambda b, r, n: (blk(b, r, n), 0)),
                   pl.BlockSpec((DIL_BLOCK, 2 * HEAD_DIM), lambda b, r, n: (blk(b, r, n), 0))),
        compiler_params=_params(("parallel", "parallel", "arbitrary")),
        name=f"dilated{group}",
    )(qkv, qkv, qkv, qkv, qkv, bias)


def _unpermute_rows(a, dilation):
    if dilation == 1:
        return a
    t, c = a.shape
    tm = _perm_tile(dilation)
    return a.reshape(t // tm, dilation, tm // dilation, c).transpose(0, 2, 1, 3).reshape(t, c)


def _perm_matrix(rows, dilation):
    per = rows // dilation
    p = np.zeros((rows, rows), np.float32)
    for r in range(dilation):
        for n in range(per):
            p[n * dilation + r, r * per + n] = 1.0
    return jnp.asarray(p, BF16)


def _t5_bucket_np(n):
    max_exact = NUM_BUCKETS // 2
    nf = np.maximum(n, 1).astype(np.float32)
    large = max_exact + (np.log(nf / np.float32(max_exact)) / np.float32(math.log(MAX_DISTANCE / max_exact))
                         * np.float32(NUM_BUCKETS - max_exact)).astype(np.int32)
    large = np.minimum(large, NUM_BUCKETS - 1)
    return np.where(n < max_exact, n, large)


def _mix_tail(mix_in, wo_ref, x_ref, gm_ref, g_ref, sh_ref, sc_ref, wr_ref, br_ref,
              x_out, h_out, lg_out):
    mix = _dot(mix_in, wo_ref[...])
    x_new = x_ref[...] + gm_ref[0] * mix
    x_out[...] = x_new
    h = _norm_mod(x_new, g_ref[...], sh_ref[0], sc_ref[0])
    h_out[...] = h.astype(BF16)
    lg_out[...] = jnp.dot(h, wr_ref[...], preferred_element_type=F32,
                          precision=lax.Precision.HIGHEST) + br_ref[...]


def _merge_oproj_kernel(o0_ref, o1_ref, o2_ref, p1_ref, p2_ref, lse_ref, ex_ref, wo_ref, x_ref, gm_ref, g_ref,
                        sh_ref, sc_ref, wr_ref, br_ref, x_out, h_out, lg_out):
    lse = lse_ref[...]
    l0, l1, l2 = lse[:, 0:16], lse[:, 16:32], lse[:, 32:48]
    m = jnp.maximum(jnp.maximum(l0, l1), l2)
    e0, e1, e2 = jnp.exp(l0 - m), jnp.exp(l1 - m), jnp.exp(l2 - m)
    inv = 1.0 / (e0 + e1 + e2)
    ex = ex_ref[...]
    o1 = _dot(p1_ref[...], o1_ref[...])
    o2 = _dot(p2_ref[...], o2_ref[0].reshape(ROW_TILE, o2_ref.shape[-1]))
    merged = (_dot((e0 * inv).astype(BF16), ex) * o0_ref[...].astype(F32)
              + _dot((e1 * inv).astype(BF16), ex) * o1
              + _dot((e2 * inv).astype(BF16), ex) * o2)
    _mix_tail(merged.astype(BF16), wo_ref, x_ref, gm_ref, g_ref, sh_ref, sc_ref, wr_ref, br_ref,
              x_out, h_out, lg_out)


def _oproj_kernel(o_ref, wo_ref, x_ref, gm_ref, g_ref, sh_ref, sc_ref, wr_ref, br_ref,
                  x_out, h_out, lg_out):
    _mix_tail(o_ref[...], wo_ref, x_ref, gm_ref, g_ref, sh_ref, sc_ref, wr_ref, br_ref,
              x_out, h_out, lg_out)


def _mixer_tail(mix_inputs, w_o, x, gate_m, g_ffn, shift_f, scale_f, w_r, b_r, seq, merged):
    t, d = x.shape
    tpb = seq // ROW_TILE
    row = lambda i: (i, 0)
    full = lambda i: (0, 0)
    per_b = lambda i: (i // tpb, 0, 0)
    tail_specs = [
        pl.BlockSpec((d, d), full),
        pl.BlockSpec((ROW_TILE, d), row),
        pl.BlockSpec((1, 1, d), per_b),
        pl.BlockSpec((1, d), full),
        pl.BlockSpec((1, 1, d), per_b),
        pl.BlockSpec((1, 1, d), per_b),
        pl.BlockSpec((d, N_EXPERTS), full),
        pl.BlockSpec((1, N_EXPERTS), full),
    ]
    tail_args = (w_o, x, gate_m, g_ffn.reshape(1, d), shift_f, scale_f, w_r, b_r.reshape(1, N_EXPERTS))
    if merged:
        o0, o1, o2, lse = mix_inputs
        expand = jnp.asarray(np.kron(np.eye(N_HEADS, dtype=np.float32),
                                     np.ones((1, HEAD_DIM), np.float32)), BF16)
        d1, d2 = DIL_PAIRS[1][1], DIL_PAIRS[2][1]
        assert _perm_tile(d1) == ROW_TILE
        t2 = _perm_tile(d2)
        sub2 = ROW_TILE // d2
        o2v = o2.reshape(t // t2, d2, t2 // d2, d)
        per2 = t2 // ROW_TILE
        kern = _merge_oproj_kernel
        specs = [pl.BlockSpec((ROW_TILE, d), row), pl.BlockSpec((ROW_TILE, d), row),
                 pl.BlockSpec((1, d2, sub2, d), lambda i: (i // per2, 0, i % per2, 0)),
                 pl.BlockSpec((ROW_TILE, ROW_TILE), full), pl.BlockSpec((ROW_TILE, ROW_TILE), full),
                 pl.BlockSpec((ROW_TILE, 3 * N_HEADS), row), pl.BlockSpec((N_HEADS, d), full)]
        args = (o0, o1, o2v, _perm_matrix(ROW_TILE, d1), _perm_matrix(ROW_TILE, d2), lse, expand)
    else:
        kern = _oproj_kernel
        specs = [pl.BlockSpec((ROW_TILE, d), row)]
        args = mix_inputs
    return pl.pallas_call(
        kern,
        out_shape=(jax.ShapeDtypeStruct((t, d), F32), jax.ShapeDtypeStruct((t, d), BF16),
                   jax.ShapeDtypeStruct((t, N_EXPERTS), F32)),
        grid=(t // ROW_TILE,),
        in_specs=specs + tail_specs,
        out_specs=(pl.BlockSpec((ROW_TILE, d), row), pl.BlockSpec((ROW_TILE, d), row),
                   pl.BlockSpec((ROW_TILE, N_EXPERTS), row)),
        compiler_params=_params(("parallel",)),
        name="mixer_tail_merge" if merged else "mixer_tail",
    )(*args, *tail_args)


def _fox_kernel(q_ref, k_ref, v_ref, o_ref, vt_scr):
    qi = pl.program_id(2)
    seq = k_ref.shape[0]
    nk = seq // FOX_TILE
    half = HEAD_DIM

    ext = FOX_SUM_ROWS
    @pl.when(qi == 0)
    def _():
        lane = lax.broadcasted_iota(jnp.int32, (FOX_TILE, 2 * half), 1)
        erow = lax.broadcasted_iota(jnp.int32, (2 * ext, 2 * FOX_TILE), 0)
        ecol = lax.broadcasted_iota(jnp.int32, (2 * ext, 2 * FOX_TILE), 1)
        sum_rows = jnp.where((erow < ext) == (ecol < FOX_TILE), 1.0, 0.0).astype(BF16)
        for j in range(nk):
            vj = v_ref[j * FOX_TILE:(j + 1) * FOX_TILE, :].astype(F32)
            vt_scr[j, :2 * half, :FOX_TILE] = jnp.where(lane < half, vj, 0.0).T.astype(BF16)
            vt_scr[j, :2 * half, FOX_TILE:] = jnp.where(lane >= half, vj, 0.0).T.astype(BF16)
            vt_scr[j, 2 * half:, :] = sum_rows

    q0 = q_ref[:, :AUG]
    q1 = q_ref[:, AUG:]

    def softmax_step(s, m):
        m_new = jnp.maximum(m, jnp.max(s, axis=0, keepdims=True))
        return m_new, jnp.exp2(m - m_new), jnp.exp2((s - m_new).astype(BF16))

    def step(kj, carry, masked):
        m0, m1, acc = carry
        rows = pl.ds(pl.multiple_of(kj * FOX_TILE, FOX_TILE), FOX_TILE)
        s0 = _dot_nt(k_ref[rows, :AUG], q0)
        s1 = _dot_nt(k_ref[rows, AUG:], q1)
        if masked:
            kpos = lax.broadcasted_iota(jnp.int32, (FOX_TILE, FOX_TILE), 0)
            qpos = lax.broadcasted_iota(jnp.int32, (FOX_TILE, FOX_TILE), 1)
            keep = kpos <= qpos
            s0 = jnp.where(keep, s0, NEG)
            s1 = jnp.where(keep, s1, NEG)
        m0, a0, p0 = softmax_step(s0, m0)
        m1, a1, p1 = softmax_step(s1, m1)
        pv = _dot(vt_scr[kj], jnp.concatenate([p0, p1], axis=0))
        alpha = jnp.concatenate([jnp.broadcast_to(a0, (half, FOX_TILE)), jnp.broadcast_to(a1, (half, FOX_TILE)),
                                 jnp.broadcast_to(a0, (ext, FOX_TILE)), jnp.broadcast_to(a1, (ext, FOX_TILE))],
                                axis=0)
        return m0, m1, alpha * acc + pv

    neg = jnp.full((1, FOX_TILE), NEG, F32)
    init = (neg, neg, jnp.zeros((2 * half + 2 * ext, FOX_TILE), F32))
    carry = lax.fori_loop(0, qi, lambda kj, c: step(kj, c, False), init)
    _, _, acc = step(qi, carry, True)
    inv0 = 1.0 / acc[2 * half:2 * half + 1, :]
    inv1 = 1.0 / acc[2 * half + ext:2 * half + ext + 1, :]
    inv = jnp.concatenate([jnp.broadcast_to(inv0, (half, FOX_TILE)),
                           jnp.broadcast_to(inv1, (half, FOX_TILE))], axis=0)
    o_ref[...] = (acc[:2 * half, :] * inv).T.astype(o_ref.dtype)


def _fox(q_aug, k_aug, v, n_batch, seq):
    t = q_aug.shape[0]
    nq = seq // FOX_TILE
    pairs = N_HEADS // 2
    return pl.pallas_call(
        _fox_kernel,
        out_shape=jax.ShapeDtypeStruct((t, N_HEADS * HEAD_DIM), BF16),
        grid=(n_batch, pairs, nq),
        in_specs=[
            pl.BlockSpec((FOX_TILE, 2 * AUG), lambda b, p, qi: (b * nq + qi, p)),
            pl.BlockSpec((seq, 2 * AUG), lambda b, p, qi: (b, p)),
            pl.BlockSpec((seq, 2 * HEAD_DIM), lambda b, p, qi: (b, p)),
        ],
        out_specs=pl.BlockSpec((FOX_TILE, 2 * HEAD_DIM), lambda b, p, qi: (b * nq + qi, p)),
        scratch_shapes=[pltpu.VMEM((nq, 2 * HEAD_DIM + 2 * FOX_SUM_ROWS, 2 * FOX_TILE), BF16)],
        compiler_params=_params(("parallel", "parallel", "arbitrary")),
        name="fox",
    )(q_aug, k_aug, v)


def _expert_kernel(be_ref, nu_ref, x_ref, wgu_ref, bgu_ref, wd_ref, bd_ref, o_ref, wgu_bf, wd_bf):
    i = pl.program_id(0)
    d_ff = wd_ref.shape[1]

    @pl.when(jnp.logical_or(i == 0, be_ref[i] != be_ref[jnp.maximum(i - 1, 0)]))
    def _():
        wgu_bf[...] = wgu_ref[0].astype(BF16)
        wd_bf[...] = wd_ref[0].astype(BF16)

    @pl.when(i < nu_ref[0])
    def _():
        gu = _dot(x_ref[...], wgu_bf[...]) + bgu_ref[0]
        g = jnp.minimum(gu[:, :d_ff], SWIGLU_LIMIT)
        u = jnp.clip(gu[:, d_ff:], -SWIGLU_LIMIT, SWIGLU_LIMIT)
        act = (u + 1.0) * g * (1.0 / (1.0 + jnp.exp(-SWIGLU_ALPHA * g)))
        y = _dot(act.astype(BF16), wd_bf[...]) + bd_ref[0]
        o_ref[...] = y.astype(o_ref.dtype)

    @pl.when(i >= nu_ref[0])
    def _():
        o_ref[...] = jnp.zeros_like(o_ref)


def _experts(xbuf, w_gu, b_gu, w_down, b_down, block_expert, n_used, layer):
    n_rows, d = xbuf.shape
    depth, e, _, n_gu = w_gu.shape
    d_ff = w_down.shape[2]
    n_blocks = n_rows // MOE_TILE
    grid_spec = pltpu.PrefetchScalarGridSpec(
        num_scalar_prefetch=2,
        grid=(n_blocks,),
        in_specs=[
            pl.BlockSpec((MOE_TILE, d), lambda i, be, nu: (i, 0)),
            pl.BlockSpec((None, 1, d, n_gu), lambda i, be, nu: (layer, be[i], 0, 0)),
            pl.BlockSpec((None, 1, 1, n_gu), lambda i, be, nu: (layer, be[i], 0, 0)),
            pl.BlockSpec((None, 1, d_ff, d), lambda i, be, nu: (layer, be[i], 0, 0)),
            pl.BlockSpec((None, 1, 1, d), lambda i, be, nu: (layer, be[i], 0, 0)),
        ],
        out_specs=pl.BlockSpec((MOE_TILE, d), lambda i, be, nu: (i, 0)),
        scratch_shapes=[pltpu.VMEM((d, n_gu), BF16), pltpu.VMEM((d_ff, d), BF16)],
    )
    return pl.pallas_call(
        _expert_kernel,
        out_shape=jax.ShapeDtypeStruct((n_rows, d), BF16),
        grid_spec=grid_spec,
        compiler_params=_params(("arbitrary",)),
        name="experts",
    )(block_expert, n_used, xbuf, w_gu, b_gu.reshape(depth, e, 1, n_gu), w_down, b_down.reshape(depth, e, 1, d))


def _router_kernel(lg_ref, tri_ref, upper_ref, dest_ref, gate_ref, cnt_ref, carry, base):
    phase = pl.program_id(0)
    i = pl.program_id(1)
    tm = lg_ref.shape[0]
    lane = lax.broadcasted_iota(jnp.int32, (tm, N_EXPERTS), 1).astype(F32)
    wide = lax.broadcasted_iota(jnp.int32, (tm, 128), 1)

    work = lg_ref[...]
    hots, vals = [], []
    for _ in range(TOP_K):
        m = jnp.max(work, axis=1, keepdims=True)
        idx = jnp.min(jnp.where(work == m, lane, float(N_EXPERTS)), axis=1, keepdims=True)
        hot = lane == idx
        hots.append(hot)
        vals.append(m)
        work = jnp.where(hot, -jnp.inf, work)
    exps = [jnp.exp(v - vals[0]) for v in vals]
    inv = 1.0 / (exps[0] + exps[1] + exps[2] + exps[3])
    gate_tile = jnp.zeros((tm, 128), F32)
    for k in range(TOP_K):
        gate_tile = jnp.where(wide == k, exps[k] * inv, gate_tile)
    gate_ref[...] = gate_tile

    chosen = jnp.zeros((tm, N_EXPERTS), F32)
    for hot in hots:
        chosen = chosen + hot.astype(F32)

    @pl.when(jnp.logical_and(phase == 0, i == 0))
    def _():
        carry[...] = jnp.zeros_like(carry)
        base[...] = jnp.zeros_like(base)

    @pl.when(jnp.logical_and(phase == 1, i == 0))
    def _():
        counts = carry[...]
        padded = jnp.floor((counts + (MOE_TILE - 1)) * (1.0 / MOE_TILE)) * MOE_TILE
        base[...] = jnp.dot(jnp.broadcast_to(padded, (8, N_EXPERTS)), upper_ref[...],
                            preferred_element_type=F32, precision=lax.Precision.HIGHEST)[0:1, :]
        carry[...] = jnp.zeros_like(carry)

    ahead = _dot(tri_ref[...], chosen.astype(BF16))
    pos = ahead + carry[...] + base[...]
    dest_tile = jnp.zeros((tm, 128), F32)
    for k in range(TOP_K):
        row = jnp.sum(jnp.where(hots[k], pos, 0.0), axis=1, keepdims=True)
        dest_tile = jnp.where(wide == k, row, dest_tile)
    dest_ref[...] = dest_tile.astype(jnp.int32)
    carry[...] = carry[...] + jnp.sum(chosen, axis=0, keepdims=True)
    cnt_ref[...] = carry[...]


def _router(logits):
    t = logits.shape[0]
    nt = t // ROW_TILE
    tri = jnp.asarray(np.tril(np.ones((ROW_TILE, ROW_TILE), np.float32), -1), BF16)
    upper = jnp.asarray(np.triu(np.ones((N_EXPERTS, N_EXPERTS), np.float32), 1), F32)
    return pl.pallas_call(
        _router_kernel,
        out_shape=(jax.ShapeDtypeStruct((t + ROW_TILE, 128), jnp.int32),
                   jax.ShapeDtypeStruct((t + ROW_TILE, 128), F32),
                   jax.ShapeDtypeStruct((1, N_EXPERTS), F32)),
        grid=(2, nt),
        in_specs=[
            pl.BlockSpec((ROW_TILE, N_EXPERTS), lambda p, i: (i, 0)),
            pl.BlockSpec((ROW_TILE, ROW_TILE), lambda p, i: (0, 0)),
            pl.BlockSpec((N_EXPERTS, N_EXPERTS), lambda p, i: (0, 0)),
        ],
        out_specs=(pl.BlockSpec((ROW_TILE, 128), lambda p, i: (p * i + (1 - p) * nt, 0)),
                   pl.BlockSpec((ROW_TILE, 128), lambda p, i: (p * i + (1 - p) * nt, 0)),
                   pl.BlockSpec((1, N_EXPERTS), lambda p, i: (0, 0))),
        scratch_shapes=[pltpu.VMEM((1, N_EXPERTS), F32), pltpu.VMEM((1, N_EXPERTS), F32)],
        compiler_params=_params(("arbitrary", "arbitrary")),
        name="router",
    )(logits, tri, upper)


def _route(logits):
    t = logits.shape[0]
    n_slots = t * TOP_K
    n_rows = n_slots + N_EXPERTS * MOE_TILE
    n_blocks = n_rows // MOE_TILE
    dest_w, gates_w, counts = _router(logits)
    counts = counts.reshape(N_EXPERTS).astype(jnp.int32)
    pad_ends = jnp.cumsum((counts + MOE_TILE - 1) // MOE_TILE * MOE_TILE)
    block_start = jnp.arange(n_blocks, dtype=jnp.int32) * MOE_TILE
    block_expert = jnp.minimum(jnp.sum(block_start[:, None] >= pad_ends[None, :], axis=1),
                               N_EXPERTS - 1).astype(jnp.int32)
    n_used = (pad_ends[-1] // MOE_TILE).astype(jnp.int32).reshape(1)
    dest = dest_w[:t, :TOP_K].T.reshape(-1)
    tok = jnp.tile(jnp.arange(t, dtype=jnp.int32), TOP_K)
    tok_rows = jnp.zeros((n_rows,), jnp.int32).at[dest].set(tok, unique_indices=True)
    return dest, tok_rows, gates_w, block_expert, n_used


def _combine_kernel(x_ref, y0_ref, y1_ref, y2_ref, y3_ref, gt_ref, gf_ref, g_ref, o_ref, *, final):
    gt = gt_ref[...]
    moe = (gt[:, 0:1] * y0_ref[...].astype(F32) + gt[:, 1:2] * y1_ref[...].astype(F32)
           + gt[:, 2:3] * y2_ref[...].astype(F32) + gt[:, 3:4] * y3_ref[...].astype(F32))
    x_new = x_ref[...] + gf_ref[0] * moe
    if final:
        x_new = _norm_mod(x_new, g_ref[...], None, None)
    o_ref[...] = x_new


def _combine(x, y_slots, gates, gate_f, g_final, seq, final):
    t, d = x.shape
    tpb = seq // ROW_TILE
    nt = t // ROW_TILE
    assert TOP_K == 4
    return pl.pallas_call(
        functools.partial(_combine_kernel, final=final),
        out_shape=jax.ShapeDtypeStruct((t, d), F32),
        grid=(nt,),
        in_specs=[pl.BlockSpec((ROW_TILE, d), lambda i: (i, 0))] + [
            pl.BlockSpec((ROW_TILE, d), functools.partial(lambda i, k: (k * nt + i, 0), k=k))
            for k in range(TOP_K)] + [
            pl.BlockSpec((ROW_TILE, 128), lambda i: (i, 0)),
            pl.BlockSpec((1, 1, d), lambda i: (i // tpb, 0, 0)),
            pl.BlockSpec((1, d), lambda i: (0, 0)),
        ],
        out_specs=pl.BlockSpec((ROW_TILE, d), lambda i: (i, 0)),
        compiler_params=_params(("parallel",)),
        name="combine_final" if final else "combine",
    )(x, y_slots, y_slots, y_slots, y_slots, gates, gate_f, g_final.reshape(1, d))


def _moe(x, h, logits, gate_f, w_gu, b_gu, w_down, b_down, g_final, seq, layer, final):
    t, d = x.shape
    dest, tok_rows, gates, block_expert, n_used = _route(logits)
    xbuf = jnp.take(h, tok_rows, axis=0, mode="clip")
    ybuf = _experts(xbuf, w_gu, b_gu, w_down, b_down, block_expert, n_used, layer)
    y_slots = jnp.take(ybuf, dest, axis=0, mode="clip")
    return _combine(x, y_slots, gates, gate_f, g_final, seq, final)


def kernel(x, c, ada_w, ada_b, norm_mix_g, norm_ffn_g, a_w_qkv, a_w_o, rel_bias, kv_norm_g, w_kvf, b_f,
           b_w_q, b_w_o, router_w, router_b, w_gu, b_gu, w_down, b_down, final_norm_g):
    n_batch, seq, d = x.shape
    t = n_batch * seq
    width = N_HEADS * HEAD_DIM
    xf = x.reshape(t, d)

    ada = _ada(c, ada_w, ada_b)
    mods = [[ada[l, :, i * d:(i + 1) * d].reshape(n_batch, 1, d) for i in range(6)] for l in range(2)]

    shift_m, scale_m, gate_m, shift_f, scale_f, gate_f = mods[0]
    qscale = np.ones((3, 3, 1), np.float32)
    qscale[:, 0] = HEAD_DIM ** -0.5
    w_qkv = (a_w_qkv[0].reshape(d, 3, 3, width) * qscale).astype(BF16)

    qi = np.arange(DIL_BLOCK, dtype=np.int32)[:, None]
    kj = np.arange(2 * DIL_BLOCK, dtype=np.int32)[None, :]
    delta = qi + DIL_BLOCK - kj
    in_band = (delta >= 0) & (delta <= DIL_BLOCK)
    outs, lses = [], []
    for g, (window, dilation) in enumerate(DIL_PAIRS):
        assert window // dilation == DIL_BLOCK
        qkv = _proj(xf, norm_mix_g[0], shift_m, scale_m, w_qkv[:, g].reshape(d, 3 * width), seq, dilation, g)
        bucket = _t5_bucket_np(np.clip(delta, 0, None) * dilation)
        tab = rel_bias[:, g * N_HEADS:(g + 1) * N_HEADS].astype(F32)
        bias = jnp.where(in_band, jnp.transpose(tab[bucket], (2, 0, 1)), NEG)
        bias_first = jnp.where(kj >= DIL_BLOCK, bias, NEG)
        o, lse = _dilated_group(qkv, jnp.stack([bias_first, bias]), g, dilation, n_batch, seq)
        outs.append(o)
        lses.append(_unpermute_rows(lse, dilation)[:, :N_HEADS])
    lse_all = jnp.concatenate(lses, axis=1)
    x1, h1, logits1 = _mixer_tail((outs[0], outs[1], outs[2], lse_all), a_w_o[0].astype(BF16), xf, gate_m,
                                  norm_ffn_g[0], shift_f, scale_f, router_w[0], router_b[0], seq, True)
    x2 = _moe(x1, h1, logits1, gate_f, w_gu, b_gu, w_down, b_down, final_norm_g, seq, 0, False)

    w_f = jnp.pad(jnp.tile(w_kvf[:, 2 * width:], (1, 3)), ((0, 0), (0, 128 - 3 * N_HEADS))).astype(BF16)
    b_fp = jnp.pad(jnp.tile(b_f, 3), (0, 128 - 3 * N_HEADS)).reshape(1, 128)
    k_aug, v_sh, fcum = _kvf(x2, kv_norm_g, _aug_weight(w_kvf[:, :width]), w_kvf[:, width:2 * width].astype(BF16),
                             w_f, b_fp, n_batch)

    shift_m, scale_m, gate_m, shift_f, scale_f, gate_f = mods[1]
    q_aug = _qaug(x2, norm_mix_g[1], shift_m, scale_m, _aug_weight(b_w_q[0] * (HEAD_DIM ** -0.5 * LOG2E)), fcum, seq)
    o1 = _fox(q_aug, k_aug, v_sh, n_batch, seq)

    x3, h3, logits3 = _mixer_tail((o1,), b_w_o[0].astype(BF16), x2, gate_m, norm_ffn_g[1], shift_f, scale_f,
                                  router_w[1], router_b[1], seq, False)
    out = _moe(x3, h3, logits3, gate_f, w_gu, b_gu, w_down, b_down, final_norm_g, seq, 1, True)
    return out.reshape(n_batch, seq, d)
```

```python
import functools
import math

import numpy as np
import jax
import jax.numpy as jnp
from jax import lax
from jax.experimental import pallas as pl
from jax.experimental.pallas import tpu as pltpu

F32 = jnp.float32
BF16 = jnp.bfloat16

D_MODEL = 1024
HEAD_DIM = 64
N_HEADS = 16
DIL_PAIRS = ((128, 1), (512, 4), (2048, 16))
DIL_BLOCK = 128
NUM_BUCKETS = 32
MAX_DISTANCE = 2048
N_EXPERTS = 32
TOP_K = 4
SWIGLU_LIMIT = 7.0
SWIGLU_ALPHA = 1.702
RMS_EPS = 1e-6
NEG = -1e30
LOG2E = math.log2(math.e)

ROW_TILE = 512
MOE_TILE = 512
FOX_TILE = 512
AUG = 128
FOX_SUM_ROWS = 8
VMEM_LIMIT = 56 * 1024 * 1024


def _params(sem, vmem=VMEM_LIMIT):
    return pltpu.CompilerParams(dimension_semantics=sem, vmem_limit_bytes=vmem)


def _dot(a, b):
    return jnp.dot(a, b, preferred_element_type=F32)


def _dot_nt(a, b):
    return lax.dot_general(a, b, (((1,), (1,)), ((), ())), preferred_element_type=F32)


def _dot_tn(a, b):
    return lax.dot_general(a, b, (((0,), (0,)), ((), ())), preferred_element_type=F32)


def _norm_mod(x, g, shift, scale):
    ms = jnp.mean(x * x, axis=-1, keepdims=True)
    y = x * lax.rsqrt(ms + RMS_EPS) * g
    if scale is not None:
        y = y * (1.0 + scale) + shift
    return y


def _ada_kernel(c_ref, w_ref, b_ref, o_ref):
    c = c_ref[...]
    act = c * (1.0 / (1.0 + jnp.exp(-c)))
    o_ref[0] = jnp.dot(act, w_ref[0], preferred_element_type=F32,
                       precision=lax.Precision.HIGHEST) + b_ref[0]


def _ada(c, ada_w, ada_b):
    depth, d, n = ada_w.shape
    bsz = c.shape[0]
    tn = 1536
    return pl.pallas_call(
        _ada_kernel,
        out_shape=jax.ShapeDtypeStruct((depth, bsz, n), F32),
        grid=(depth, n // tn),
        in_specs=[
            pl.BlockSpec((bsz, d), lambda l, j: (0, 0)),
            pl.BlockSpec((1, d, tn), lambda l, j: (l, 0, j)),
            pl.BlockSpec((1, 1, tn), lambda l, j: (l, 0, j)),
        ],
        out_specs=pl.BlockSpec((1, bsz, tn), lambda l, j: (l, 0, j)),
        compiler_params=_params(("parallel", "parallel")),
        name="ada",
    )(c, ada_w, ada_b.reshape(depth, 1, n))


def _perm_tile(dilation):
    return max(ROW_TILE, DIL_BLOCK * dilation)


def _proj_kernel(x_ref, g_ref, sh_ref, sc_ref, w_ref, o_ref, h_scr, *xs_scr, dilation):
    @pl.when(pl.program_id(1) == 0)
    def _():
        if dilation == 1:
            h_scr[...] = _norm_mod(x_ref[...], g_ref[...], sh_ref[0], sc_ref[0]).astype(BF16)
        else:
            (xs,) = xs_scr
            n_lane = xs.shape[0]
            for c in range(n_lane):
                xs[c] = x_ref[:, c * 128:(c + 1) * 128]
            chunk = x_ref.shape[0] // dilation
            for r in range(dilation):
                xr = jnp.concatenate([xs[c, pl.ds(r, chunk, stride=dilation), :] for c in range(n_lane)], axis=1)
                h_scr[r * chunk:(r + 1) * chunk, :] = _norm_mod(xr, g_ref[...], sh_ref[0], sc_ref[0]).astype(BF16)

    o_ref[...] = _dot(h_scr[...], w_ref[...]).astype(o_ref.dtype)


def _proj(x, g, shift, scale, w, seq, dilation, group):
    t, d = x.shape
    n = w.shape[1]
    tm = _perm_tile(dilation)
    tn = d
    tpb = seq // tm
    return pl.pallas_call(
        functools.partial(_proj_kernel, dilation=dilation),
        out_shape=jax.ShapeDtypeStruct((t, n), BF16),
        grid=(t // tm, n // tn),
        in_specs=[
            pl.BlockSpec((tm, d), lambda i, j: (i, 0)),
            pl.BlockSpec((1, d), lambda i, j: (0, 0)),
            pl.BlockSpec((1, 1, d), lambda i, j: (i // tpb, 0, 0)),
            pl.BlockSpec((1, 1, d), lambda i, j: (i // tpb, 0, 0)),
            pl.BlockSpec((d, tn), lambda i, j: (0, j)),
        ],
        out_specs=pl.BlockSpec((tm, tn), lambda i, j: (i, j)),
        scratch_shapes=[pltpu.VMEM((tm, d), BF16)] + (
            [pltpu.VMEM((d // 128, tm, 128), F32)] if dilation > 1 else []),
        compiler_params=_params(("parallel", "arbitrary")),
        name=f"proj{group}",
    )(x, g.reshape(1, d), shift, scale, w)


def _split3(f):
    def top(v):
        return lax.bitcast_convert_type(lax.bitcast_convert_type(v, jnp.uint32) & jnp.uint32(0xFFFF0000), F32)
    hi = top(f)
    r1 = f - hi
    mid = top(r1)
    return hi, mid, r1 - mid


def _forget_pieces(fc):
    hi, mid, lo = _split3(fc)
    lane = lax.broadcasted_iota(jnp.int32, fc.shape, 1)
    x = jnp.where(lane < N_HEADS, hi, jnp.where(lane < 2 * N_HEADS, mid, jnp.where(lane < 3 * N_HEADS, lo, 0.0)))
    return x.astype(BF16)


def _aug_tables(key_side):
    place = np.zeros((128, N_HEADS * AUG), np.float32)
    const = np.zeros((1, N_HEADS * AUG), np.float32)
    for h in range(N_HEADS):
        base = h * AUG + HEAD_DIM
        for piece in range(3):
            if key_side:
                place[piece * N_HEADS + h, base + piece] = 1.0
                const[0, base + 3 + piece] = 1.0
            else:
                place[piece * N_HEADS + h, base + 3 + piece] = 1.0
                const[0, base + piece] = -1.0
    return jnp.asarray(place, BF16), jnp.asarray(const, F32)


def _aug_weight(w):
    d = w.shape[0]
    w = jnp.pad(w.reshape(d, N_HEADS, HEAD_DIM), ((0, 0), (0, 0), (0, AUG - HEAD_DIM)))
    return w.reshape(d, N_HEADS * AUG).astype(BF16)


def _kvf_kernel(x_ref, g_ref, wk_ref, wv_ref, wf_ref, bf_ref, tri_ref, pl_ref, cr_ref,
                k_ref, v_ref, f_ref, carry, *, tpb):
    h = _norm_mod(x_ref[...], g_ref[...], None, None).astype(BF16)
    v_ref[...] = _dot(h, wv_ref[...]).astype(v_ref.dtype)
    z = _dot(h, wf_ref[...]) + bf_ref[...]
    lf = jnp.minimum(z, 0.0) - jnp.log(1.0 + jnp.exp(-jnp.abs(z)))
    hi, mid, lo = _split3(lf)
    tri = tri_ref[...]
    cs = _dot(tri, hi.astype(BF16)) + _dot(tri, mid.astype(BF16)) + _dot(tri, lo.astype(BF16))

    @pl.when(pl.program_id(0) % tpb == 0)
    def _():
        carry[...] = jnp.zeros_like(carry)

    cs = cs + carry[...]
    carry[...] = cs[ROW_TILE - 1:ROW_TILE, :]
    f2 = cs * LOG2E
    f_ref[...] = f2
    k_ref[...] = (_dot(h, wk_ref[...]) + _dot(_forget_pieces(f2), pl_ref[...]) + cr_ref[...]).astype(k_ref.dtype)


def _kvf(x, g, w_k, w_v, w_f, b_f, n_batch):
    t, d = x.shape
    n_aug = N_HEADS * AUG
    tri = jnp.asarray(np.tril(np.ones((ROW_TILE, ROW_TILE), np.float32)), BF16)
    place, const = _aug_tables(True)
    assert (t // ROW_TILE) % n_batch == 0
    full = lambda i: (0, 0)
    row = lambda i: (i, 0)
    return pl.pallas_call(
        functools.partial(_kvf_kernel, tpb=t // ROW_TILE // n_batch),
        out_shape=(jax.ShapeDtypeStruct((t, n_aug), BF16), jax.ShapeDtypeStruct((t, d), BF16),
                   jax.ShapeDtypeStruct((t, 128), F32)),
        grid=(t // ROW_TILE,),
        in_specs=[
            pl.BlockSpec((ROW_TILE, d), row),
            pl.BlockSpec((1, d), full),
            pl.BlockSpec((d, n_aug), full),
            pl.BlockSpec((d, d), full),
            pl.BlockSpec((d, 128), full),
            pl.BlockSpec((1, 128), full),
            pl.BlockSpec((ROW_TILE, ROW_TILE), full),
            pl.BlockSpec((128, n_aug), full),
            pl.BlockSpec((1, n_aug), full),
        ],
        out_specs=(pl.BlockSpec((ROW_TILE, n_aug), row), pl.BlockSpec((ROW_TILE, d), row),
                   pl.BlockSpec((ROW_TILE, 128), row)),
        scratch_shapes=[pltpu.VMEM((1, 128), F32)],
        compiler_params=_params(("arbitrary",)),
        name="kvf",
    )(x, g.reshape(1, d), w_k, w_v, w_f, b_f, tri, place, const)


def _qaug_kernel(x_ref, g_ref, sh_ref, sc_ref, w_ref, f_ref, pl_ref, cr_ref, o_ref):
    h = _norm_mod(x_ref[...], g_ref[...], sh_ref[0], sc_ref[0]).astype(BF16)
    o_ref[...] = (_dot(h, w_ref[...]) + _dot(_forget_pieces(f_ref[...]), pl_ref[...])
                  + cr_ref[...]).astype(o_ref.dtype)


def _qaug(x, g, shift, scale, w, fcum, seq):
    t, d = x.shape
    n_aug = N_HEADS * AUG
    tpb = seq // ROW_TILE
    place, const = _aug_tables(False)
    full = lambda i: (0, 0)
    row = lambda i: (i, 0)
    per_b = lambda i: (i // tpb, 0, 0)
    return pl.pallas_call(
        _qaug_kernel,
        out_shape=jax.ShapeDtypeStruct((t, n_aug), BF16),
        grid=(t // ROW_TILE,),
        in_specs=[
            pl.BlockSpec((ROW_TILE, d), row),
            pl.BlockSpec((1, d), full),
            pl.BlockSpec((1, 1, d), per_b),
            pl.BlockSpec((1, 1, d), per_b),
            pl.BlockSpec((d, n_aug), full),
            pl.BlockSpec((ROW_TILE, 128), row),
            pl.BlockSpec((128, n_aug), full),
            pl.BlockSpec((1, n_aug), full),
        ],
        out_specs=pl.BlockSpec((ROW_TILE, n_aug), row),
        compiler_params=_params(("parallel",)),
        name="qaug",
    )(x, g.reshape(1, d), shift, scale, w, fcum, place, const)


def _dil_kernel(q_ref, kp_ref, kc_ref, vp_ref, vc_ref, bias_ref, o_ref, lse_ref):
    table = jnp.minimum(pl.program_id(2), 1)
    pair_w = 2 * HEAD_DIM
    lane = lax.broadcasted_iota(jnp.int32, (DIL_BLOCK, pair_w), 1)
    first = lane < HEAD_DIM
    first2 = lax.broadcasted_iota(jnp.int32, (2 * DIL_BLOCK, pair_w), 1) < HEAD_DIM
    lse_tile = jnp.zeros((DIL_BLOCK, pair_w), F32)
    zero = jnp.zeros((), BF16)
    for pair in range(N_HEADS // 2):
        sl = slice(pair * pair_w, (pair + 1) * pair_w)
        q = q_ref[:, sl]
        k2 = jnp.concatenate([kp_ref[:, sl], kc_ref[:, sl]], axis=0)
        v2 = jnp.concatenate([vp_ref[:, sl], vc_ref[:, sl]], axis=0)
        probs, invs = [], []
        for which in range(2):
            h = 2 * pair + which
            qh = jnp.where(first, q, zero) if which == 0 else jnp.where(first, zero, q)
            s = _dot_nt(qh, k2) + bias_ref[table, h]
            m = jnp.max(s, axis=-1, keepdims=True)
            p = jnp.exp(s - m)
            den = jnp.sum(p, axis=-1, keepdims=True)
            probs.append(p.astype(BF16))
            invs.append(1.0 / den)
            lse_tile = jnp.where(lane == h, m + jnp.log(den), lse_tile)
        vcat = jnp.concatenate([jnp.where(first2, v2, zero), jnp.where(first2, zero, v2)], axis=0)
        o = _dot(jnp.concatenate(probs, axis=1), vcat)
        o_ref[:, sl] = (o * jnp.where(first, invs[0], invs[1])).astype(o_ref.dtype)
    lse_ref[...] = lse_tile


def _dilated_group(qkv, bias, group, dilation, n_batch, seq):
    t = qkv.shape[0]
    width = N_HEADS * HEAD_DIM
    nb = seq // dilation // DIL_BLOCK
    bpb = seq // DIL_BLOCK

    def blk(b, r, n):
        return b * bpb + n * dilation + r

    def spec(part, prev):
        if prev:
            return pl.BlockSpec((DIL_BLOCK, width), lambda b, r, n: (blk(b, r, jnp.maximum(n - 1, 0)), part))
        return pl.BlockSpec((DIL_BLOCK, width), lambda b, r, n: (blk(b, r, n), part))

    return pl.pallas_call(
        _dil_kernel,
        out_shape=(jax.ShapeDtypeStruct((t, width), BF16), jax.ShapeDtypeStruct((t, 2 * HEAD_DIM), F32)),
        grid=(n_batch, dilation, nb),
        in_specs=[spec(0, False), spec(1, True), spec(1, False), spec(2, True), spec(2, False),
                  pl.BlockSpec((2, N_HEADS, DIL_BLOCK, 2 * DIL_BLOCK), lambda b, r, n: (0, 0, 0, 0))],
        out_specs=(pl.BlockSpec((DIL_BLOCK, width), lambda b, r, n: (blk(b, r, n), 0)),
                   pl.BlockSpec((DIL_BLOCK, 2 * HEAD_DIM), lambda b, r, n: (blk(b, r, n), 0))),
        compiler_params=_params(("parallel", "parallel", "arbitrary")),
        name=f"dilated{group}",
    )(qkv, qkv, qkv, qkv, qkv, bias)


def _unpermute_rows(a, dilation):
    if dilation == 1:
        return a
    t, c = a.shape
    tm = _perm_tile(dilation)
    return a.reshape(t // tm, dilation, tm // dilation, c).transpose(0, 2, 1, 3).reshape(t, c)


def _perm_matrix(rows, dilation):
    per = rows // dilation
    p = np.zeros((rows, rows), np.float32)
    for r in range(dilation):
        for n in range(per):
            p[n * dilation + r, r * per + n] = 1.0
    return jnp.asarray(p, BF16)


def _t5_bucket_np(n):
    max_exact = NUM_BUCKETS // 2
    nf = np.maximum(n, 1).astype(np.float32)
    large = max_exact + (np.log(nf / np.float32(max_exact)) / np.float32(math.log(MAX_DISTANCE / max_exact))
                         * np.float32(NUM_BUCKETS - max_exact)).astype(np.int32)
    large = np.minimum(large, NUM_BUCKETS - 1)
    return np.where(n < max_exact, n, large)


def _mix_tail(mix_in, wo_ref, x_ref, gm_ref, g_ref, sh_ref, sc_ref, wr_ref, br_ref,
              x_out, h_out, lg_out):
    mix = _dot(mix_in, wo_ref[...])
    x_new = x_ref[...] + gm_ref[0] * mix
    x_out[...] = x_new
    h = _norm_mod(x_new, g_ref[...], sh_ref[0], sc_ref[0])
    h_out[...] = h
    lg_out[...] = jnp.dot(h, wr_ref[...], preferred_element_type=F32,
                          precision=lax.Precision.HIGHEST) + br_ref[...]


def _merge_oproj_kernel(o0_ref, o1_ref, o2_ref, p1_ref, p2_ref, lse_ref, ex_ref, wo_ref, x_ref, gm_ref, g_ref,
                        sh_ref, sc_ref, wr_ref, br_ref, x_out, h_out, lg_out):
    lse = lse_ref[...]
    l0, l1, l2 = lse[:, 0:16], lse[:, 16:32], lse[:, 32:48]
    m = jnp.maximum(jnp.maximum(l0, l1), l2)
    e0, e1, e2 = jnp.exp(l0 - m), jnp.exp(l1 - m), jnp.exp(l2 - m)
    inv = 1.0 / (e0 + e1 + e2)
    ex = ex_ref[...]
    o1 = _dot(p1_ref[...], o1_ref[...])
    o2 = _dot(p2_ref[...], o2_ref[0].reshape(ROW_TILE, o2_ref.shape[-1]))
    merged = (_dot((e0 * inv).astype(BF16), ex) * o0_ref[...].astype(F32)
              + _dot((e1 * inv).astype(BF16), ex) * o1
              + _dot((e2 * inv).astype(BF16), ex) * o2)
    _mix_tail(merged.astype(BF16), wo_ref, x_ref, gm_ref, g_ref, sh_ref, sc_ref, wr_ref, br_ref,
              x_out, h_out, lg_out)


def _oproj_kernel(o_ref, wo_ref, x_ref, gm_ref, g_ref, sh_ref, sc_ref, wr_ref, br_ref,
                  x_out, h_out, lg_out):
    _mix_tail(o_ref[...], wo_ref, x_ref, gm_ref, g_ref, sh_ref, sc_ref, wr_ref, br_ref,
              x_out, h_out, lg_out)


def _mixer_tail(mix_inputs, w_o, x, gate_m, g_ffn, shift_f, scale_f, w_r, b_r, seq, merged):
    t, d = x.shape
    tpb = seq // ROW_TILE
    row = lambda i: (i, 0)
    full = lambda i: (0, 0)
    per_b = lambda i: (i // tpb, 0, 0)
    tail_specs = [
        pl.BlockSpec((d, d), full),
        pl.BlockSpec((ROW_TILE, d), row),
        pl.BlockSpec((1, 1, d), per_b),
        pl.BlockSpec((1, d), full),
        pl.BlockSpec((1, 1, d), per_b),
        pl.BlockSpec((1, 1, d), per_b),
        pl.BlockSpec((d, N_EXPERTS), full),
        pl.BlockSpec((1, N_EXPERTS), full),
    ]
    tail_args = (w_o, x, gate_m, g_ffn.reshape(1, d), shift_f, scale_f, w_r, b_r.reshape(1, N_EXPERTS))
    if merged:
        o0, o1, o2, lse = mix_inputs
        expand = jnp.asarray(np.kron(np.eye(N_HEADS, dtype=np.float32),
                                     np.ones((1, HEAD_DIM), np.float32)), BF16)
        d1, d2 = DIL_PAIRS[1][1], DIL_PAIRS[2][1]
        assert _perm_tile(d1) == ROW_TILE
        t2 = _perm_tile(d2)
        sub2 = ROW_TILE // d2
        o2v = o2.reshape(t // t2, d2, t2 // d2, d)
        per2 = t2 // ROW_TILE
        kern = _merge_oproj_kernel
        specs = [pl.BlockSpec((ROW_TILE, d), row), pl.BlockSpec((ROW_TILE, d), row),
                 pl.BlockSpec((1, d2, sub2, d), lambda i: (i // per2, 0, i % per2, 0)),
                 pl.BlockSpec((ROW_TILE, ROW_TILE), full), pl.BlockSpec((ROW_TILE, ROW_TILE), full),
                 pl.BlockSpec((ROW_TILE, 3 * N_HEADS), row), pl.BlockSpec((N_HEADS, d), full)]
        args = (o0, o1, o2v, _perm_matrix(ROW_TILE, d1), _perm_matrix(ROW_TILE, d2), lse, expand)
    else:
        kern = _oproj_kernel
        specs = [pl.BlockSpec((ROW_TILE, d), row)]
        args = mix_inputs
    return pl.pallas_call(
        kern,
        out_shape=(jax.ShapeDtypeStruct((t, d), F32), jax.ShapeDtypeStruct((t, d), F32),
                   jax.ShapeDtypeStruct((t, N_EXPERTS), F32)),
        grid=(t // ROW_TILE,),
        in_specs=specs + tail_specs,
        out_specs=(pl.BlockSpec((ROW_TILE, d), row), pl.BlockSpec((ROW_TILE, d), row),
                   pl.BlockSpec((ROW_TILE, N_EXPERTS), row)),
        compiler_params=_params(("parallel",)),
        name="mixer_tail_merge" if merged else "mixer_tail",
    )(*args, *tail_args)


def _fox_kernel(q_ref, k_ref, v_ref, o_ref, vt_scr):
    qi = pl.program_id(2)
    seq = k_ref.shape[0]
    nk = seq // FOX_TILE
    half = HEAD_DIM

    ext = FOX_SUM_ROWS
    @pl.when(qi == 0)
    def _():
        lane = lax.broadcasted_iota(jnp.int32, (FOX_TILE, 2 * half), 1)
        erow = lax.broadcasted_iota(jnp.int32, (2 * ext, 2 * FOX_TILE), 0)
        ecol = lax.broadcasted_iota(jnp.int32, (2 * ext, 2 * FOX_TILE), 1)
        sum_rows = jnp.where((erow < ext) == (ecol < FOX_TILE), 1.0, 0.0).astype(BF16)
        for j in range(nk):
            vj = v_ref[j * FOX_TILE:(j + 1) * FOX_TILE, :].astype(F32)
            vt_scr[j, :2 * half, :FOX_TILE] = jnp.where(lane < half, vj, 0.0).T.astype(BF16)
            vt_scr[j, :2 * half, FOX_TILE:] = jnp.where(lane >= half, vj, 0.0).T.astype(BF16)
            vt_scr[j, 2 * half:, :] = sum_rows

    q0 = q_ref[:, :AUG]
    q1 = q_ref[:, AUG:]

    def softmax_step(s, m):
        m_new = jnp.maximum(m, jnp.max(s, axis=0, keepdims=True))
        return m_new, jnp.exp2(m - m_new), jnp.exp2((s - m_new).astype(BF16))

    def step(kj, carry, masked):
        m0, m1, acc = carry
        rows = pl.ds(pl.multiple_of(kj * FOX_TILE, FOX_TILE), FOX_TILE)
        s0 = _dot_nt(k_ref[rows, :AUG], q0)
        s1 = _dot_nt(k_ref[rows, AUG:], q1)
        if masked:
            kpos = lax.broadcasted_iota(jnp.int32, (FOX_TILE, FOX_TILE), 0)
            qpos = lax.broadcasted_iota(jnp.int32, (FOX_TILE, FOX_TILE), 1)
            keep = kpos <= qpos
            s0 = jnp.where(keep, s0, NEG)
            s1 = jnp.where(keep, s1, NEG)
        m0, a0, p0 = softmax_step(s0, m0)
        m1, a1, p1 = softmax_step(s1, m1)
        pv = _dot(vt_scr[kj], jnp.concatenate([p0, p1], axis=0))
        alpha = jnp.concatenate([jnp.broadcast_to(a0, (half, FOX_TILE)), jnp.broadcast_to(a1, (half, FOX_TILE)),
                                 jnp.broadcast_to(a0, (ext, FOX_TILE)), jnp.broadcast_to(a1, (ext, FOX_TILE))],
                                axis=0)
        return m0, m1, alpha * acc + pv

    neg = jnp.full((1, FOX_TILE), NEG, F32)
    init = (neg, neg, jnp.zeros((2 * half + 2 * ext, FOX_TILE), F32))
    carry = lax.fori_loop(0, qi, lambda kj, c: step(kj, c, False), init)
    _, _, acc = step(qi, carry, True)
    inv0 = 1.0 / acc[2 * half:2 * half + 1, :]
    inv1 = 1.0 / acc[2 * half + ext:2 * half + ext + 1, :]
    inv = jnp.concatenate([jnp.broadcast_to(inv0, (half, FOX_TILE)),
                           jnp.broadcast_to(inv1, (half, FOX_TILE))], axis=0)
    o_ref[...] = (acc[:2 * half, :] * inv).T.astype(o_ref.dtype)


def _fox(q_aug, k_aug, v, n_batch, seq):
    t = q_aug.shape[0]
    nq = seq // FOX_TILE
    pairs = N_HEADS // 2
    return pl.pallas_call(
        _fox_kernel,
        out_shape=jax.ShapeDtypeStruct((t, N_HEADS * HEAD_DIM), BF16),
        grid=(n_batch, pairs, nq),
        in_specs=[
            pl.BlockSpec((FOX_TILE, 2 * AUG), lambda b, p, qi: (b * nq + qi, p)),
            pl.BlockSpec((seq, 2 * AUG), lambda b, p, qi: (b, p)),
            pl.BlockSpec((seq, 2 * HEAD_DIM), lambda b, p, qi: (b, p)),
        ],
        out_specs=pl.BlockSpec((FOX_TILE, 2 * HEAD_DIM), lambda b, p, qi: (b * nq + qi, p)),
        scratch_shapes=[pltpu.VMEM((nq, 2 * HEAD_DIM + 2 * FOX_SUM_ROWS, 2 * FOX_TILE), BF16)],
        compiler_params=_params(("parallel", "parallel", "arbitrary")),
        name="fox",
    )(q_aug, k_aug, v)


def _expert_kernel(be_ref, nu_ref, tok_ref, h_hbm, wgu_ref, bgu_ref, wd_ref, bd_ref, o_ref,
                   wgu_bf, wd_bf, xbuf_a, xbuf_b, gsem):
    i = pl.program_id(0)
    d_ff = wd_ref.shape[1]
    odd = lax.rem(i, 2) == 1
    pairs = MOE_TILE // 2
    bufs = ((xbuf_a, gsem.at[0]), (xbuf_b, gsem.at[1]))

    def issue_gather(blk, buf, sem):
        base = blk * pairs
        for c in range(pairs):
            word = tok_ref[base + c]
            for r, tok in ((2 * c, word & 0xFFFF), (2 * c + 1, lax.shift_right_logical(word, 16))):
                pltpu.make_async_copy(h_hbm.at[pl.ds(tok, 1), :], buf.at[pl.ds(r, 1), :], sem).start()

    def wait_gather(buf, sem):
        pltpu.make_async_copy(h_hbm.at[pl.ds(0, MOE_TILE), :], buf, sem).wait()

    @pl.when(i == 0)
    def _():
        issue_gather(0, *bufs[0])

    @pl.when(jnp.logical_or(i == 0, be_ref[i] != be_ref[jnp.maximum(i - 1, 0)]))
    def _():
        wgu_bf[...] = wgu_ref[0].astype(BF16)
        wd_bf[...] = wd_ref[0].astype(BF16)

    def block(cur, nxt):
        @pl.when(i <= nu_ref[0])
        def _():
            wait_gather(*cur)

        @pl.when(i < nu_ref[0])
        def _():
            issue_gather(i + 1, *nxt)
            gu = _dot(cur[0][...].astype(BF16), wgu_bf[...]) + bgu_ref[0]
            g = jnp.minimum(gu[:, :d_ff], SWIGLU_LIMIT)
            u = jnp.clip(gu[:, d_ff:], -SWIGLU_LIMIT, SWIGLU_LIMIT)
            act = (u + 1.0) * g * (1.0 / (1.0 + jnp.exp(-SWIGLU_ALPHA * g)))
            y = _dot(act.astype(BF16), wd_bf[...]) + bd_ref[0]
            o_ref[...] = y.astype(o_ref.dtype)

    @pl.when(jnp.logical_not(odd))
    def _():
        block(bufs[0], bufs[1])

    @pl.when(odd)
    def _():
        block(bufs[1], bufs[0])

    @pl.when(i >= nu_ref[0])
    def _():
        o_ref[...] = jnp.zeros_like(o_ref)


def _experts(h, tok_rows, w_gu, b_gu, w_down, b_down, block_expert, n_used, layer):
    t, d = h.shape
    n_rows = tok_rows.shape[0]
    depth, e, _, n_gu = w_gu.shape
    d_ff = w_down.shape[2]
    n_blocks = n_rows // MOE_TILE
    assert t <= 65536
    packed = tok_rows[0::2] | (tok_rows[1::2] << 16)
    grid_spec = pltpu.PrefetchScalarGridSpec(
        num_scalar_prefetch=3,
        grid=(n_blocks,),
        in_specs=[
            pl.BlockSpec(memory_space=pl.ANY),
            pl.BlockSpec((None, 1, d, n_gu), lambda i, be, nu, tk: (layer, be[i], 0, 0)),
            pl.BlockSpec((None, 1, 1, n_gu), lambda i, be, nu, tk: (layer, be[i], 0, 0)),
            pl.BlockSpec((None, 1, d_ff, d), lambda i, be, nu, tk: (layer, be[i], 0, 0)),
            pl.BlockSpec((None, 1, 1, d), lambda i, be, nu, tk: (layer, be[i], 0, 0)),
        ],
        out_specs=pl.BlockSpec((MOE_TILE, d), lambda i, be, nu, tk: (i, 0)),
        scratch_shapes=[pltpu.VMEM((d, n_gu), BF16), pltpu.VMEM((d_ff, d), BF16),
                        pltpu.VMEM((MOE_TILE, d), F32), pltpu.VMEM((MOE_TILE, d), F32),
                        pltpu.SemaphoreType.DMA((2,))],
    )
    return pl.pallas_call(
        _expert_kernel,
        out_shape=jax.ShapeDtypeStruct((n_rows, d), BF16),
        grid_spec=grid_spec,
        compiler_params=_params(("arbitrary",)),
        name="experts",
    )(block_expert, n_used, packed, h, w_gu, b_gu.reshape(depth, e, 1, n_gu), w_down,
      b_down.reshape(depth, e, 1, d))


def _router_kernel(lg_ref, tri_ref, upper_ref, dest_ref, gate_ref, cnt_ref, carry, base):
    phase = pl.program_id(0)
    i = pl.program_id(1)
    tm = lg_ref.shape[0]
    lane = lax.broadcasted_iota(jnp.int32, (tm, N_EXPERTS), 1).astype(F32)
    wide = lax.broadcasted_iota(jnp.int32, (tm, 128), 1)

    work = lg_ref[...]
    hots, vals = [], []
    for _ in range(TOP_K):
        m = jnp.max(work, axis=1, keepdims=True)
        idx = jnp.min(jnp.where(work == m, lane, float(N_EXPERTS)), axis=1, keepdims=True)
        hot = lane == idx
        hots.append(hot)
        vals.append(m)
        work = jnp.where(hot, -jnp.inf, work)
    exps = [jnp.exp(v - vals[0]) for v in vals]
    inv = 1.0 / (exps[0] + exps[1] + exps[2] + exps[3])
    gate_tile = jnp.zeros((tm, 128), F32)
    for k in range(TOP_K):
        gate_tile = jnp.where(wide == k, exps[k] * inv, gate_tile)
    gate_ref[...] = gate_tile

    chosen = jnp.zeros((tm, N_EXPERTS), F32)
    for hot in hots:
        chosen = chosen + hot.astype(F32)

    @pl.when(jnp.logical_and(phase == 0, i == 0))
    def _():
        carry[...] = jnp.zeros_like(carry)
        base[...] = jnp.zeros_like(base)

    @pl.when(jnp.logical_and(phase == 1, i == 0))
    def _():
        counts = carry[...]
        padded = jnp.floor((counts + (MOE_TILE - 1)) * (1.0 / MOE_TILE)) * MOE_TILE
        base[...] = jnp.dot(jnp.broadcast_to(padded, (8, N_EXPERTS)), upper_ref[...],
                            preferred_element_type=F32, precision=lax.Precision.HIGHEST)[0:1, :]
        carry[...] = jnp.zeros_like(carry)

    ahead = _dot(tri_ref[...], chosen.astype(BF16))
    pos = ahead + carry[...] + base[...]
    dest_tile = jnp.zeros((tm, 128), F32)
    for k in range(TOP_K):
        row = jnp.sum(jnp.where(hots[k], pos, 0.0), axis=1, keepdims=True)
        dest_tile = jnp.where(wide == k, row, dest_tile)
    dest_ref[...] = dest_tile.astype(jnp.int32)
    carry[...] = carry[...] + jnp.sum(chosen, axis=0, keepdims=True)
    cnt_ref[...] = carry[...]


def _router(logits):
    t = logits.shape[0]
    nt = t // ROW_TILE
    tri = jnp.asarray(np.tril(np.ones((ROW_TILE, ROW_TILE), np.float32), -1), BF16)
    upper = jnp.asarray(np.triu(np.ones((N_EXPERTS, N_EXPERTS), np.float32), 1), F32)
    return pl.pallas_call(
        _router_kernel,
        out_shape=(jax.ShapeDtypeStruct((t + ROW_TILE, 128), jnp.int32),
                   jax.ShapeDtypeStruct((t + ROW_TILE, 128), F32),
                   jax.ShapeDtypeStruct((1, N_EXPERTS), F32)),
        grid=(2, nt),
        in_specs=[
            pl.BlockSpec((ROW_TILE, N_EXPERTS), lambda p, i: (i, 0)),
            pl.BlockSpec((ROW_TILE, ROW_TILE), lambda p, i: (0, 0)),
            pl.BlockSpec((N_EXPERTS, N_EXPERTS), lambda p, i: (0, 0)),
        ],
        out_specs=(pl.BlockSpec((ROW_TILE, 128), lambda p, i: (p * i + (1 - p) * nt, 0)),
                   pl.BlockSpec((ROW_TILE, 128), lambda p, i: (p * i + (1 - p) * nt, 0)),
                   pl.BlockSpec((1, N_EXPERTS), lambda p, i: (0, 0))),
        scratch_shapes=[pltpu.VMEM((1, N_EXPERTS), F32), pltpu.VMEM((1, N_EXPERTS), F32)],
        compiler_params=_params(("arbitrary", "arbitrary")),
        name="router",
    )(logits, tri, upper)


def _route(logits):
    t = logits.shape[0]
    n_slots = t * TOP_K
    n_rows = n_slots + N_EXPERTS * MOE_TILE
    n_blocks = n_rows // MOE_TILE
    dest_w, gates_w, counts = _router(logits)
    counts = counts.reshape(N_EXPERTS).astype(jnp.int32)
    pad_ends = jnp.cumsum((counts + MOE_TILE - 1) // MOE_TILE * MOE_TILE)
    block_start = jnp.arange(n_blocks, dtype=jnp.int32) * MOE_TILE
    block_expert = jnp.minimum(jnp.sum(block_start[:, None] >= pad_ends[None, :], axis=1),
                               N_EXPERTS - 1).astype(jnp.int32)
    n_used = (pad_ends[-1] // MOE_TILE).astype(jnp.int32).reshape(1)
    dest = dest_w[:t, :TOP_K].T.reshape(-1)
    tok = jnp.tile(jnp.arange(t, dtype=jnp.int32), TOP_K)
    tok_rows = jnp.zeros((n_rows,), jnp.int32).at[dest].set(tok, unique_indices=True)
    return dest, tok_rows, gates_w, block_expert, n_used


def _combine_kernel(x_ref, y0_ref, y1_ref, y2_ref, y3_ref, gt_ref, gf_ref, g_ref, o_ref, *, final):
    gt = gt_ref[...]
    moe = (gt[:, 0:1] * y0_ref[...].astype(F32) + gt[:, 1:2] * y1_ref[...].astype(F32)
           + gt[:, 2:3] * y2_ref[...].astype(F32) + gt[:, 3:4] * y3_ref[...].astype(F32))
    x_new = x_ref[...] + gf_ref[0] * moe
    if final:
        x_new = _norm_mod(x_new, g_ref[...], None, None)
    o_ref[...] = x_new


def _combine(x, y_slots, gates, gate_f, g_final, seq, final):
    t, d = x.shape
    tpb = seq // ROW_TILE
    nt = t // ROW_TILE
    assert TOP_K == 4
    return pl.pallas_call(
        functools.partial(_combine_kernel, final=final),
        out_shape=jax.ShapeDtypeStruct((t, d), F32),
        grid=(nt,),
        in_specs=[pl.BlockSpec((ROW_TILE, d), lambda i: (i, 0))] + [
            pl.BlockSpec((ROW_TILE, d), functools.partial(lambda i, k: (k * nt + i, 0), k=k))
            for k in range(TOP_K)] + [
            pl.BlockSpec((ROW_TILE, 128), lambda i: (i, 0)),
            pl.BlockSpec((1, 1, d), lambda i: (i // tpb, 0, 0)),
            pl.BlockSpec((1, d), lambda i: (0, 0)),
        ],
        out_specs=pl.BlockSpec((ROW_TILE, d), lambda i: (i, 0)),
        compiler_params=_params(("parallel",)),
        name="combine_final" if final else "combine",
    )(x, y_slots, y_slots, y_slots, y_slots, gates, gate_f, g_final.reshape(1, d))


def _moe(x, h, logits, gate_f, w_gu, b_gu, w_down, b_down, g_final, seq, layer, final):
    t, d = x.shape
    dest, tok_rows, gates, block_expert, n_used = _route(logits)
    ybuf = _experts(h, tok_rows, w_gu, b_gu, w_down, b_down, block_expert, n_used, layer)
    y_slots = jnp.take(ybuf, dest, axis=0, mode="clip")
    return _combine(x, y_slots, gates, gate_f, g_final, seq, final)


def kernel(x, c, ada_w, ada_b, norm_mix_g, norm_ffn_g, a_w_qkv, a_w_o, rel_bias, kv_norm_g, w_kvf, b_f,
           b_w_q, b_w_o, router_w, router_b, w_gu, b_gu, w_down, b_down, final_norm_g):
    n_batch, seq, d = x.shape
    t = n_batch * seq
    width = N_HEADS * HEAD_DIM
    xf = x.reshape(t, d)

    ada = _ada(c, ada_w, ada_b)
    mods = [[ada[l, :, i * d:(i + 1) * d].reshape(n_batch, 1, d) for i in range(6)] for l in range(2)]

    shift_m, scale_m, gate_m, shift_f, scale_f, gate_f = mods[0]
    qscale = np.ones((3, 3, 1), np.float32)
    qscale[:, 0] = HEAD_DIM ** -0.5
    w_qkv = (a_w_qkv[0].reshape(d, 3, 3, width) * qscale).astype(BF16)

    qi = np.arange(DIL_BLOCK, dtype=np.int32)[:, None]
    kj = np.arange(2 * DIL_BLOCK, dtype=np.int32)[None, :]
    delta = qi + DIL_BLOCK - kj
    in_band = (delta >= 0) & (delta <= DIL_BLOCK)
    outs, lses = [], []
    for g, (window, dilation) in enumerate(DIL_PAIRS):
        assert window // dilation == DIL_BLOCK
        qkv = _proj(xf, norm_mix_g[0], shift_m, scale_m, w_qkv[:, g].reshape(d, 3 * width), seq, dilation, g)
        bucket = _t5_bucket_np(np.clip(delta, 0, None) * dilation)
        tab = rel_bias[:, g * N_HEADS:(g + 1) * N_HEADS].astype(F32)
        bias = jnp.where(in_band, jnp.transpose(tab[bucket], (2, 0, 1)), NEG)
        bias_first = jnp.where(kj >= DIL_BLOCK, bias, NEG)
        o, lse = _dilated_group(qkv, jnp.stack([bias_first, bias]), g, dilation, n_batch, seq)
        outs.append(o)
        lses.append(_unpermute_rows(lse, dilation)[:, :N_HEADS])
    lse_all = jnp.concatenate(lses, axis=1)
    x1, h1, logits1 = _mixer_tail((outs[0], outs[1], outs[2], lse_all), a_w_o[0].astype(BF16), xf, gate_m,
                                  norm_ffn_g[0], shift_f, scale_f, router_w[0], router_b[0], seq, True)
    x2 = _moe(x1, h1, logits1, gate_f, w_gu, b_gu, w_down, b_down, final_norm_g, seq, 0, False)

    w_f = jnp.pad(jnp.tile(w_kvf[:, 2 * width:], (1, 3)), ((0, 0), (0, 128 - 3 * N_HEADS))).astype(BF16)
    b_fp = jnp.pad(jnp.tile(b_f, 3), (0, 128 - 3 * N_HEADS)).reshape(1, 128)
    k_aug, v_sh, fcum = _kvf(x2, kv_norm_g, _aug_weight(w_kvf[:, :width]), w_kvf[:, width:2 * width].astype(BF16),
                             w_f, b_fp, n_batch)

    shift_m, scale_m, gate_m, shift_f, scale_f, gate_f = mods[1]
    q_aug = _qaug(x2, norm_mix_g[1], shift_m, scale_m, _aug_weight(b_w_q[0] * (HEAD_DIM ** -0.5 * LOG2E)), fcum, seq)
    o1 = _fox(q_aug, k_aug, v_sh, n_batch, seq)

    x3, h3, logits3 = _mixer_tail((o1,), b_w_o[0].astype(BF16), x2, gate_m, norm_ffn_g[1], shift_f, scale_f,
                                  router_w[1], router_b[1], seq, False)
    out = _moe(x3, h3, logits3, gate_f, w_gu, b_gu, w_down, b_down, final_norm_g, seq, 1, True)
    return out.reshape(n_batch, seq, d)
```

```python
import functools
import math

import numpy as np
import jax
import jax.numpy as jnp
from jax import lax
from jax.experimental import pallas as pl
from jax.experimental.pallas import tpu as pltpu

F32 = jnp.float32
BF16 = jnp.bfloat16

D_MODEL = 1024
HEAD_DIM = 64
N_HEADS = 16
DIL_PAIRS = ((128, 1), (512, 4), (2048, 16))
DIL_BLOCK = 128
NUM_BUCKETS = 32
MAX_DISTANCE = 2048
N_EXPERTS = 32
TOP_K = 4
SWIGLU_LIMIT = 7.0
SWIGLU_ALPHA = 1.702
RMS_EPS = 1e-6
NEG = -1e30
LOG2E = math.log2(math.e)

ROW_TILE = 512
MOE_TILE = 512
FOX_TILE = 512
AUG = 128
FOX_SUM_ROWS = 8
VMEM_LIMIT = 56 * 1024 * 1024


def _params(sem, vmem=VMEM_LIMIT):
    return pltpu.CompilerParams(dimension_semantics=sem, vmem_limit_bytes=vmem)


def _dot(a, b):
    return jnp.dot(a, b, preferred_element_type=F32)


def _dot_nt(a, b):
    return lax.dot_general(a, b, (((1,), (1,)), ((), ())), preferred_element_type=F32)


def _dot_tn(a, b):
    return lax.dot_general(a, b, (((0,), (0,)), ((), ())), preferred_element_type=F32)


def _norm_mod(x, g, shift, scale):
    ms = jnp.mean(x * x, axis=-1, keepdims=True)
    y = x * lax.rsqrt(ms + RMS_EPS) * g
    if scale is not None:
        y = y * (1.0 + scale) + shift
    return y


def _ada_kernel(c_ref, w_ref, b_ref, o_ref):
    c = c_ref[...]
    act = c * (1.0 / (1.0 + jnp.exp(-c)))
    o_ref[0] = jnp.dot(act, w_ref[0], preferred_element_type=F32,
                       precision=lax.Precision.HIGHEST) + b_ref[0]


def _ada(c, ada_w, ada_b):
    depth, d, n = ada_w.shape
    bsz = c.shape[0]
    tn = 1536
    return pl.pallas_call(
        _ada_kernel,
        out_shape=jax.ShapeDtypeStruct((depth, bsz, n), F32),
        grid=(depth, n // tn),
        in_specs=[
            pl.BlockSpec((bsz, d), lambda l, j: (0, 0)),
            pl.BlockSpec((1, d, tn), lambda l, j: (l, 0, j)),
            pl.BlockSpec((1, 1, tn), lambda l, j: (l, 0, j)),
        ],
        out_specs=pl.BlockSpec((1, bsz, tn), lambda l, j: (l, 0, j)),
        compiler_params=_params(("parallel", "parallel")),
        name="ada",
    )(c, ada_w, ada_b.reshape(depth, 1, n))


def _perm_tile(dilation):
    return max(ROW_TILE, DIL_BLOCK * dilation)


def _proj_kernel(x_ref, g_ref, sh_ref, sc_ref, w_ref, o_ref, h_scr, *xs_scr, dilation):
    @pl.when(pl.program_id(1) == 0)
    def _():
        if dilation == 1:
            h_scr[...] = _norm_mod(x_ref[...], g_ref[...], sh_ref[0], sc_ref[0]).astype(BF16)
        else:
            (xs,) = xs_scr
            n_lane = xs.shape[0]
            for c in range(n_lane):
                xs[c] = x_ref[:, c * 128:(c + 1) * 128]
            chunk = x_ref.shape[0] // dilation
            for r in range(dilation):
                xr = jnp.concatenate([xs[c, pl.ds(r, chunk, stride=dilation), :] for c in range(n_lane)], axis=1)
                h_scr[r * chunk:(r + 1) * chunk, :] = _norm_mod(xr, g_ref[...], sh_ref[0], sc_ref[0]).astype(BF16)

    o_ref[...] = _dot(h_scr[...], w_ref[...]).astype(o_ref.dtype)


def _proj(x, g, shift, scale, w, seq, dilation, group):
    t, d = x.shape
    n = w.shape[1]
    tm = _perm_tile(dilation)
    tn = d
    tpb = seq // tm
    return pl.pallas_call(
        functools.partial(_proj_kernel, dilation=dilation),
        out_shape=jax.ShapeDtypeStruct((t, n), BF16),
        grid=(t // tm, n // tn),
        in_specs=[
            pl.BlockSpec((tm, d), lambda i, j: (i, 0)),
            pl.BlockSpec((1, d), lambda i, j: (0, 0)),
            pl.BlockSpec((1, 1, d), lambda i, j: (i // tpb, 0, 0)),
            pl.BlockSpec((1, 1, d), lambda i, j: (i // tpb, 0, 0)),
            pl.BlockSpec((d, tn), lambda i, j: (0, j)),
        ],
        out_specs=pl.BlockSpec((tm, tn), lambda i, j: (i, j)),
        scratch_shapes=[pltpu.VMEM((tm, d), BF16)] + (
            [pltpu.VMEM((d // 128, tm, 128), F32)] if dilation > 1 else []),
        compiler_params=_params(("parallel", "arbitrary")),
        name=f"proj{group}",
    )(x, g.reshape(1, d), shift, scale, w)


def _split3(f):
    def top(v):
        return lax.bitcast_convert_type(lax.bitcast_convert_type(v, jnp.uint32) & jnp.uint32(0xFFFF0000), F32)
    hi = top(f)
    r1 = f - hi
    mid = top(r1)
    return hi, mid, r1 - mid


def _forget_pieces(fc):
    hi, mid, lo = _split3(fc)
    lane = lax.broadcasted_iota(jnp.int32, fc.shape, 1)
    x = jnp.where(lane < N_HEADS, hi, jnp.where(lane < 2 * N_HEADS, mid, jnp.where(lane < 3 * N_HEADS, lo, 0.0)))
    return x.astype(BF16)


def _aug_tables(key_side):
    place = np.zeros((128, N_HEADS * AUG), np.float32)
    const = np.zeros((1, N_HEADS * AUG), np.float32)
    for h in range(N_HEADS):
        base = h * AUG + HEAD_DIM
        for piece in range(3):
            if key_side:
                place[piece * N_HEADS + h, base + piece] = 1.0
                const[0, base + 3 + piece] = 1.0
            else:
                place[piece * N_HEADS + h, base + 3 + piece] = 1.0
                const[0, base + piece] = -1.0
    return jnp.asarray(place, BF16), jnp.asarray(const, F32)


def _aug_weight(w):
    d = w.shape[0]
    w = jnp.pad(w.reshape(d, N_HEADS, HEAD_DIM), ((0, 0), (0, 0), (0, AUG - HEAD_DIM)))
    return w.reshape(d, N_HEADS * AUG).astype(BF16)


def _kvf_kernel(x_ref, g_ref, wk_ref, wv_ref, wf_ref, bf_ref, tri_ref, pl_ref, cr_ref,
                k_ref, v_ref, f_ref, carry, *, tpb):
    h = _norm_mod(x_ref[...], g_ref[...], None, None).astype(BF16)
    v_ref[...] = _dot(h, wv_ref[...]).astype(v_ref.dtype)
    z = _dot(h, wf_ref[...]) + bf_ref[...]
    lf = jnp.minimum(z, 0.0) - jnp.log(1.0 + jnp.exp(-jnp.abs(z)))
    hi, mid, lo = _split3(lf)
    tri = tri_ref[...]
    cs = _dot(tri, hi.astype(BF16)) + _dot(tri, mid.astype(BF16)) + _dot(tri, lo.astype(BF16))

    @pl.when(pl.program_id(0) % tpb == 0)
    def _():
        carry[...] = jnp.zeros_like(carry)

    cs = cs + carry[...]
    carry[...] = cs[ROW_TILE - 1:ROW_TILE, :]
    f2 = cs * LOG2E
    f_ref[...] = f2
    k_ref[...] = (_dot(h, wk_ref[...]) + _dot(_forget_pieces(f2), pl_ref[...]) + cr_ref[...]).astype(k_ref.dtype)


def _kvf(x, g, w_k, w_v, w_f, b_f, n_batch):
    t, d = x.shape
    n_aug = N_HEADS * AUG
    tri = jnp.asarray(np.tril(np.ones((ROW_TILE, ROW_TILE), np.float32)), BF16)
    place, const = _aug_tables(True)
    assert (t // ROW_TILE) % n_batch == 0
    full = lambda i: (0, 0)
    row = lambda i: (i, 0)
    return pl.pallas_call(
        functools.partial(_kvf_kernel, tpb=t // ROW_TILE // n_batch),
        out_shape=(jax.ShapeDtypeStruct((t, n_aug), BF16), jax.ShapeDtypeStruct((t, d), BF16),
                   jax.ShapeDtypeStruct((t, 128), F32)),
        grid=(t // ROW_TILE,),
        in_specs=[
            pl.BlockSpec((ROW_TILE, d), row),
            pl.BlockSpec((1, d), full),
            pl.BlockSpec((d, n_aug), full),
            pl.BlockSpec((d, d), full),
            pl.BlockSpec((d, 128), full),
            pl.BlockSpec((1, 128), full),
            pl.BlockSpec((ROW_TILE, ROW_TILE), full),
            pl.BlockSpec((128, n_aug), full),
            pl.BlockSpec((1, n_aug), full),
        ],
        out_specs=(pl.BlockSpec((ROW_TILE, n_aug), row), pl.BlockSpec((ROW_TILE, d), row),
                   pl.BlockSpec((ROW_TILE, 128), row)),
        scratch_shapes=[pltpu.VMEM((1, 128), F32)],
        compiler_params=_params(("arbitrary",)),
        name="kvf",
    )(x, g.reshape(1, d), w_k, w_v, w_f, b_f, tri, place, const)


def _qaug_kernel(x_ref, g_ref, sh_ref, sc_ref, w_ref, f_ref, pl_ref, cr_ref, o_ref):
    h = _norm_mod(x_ref[...], g_ref[...], sh_ref[0], sc_ref[0]).astype(BF16)
    o_ref[...] = (_dot(h, w_ref[...]) + _dot(_forget_pieces(f_ref[...]), pl_ref[...])
                  + cr_ref[...]).astype(o_ref.dtype)


def _qaug(x, g, shift, scale, w, fcum, seq):
    t, d = x.shape
    n_aug = N_HEADS * AUG
    tpb = seq // ROW_TILE
    place, const = _aug_tables(False)
    full = lambda i: (0, 0)
    row = lambda i: (i, 0)
    per_b = lambda i: (i // tpb, 0, 0)
    return pl.pallas_call(
        _qaug_kernel,
        out_shape=jax.ShapeDtypeStruct((t, n_aug), BF16),
        grid=(t // ROW_TILE,),
        in_specs=[
            pl.BlockSpec((ROW_TILE, d), row),
            pl.BlockSpec((1, d), full),
            pl.BlockSpec((1, 1, d), per_b),
            pl.BlockSpec((1, 1, d), per_b),
            pl.BlockSpec((d, n_aug), full),
            pl.BlockSpec((ROW_TILE, 128), row),
            pl.BlockSpec((128, n_aug), full),
            pl.BlockSpec((1, n_aug), full),
        ],
        out_specs=pl.BlockSpec((ROW_TILE, n_aug), row),
        compiler_params=_params(("parallel",)),
        name="qaug",
    )(x, g.reshape(1, d), shift, scale, w, fcum, place, const)


def _dil_kernel(q_ref, kp_ref, kc_ref, vp_ref, vc_ref, bias_ref, o_ref, lse_ref):
    table = jnp.minimum(pl.program_id(2), 1)
    pair_w = 2 * HEAD_DIM
    lane = lax.broadcasted_iota(jnp.int32, (DIL_BLOCK, pair_w), 1)
    first = lane < HEAD_DIM
    first2 = lax.broadcasted_iota(jnp.int32, (2 * DIL_BLOCK, pair_w), 1) < HEAD_DIM
    lse_tile = jnp.zeros((DIL_BLOCK, pair_w), F32)
    zero = jnp.zeros((), BF16)
    for pair in range(N_HEADS // 2):
        sl = slice(pair * pair_w, (pair + 1) * pair_w)
        q = q_ref[:, sl]
        k2 = jnp.concatenate([kp_ref[:, sl], kc_ref[:, sl]], axis=0)
        v2 = jnp.concatenate([vp_ref[:, sl], vc_ref[:, sl]], axis=0)
        probs, invs = [], []
        for which in range(2):
            h = 2 * pair + which
            qh = jnp.where(first, q, zero) if which == 0 else jnp.where(first, zero, q)
            s = _dot_nt(qh, k2) + bias_ref[table, h]
            m = jnp.max(s, axis=-1, keepdims=True)
            p = jnp.exp(s - m)
            den = jnp.sum(p, axis=-1, keepdims=True)
            probs.append(p.astype(BF16))
            invs.append(1.0 / den)
            lse_tile = jnp.where(lane == h, m + jnp.log(den), lse_tile)
        vcat = jnp.concatenate([jnp.where(first2, v2, zero), jnp.where(first2, zero, v2)], axis=0)
        o = _dot(jnp.concatenate(probs, axis=1), vcat)
        o_ref[:, sl] = (o * jnp.where(first, invs[0], invs[1])).astype(o_ref.dtype)
    lse_ref[...] = lse_tile


def _dilated_group(qkv, bias, group, dilation, n_batch, seq):
    t = qkv.shape[0]
    width = N_HEADS * HEAD_DIM
    nb = seq // dilation // DIL_BLOCK
    bpb = seq // DIL_BLOCK

    def blk(b, r, n):
        return b * bpb + n * dilation + r

    def spec(part, prev):
        if prev:
            return pl.BlockSpec((DIL_BLOCK, width), lambda b, r, n: (blk(b, r, jnp.maximum(n - 1, 0)), part))
        return pl.BlockSpec((DIL_BLOCK, width), lambda b, r, n: (blk(b, r, n), part))

    return pl.pallas_call(
        _dil_kernel,
        out_shape=(jax.ShapeDtypeStruct((t, width), BF16), jax.ShapeDtypeStruct((t, 2 * HEAD_DIM), F32)),
        grid=(n_batch, dilation, nb),
        in_specs=[spec(0, False), spec(1, True), spec(1, False), spec(2, True), spec(2, False),
                  pl.BlockSpec((2, N_HEADS, DIL_BLOCK, 2 * DIL_BLOCK), lambda b, r, n: (0, 0, 0, 0))],
        out_specs=(pl.BlockSpec((DIL_BLOCK, width), lambda b, r, n: (blk(b, r, n), 0)),
                   pl.BlockSpec((DIL_BLOCK, 2 * HEAD_DIM), lambda b, r, n: (blk(b, r, n), 0))),
        compiler_params=_params(("parallel", "parallel", "arbitrary")),
        name=f"dilated{group}",
    )(qkv, qkv, qkv, qkv, qkv, bias)


def _unpermute_rows(a, dilation):
    if dilation == 1:
        return a
    t, c = a.shape
    tm = _perm_tile(dilation)
    return a.reshape(t // tm, dilation, tm // dilation, c).transpose(0, 2, 1, 3).reshape(t, c)


def _perm_matrix(rows, dilation):
    per = rows // dilation
    p = np.zeros((rows, rows), np.float32)
    for r in range(dilation):
        for n in range(per):
            p[n * dilation + r, r * per + n] = 1.0
    return jnp.asarray(p, BF16)


def _t5_bucket_np(n):
    max_exact = NUM_BUCKETS // 2
    nf = np.maximum(n, 1).astype(np.float32)
    large = max_exact + (np.log(nf / np.float32(max_exact)) / np.float32(math.log(MAX_DISTANCE / max_exact))
                         * np.float32(NUM_BUCKETS - max_exact)).astype(np.int32)
    large = np.minimum(large, NUM_BUCKETS - 1)
    return np.where(n < max_exact, n, large)


def _mix_tail(mix_in, wo_ref, x_ref, gm_ref, g_ref, sh_ref, sc_ref, wr_ref, br_ref,
              x_out, h_out, lg_out):
    mix = _dot(mix_in, wo_ref[...])
    x_new = x_ref[...] + gm_ref[0] * mix
    x_out[...] = x_new
    h = _norm_mod(x_new, g_ref[...], sh_ref[0], sc_ref[0])
    h_out[...] = h
    lg_out[...] = jnp.dot(h, wr_ref[...], preferred_element_type=F32,
                          precision=lax.Precision.HIGHEST) + br_ref[...]


def _merge_oproj_kernel(o0_ref, o1_ref, o2_ref, p1_ref, p2_ref, lse_ref, ex_ref, wo_ref, x_ref, gm_ref, g_ref,
                        sh_ref, sc_ref, wr_ref, br_ref, x_out, h_out, lg_out):
    lse = lse_ref[...]
    l0, l1, l2 = lse[:, 0:16], lse[:, 16:32], lse[:, 32:48]
    m = jnp.maximum(jnp.maximum(l0, l1), l2)
    e0, e1, e2 = jnp.exp(l0 - m), jnp.exp(l1 - m), jnp.exp(l2 - m)
    inv = 1.0 / (e0 + e1 + e2)
    ex = ex_ref[...]
    o1 = _dot(p1_ref[...], o1_ref[...])
    o2 = _dot(p2_ref[...], o2_ref[0].reshape(ROW_TILE, o2_ref.shape[-1]))
    merged = (_dot((e0 * inv).astype(BF16), ex) * o0_ref[...].astype(F32)
              + _dot((e1 * inv).astype(BF16), ex) * o1
              + _dot((e2 * inv).astype(BF16), ex) * o2)
    _mix_tail(merged.astype(BF16), wo_ref, x_ref, gm_ref, g_ref, sh_ref, sc_ref, wr_ref, br_ref,
              x_out, h_out, lg_out)


def _oproj_kernel(o_ref, wo_ref, x_ref, gm_ref, g_ref, sh_ref, sc_ref, wr_ref, br_ref,
                  x_out, h_out, lg_out):
    _mix_tail(o_ref[...], wo_ref, x_ref, gm_ref, g_ref, sh_ref, sc_ref, wr_ref, br_ref,
              x_out, h_out, lg_out)


def _mixer_tail(mix_inputs, w_o, x, gate_m, g_ffn, shift_f, scale_f, w_r, b_r, seq, merged):
    t, d = x.shape
    tpb = seq // ROW_TILE
    row = lambda i: (i, 0)
    full = lambda i: (0, 0)
    per_b = lambda i: (i // tpb, 0, 0)
    tail_specs = [
        pl.BlockSpec((d, d), full),
        pl.BlockSpec((ROW_TILE, d), row),
        pl.BlockSpec((1, 1, d), per_b),
        pl.BlockSpec((1, d), full),
        pl.BlockSpec((1, 1, d), per_b),
        pl.BlockSpec((1, 1, d), per_b),
        pl.BlockSpec((d, N_EXPERTS), full),
        pl.BlockSpec((1, N_EXPERTS), full),
    ]
    tail_args = (w_o, x, gate_m, g_ffn.reshape(1, d), shift_f, scale_f, w_r, b_r.reshape(1, N_EXPERTS))
    if merged:
        o0, o1, o2, lse = mix_inputs
        expand = jnp.asarray(np.kron(np.eye(N_HEADS, dtype=np.float32),
                                     np.ones((1, HEAD_DIM), np.float32)), BF16)
        d1, d2 = DIL_PAIRS[1][1], DIL_PAIRS[2][1]
        assert _perm_tile(d1) == ROW_TILE
        t2 = _perm_tile(d2)
        sub2 = ROW_TILE // d2
        o2v = o2.reshape(t // t2, d2, t2 // d2, d)
        per2 = t2 // ROW_TILE
        kern = _merge_oproj_kernel
        specs = [pl.BlockSpec((ROW_TILE, d), row), pl.BlockSpec((ROW_TILE, d), row),
                 pl.BlockSpec((1, d2, sub2, d), lambda i: (i // per2, 0, i % per2, 0)),
                 pl.BlockSpec((ROW_TILE, ROW_TILE), full), pl.BlockSpec((ROW_TILE, ROW_TILE), full),
                 pl.BlockSpec((ROW_TILE, 3 * N_HEADS), row), pl.BlockSpec((N_HEADS, d), full)]
        args = (o0, o1, o2v, _perm_matrix(ROW_TILE, d1), _perm_matrix(ROW_TILE, d2), lse, expand)
    else:
        kern = _oproj_kernel
        specs = [pl.BlockSpec((ROW_TILE, d), row)]
        args = mix_inputs
    return pl.pallas_call(
        kern,
        out_shape=(jax.ShapeDtypeStruct((t, d), F32), jax.ShapeDtypeStruct((t, d), F32),
                   jax.ShapeDtypeStruct((t, N_EXPERTS), F32)),
        grid=(t // ROW_TILE,),
        in_specs=specs + tail_specs,
        out_specs=(pl.BlockSpec((ROW_TILE, d), row), pl.BlockSpec((ROW_TILE, d), row),
                   pl.BlockSpec((ROW_TILE, N_EXPERTS), row)),
        compiler_params=_params(("parallel",)),
        name="mixer_tail_merge" if merged else "mixer_tail",
    )(*args, *tail_args)


def _fox_kernel(q_ref, k_ref, v_ref, o_ref, vt_scr):
    qi = pl.program_id(2)
    seq = k_ref.shape[0]
    nk = seq // FOX_TILE
    half = HEAD_DIM

    ext = FOX_SUM_ROWS
    @pl.when(qi == 0)
    def _():
        lane = lax.broadcasted_iota(jnp.int32, (FOX_TILE, 2 * half), 1)
        erow = lax.broadcasted_iota(jnp.int32, (2 * ext, 2 * FOX_TILE), 0)
        ecol = lax.broadcasted_iota(jnp.int32, (2 * ext, 2 * FOX_TILE), 1)
        sum_rows = jnp.where((erow < ext) == (ecol < FOX_TILE), 1.0, 0.0).astype(BF16)
        for j in range(nk):
            vj = v_ref[j * FOX_TILE:(j + 1) * FOX_TILE, :].astype(F32)
            vt_scr[j, :2 * half, :FOX_TILE] = jnp.where(lane < half, vj, 0.0).T.astype(BF16)
            vt_scr[j, :2 * half, FOX_TILE:] = jnp.where(lane >= half, vj, 0.0).T.astype(BF16)
            vt_scr[j, 2 * half:, :] = sum_rows

    q0 = q_ref[:, :AUG]
    q1 = q_ref[:, AUG:]

    def softmax_step(s, m):
        m_new = jnp.maximum(m, jnp.max(s, axis=0, keepdims=True))
        return m_new, jnp.exp2(m - m_new), jnp.exp2((s - m_new).astype(BF16))

    def step(kj, carry, masked):
        m0, m1, acc = carry
        rows = pl.ds(pl.multiple_of(kj * FOX_TILE, FOX_TILE), FOX_TILE)
        s0 = _dot_nt(k_ref[rows, :AUG], q0)
        s1 = _dot_nt(k_ref[rows, AUG:], q1)
        if masked:
            kpos = lax.broadcasted_iota(jnp.int32, (FOX_TILE, FOX_TILE), 0)
            qpos = lax.broadcasted_iota(jnp.int32, (FOX_TILE, FOX_TILE), 1)
            keep = kpos <= qpos
            s0 = jnp.where(keep, s0, NEG)
            s1 = jnp.where(keep, s1, NEG)
        m0, a0, p0 = softmax_step(s0, m0)
        m1, a1, p1 = softmax_step(s1, m1)
        pv = _dot(vt_scr[kj], jnp.concatenate([p0, p1], axis=0))
        alpha = jnp.concatenate([jnp.broadcast_to(a0, (half, FOX_TILE)), jnp.broadcast_to(a1, (half, FOX_TILE)),
                                 jnp.broadcast_to(a0, (ext, FOX_TILE)), jnp.broadcast_to(a1, (ext, FOX_TILE))],
                                axis=0)
        return m0, m1, alpha * acc + pv

    neg = jnp.full((1, FOX_TILE), NEG, F32)
    init = (neg, neg, jnp.zeros((2 * half + 2 * ext, FOX_TILE), F32))
    carry = lax.fori_loop(0, qi, lambda kj, c: step(kj, c, False), init)
    _, _, acc = step(qi, carry, True)
    inv0 = 1.0 / acc[2 * half:2 * half + 1, :]
    inv1 = 1.0 / acc[2 * half + ext:2 * half + ext + 1, :]
    inv = jnp.concatenate([jnp.broadcast_to(inv0, (half, FOX_TILE)),
                           jnp.broadcast_to(inv1, (half, FOX_TILE))], axis=0)
    o_ref[...] = (acc[:2 * half, :] * inv).T.astype(o_ref.dtype)


def _fox(q_aug, k_aug, v, n_batch, seq):
    t = q_aug.shape[0]
    nq = seq // FOX_TILE
    pairs = N_HEADS // 2
    return pl.pallas_call(
        _fox_kernel,
        out_shape=jax.ShapeDtypeStruct((t, N_HEADS * HEAD_DIM), BF16),
        grid=(n_batch, pairs, nq),
        in_specs=[
            pl.BlockSpec((FOX_TILE, 2 * AUG), lambda b, p, qi: (b * nq + qi, p)),
            pl.BlockSpec((seq, 2 * AUG), lambda b, p, qi: (b, p)),
            pl.BlockSpec((seq, 2 * HEAD_DIM), lambda b, p, qi: (b, p)),
        ],
        out_specs=pl.BlockSpec((FOX_TILE, 2 * HEAD_DIM), lambda b, p, qi: (b * nq + qi, p)),
        scratch_shapes=[pltpu.VMEM((nq, 2 * HEAD_DIM + 2 * FOX_SUM_ROWS, 2 * FOX_TILE), BF16)],
        compiler_params=_params(("parallel", "parallel", "arbitrary")),
        name="fox",
    )(q_aug, k_aug, v)


def _expert_kernel(be_ref, nu_ref, tok_ref, h_hbm, wgu_ref, bgu_ref, wd_ref, bd_ref, o_ref,
                   wgu_bf, wd_bf, xbuf_a, xbuf_b, gsem):
    i = pl.program_id(0)
    d_ff = wd_ref.shape[1]
    odd = lax.rem(i, 2) == 1
    pairs = MOE_TILE // 2
    bufs = ((xbuf_a, gsem.at[0]), (xbuf_b, gsem.at[1]))

    def issue_gather(blk, buf, sem):
        base = blk * pairs
        for c in range(pairs):
            word = tok_ref[base + c]
            for r, tok in ((2 * c, word & 0xFFFF), (2 * c + 1, lax.shift_right_logical(word, 16))):
                pltpu.make_async_copy(h_hbm.at[pl.ds(tok, 1), :], buf.at[pl.ds(r, 1), :], sem).start()

    def wait_gather(buf, sem):
        pltpu.make_async_copy(h_hbm.at[pl.ds(0, MOE_TILE), :], buf, sem).wait()

    @pl.when(i == 0)
    def _():
        issue_gather(0, *bufs[0])

    @pl.when(jnp.logical_or(i == 0, be_ref[i] != be_ref[jnp.maximum(i - 1, 0)]))
    def _():
        wgu_bf[...] = wgu_ref[0].astype(BF16)
        wd_bf[...] = wd_ref[0].astype(BF16)

    def block(cur, nxt):
        @pl.when(i <= nu_ref[0])
        def _():
            wait_gather(*cur)

        @pl.when(i < nu_ref[0])
        def _():
            issue_gather(i + 1, *nxt)
            gu = _dot(cur[0][...].astype(BF16), wgu_bf[...]) + bgu_ref[0]
            g = jnp.minimum(gu[:, :d_ff], SWIGLU_LIMIT)
            u = jnp.clip(gu[:, d_ff:], -SWIGLU_LIMIT, SWIGLU_LIMIT)
            act = (u + 1.0) * g * (1.0 / (1.0 + jnp.exp(-SWIGLU_ALPHA * g)))
            y = _dot(act.astype(BF16), wd_bf[...]) + bd_ref[0]
            o_ref[...] = y.astype(o_ref.dtype)

    @pl.when(jnp.logical_not(odd))
    def _():
        block(bufs[0], bufs[1])

    @pl.when(odd)
    def _():
        block(bufs[1], bufs[0])

    @pl.when(i >= nu_ref[0])
    def _():
        o_ref[...] = jnp.zeros_like(o_ref)


def _experts(h, tok_rows, w_gu, b_gu, w_down, b_down, block_expert, n_used, layer):
    t, d = h.shape
    n_rows = tok_rows.shape[0]
    depth, e, _, n_gu = w_gu.shape
    d_ff = w_down.shape[2]
    n_blocks = n_rows // MOE_TILE
    assert t <= 65536
    packed = tok_rows[0::2] | (tok_rows[1::2] << 16)
    grid_spec = pltpu.PrefetchScalarGridSpec(
        num_scalar_prefetch=3,
        grid=(n_blocks,),
        in_specs=[
            pl.BlockSpec(memory_space=pl.ANY),
            pl.BlockSpec((None, 1, d, n_gu), lambda i, be, nu, tk: (layer, be[i], 0, 0)),
            pl.BlockSpec((None, 1, 1, n_gu), lambda i, be, nu, tk: (layer, be[i], 0, 0)),
            pl.BlockSpec((None, 1, d_ff, d), lambda i, be, nu, tk: (layer, be[i], 0, 0)),
            pl.BlockSpec((None, 1, 1, d), lambda i, be, nu, tk: (layer, be[i], 0, 0)),
        ],
        out_specs=pl.BlockSpec((MOE_TILE, d), lambda i, be, nu, tk: (i, 0)),
        scratch_shapes=[pltpu.VMEM((d, n_gu), BF16), pltpu.VMEM((d_ff, d), BF16),
                        pltpu.VMEM((MOE_TILE, d), F32), pltpu.VMEM((MOE_TILE, d), F32),
                        pltpu.SemaphoreType.DMA((2,))],
    )
    return pl.pallas_call(
        _expert_kernel,
        out_shape=jax.ShapeDtypeStruct((n_rows, d), BF16),
        grid_spec=grid_spec,
        compiler_params=_params(("arbitrary",)),
        name="experts",
    )(block_expert, n_used, packed, h, w_gu, b_gu.reshape(depth, e, 1, n_gu), w_down,
      b_down.reshape(depth, e, 1, d))


def _router_kernel(lg_ref, tri_ref, upper_ref, dest_ref, gate_ref, cnt_ref, carry, base):
    phase = pl.program_id(0)
    i = pl.program_id(1)
    tm = lg_ref.shape[0]
    lane = lax.broadcasted_iota(jnp.int32, (tm, N_EXPERTS), 1).astype(F32)
    wide = lax.broadcasted_iota(jnp.int32, (tm, 128), 1)

    work = lg_ref[...]
    hots, vals = [], []
    for _ in range(TOP_K):
        m = jnp.max(work, axis=1, keepdims=True)
        idx = jnp.min(jnp.where(work == m, lane, float(N_EXPERTS)), axis=1, keepdims=True)
        hot = lane == idx
        hots.append(hot)
        vals.append(m)
        work = jnp.where(hot, -jnp.inf, work)
    exps = [jnp.exp(v - vals[0]) for v in vals]
    inv = 1.0 / (exps[0] + exps[1] + exps[2] + exps[3])
    gate_tile = jnp.zeros((tm, 128), F32)
    for k in range(TOP_K):
        gate_tile = jnp.where(wide == k, exps[k] * inv, gate_tile)
    gate_ref[...] = gate_tile

    chosen = jnp.zeros((tm, N_EXPERTS), F32)
    for hot in hots:
        chosen = chosen + hot.astype(F32)

    @pl.when(jnp.logical_and(phase == 0, i == 0))
    def _():
        carry[...] = jnp.zeros_like(carry)
        base[...] = jnp.zeros_like(base)

    @pl.when(jnp.logical_and(phase == 1, i == 0))
    def _():
        counts = carry[...]
        padded = jnp.floor((counts + (MOE_TILE - 1)) * (1.0 / MOE_TILE)) * MOE_TILE
        base[...] = jnp.dot(jnp.broadcast_to(padded, (8, N_EXPERTS)), upper_ref[...],
                            preferred_element_type=F32, precision=lax.Precision.HIGHEST)[0:1, :]
        carry[...] = jnp.zeros_like(carry)

    ahead = _dot(tri_ref[...], chosen.astype(BF16))
    pos = ahead + carry[...] + base[...]
    dest_tile = jnp.zeros((tm, 128), F32)
    for k in range(TOP_K):
        row = jnp.sum(jnp.where(hots[k], pos, 0.0), axis=1, keepdims=True)
        dest_tile = jnp.where(wide == k, row, dest_tile)
    dest_ref[...] = dest_tile.astype(jnp.int32)
    carry[...] = carry[...] + jnp.sum(chosen, axis=0, keepdims=True)
    cnt_ref[...] = carry[...]


def _router(logits):
    t = logits.shape[0]
    nt = t // ROW_TILE
    tri = jnp.asarray(np.tril(np.ones((ROW_TILE, ROW_TILE), np.float32), -1), BF16)
    upper = jnp.asarray(np.triu(np.ones((N_EXPERTS, N_EXPERTS), np.float32), 1), F32)
    return pl.pallas_call(
        _router_kernel,
        out_shape=(jax.ShapeDtypeStruct((t + ROW_TILE, 128), jnp.int32),
                   jax.ShapeDtypeStruct((t + ROW_TILE, 128), F32),
                   jax.ShapeDtypeStruct((1, N_EXPERTS), F32)),
        grid=(2, nt),
        in_specs=[
            pl.BlockSpec((ROW_TILE, N_EXPERTS), lambda p, i: (i, 0)),
            pl.BlockSpec((ROW_TILE, ROW_TILE), lambda p, i: (0, 0)),
            pl.BlockSpec((N_EXPERTS, N_EXPERTS), lambda p, i: (0, 0)),
        ],
        out_specs=(pl.BlockSpec((ROW_TILE, 128), lambda p, i: (p * i + (1 - p) * nt, 0)),
                   pl.BlockSpec((ROW_TILE, 128), lambda p, i: (p * i + (1 - p) * nt, 0)),
                   pl.BlockSpec((1, N_EXPERTS), lambda p, i: (0, 0))),
        scratch_shapes=[pltpu.VMEM((1, N_EXPERTS), F32), pltpu.VMEM((1, N_EXPERTS), F32)],
        compiler_params=_params(("arbitrary", "arbitrary")),
        name="router",
    )(logits, tri, upper)


def _route(logits):
    t = logits.shape[0]
    n_slots = t * TOP_K
    n_rows = n_slots + N_EXPERTS * MOE_TILE
    n_blocks = n_rows // MOE_TILE
    dest_w, gates_w, counts = _router(logits)
    counts = counts.reshape(N_EXPERTS).astype(jnp.int32)
    pad_ends = jnp.cumsum((counts + MOE_TILE - 1) // MOE_TILE * MOE_TILE)
    block_start = jnp.arange(n_blocks, dtype=jnp.int32) * MOE_TILE
    block_expert = jnp.minimum(jnp.sum(block_start[:, None] >= pad_ends[None, :], axis=1),
                               N_EXPERTS - 1).astype(jnp.int32)
    n_used = (pad_ends[-1] // MOE_TILE).astype(jnp.int32).reshape(1)
    dest = dest_w[:t, :TOP_K].T.reshape(-1)
    tok = jnp.tile(jnp.arange(t, dtype=jnp.int32), TOP_K)
    tok_rows = (jnp.arange(n_rows, dtype=jnp.int32) % t).at[dest].set(tok, unique_indices=True)
    return dest, tok_rows, gates_w, block_expert, n_used


def _combine_kernel(x_ref, y0_ref, y1_ref, y2_ref, y3_ref, gt_ref, gf_ref, g_ref, o_ref, *, final):
    gt = gt_ref[...]
    moe = (gt[:, 0:1] * y0_ref[...].astype(F32) + gt[:, 1:2] * y1_ref[...].astype(F32)
           + gt[:, 2:3] * y2_ref[...].astype(F32) + gt[:, 3:4] * y3_ref[...].astype(F32))
    x_new = x_ref[...] + gf_ref[0] * moe
    if final:
        x_new = _norm_mod(x_new, g_ref[...], None, None)
    o_ref[...] = x_new


def _combine(x, y_slots, gates, gate_f, g_final, seq, final):
    t, d = x.shape
    tpb = seq // ROW_TILE
    nt = t // ROW_TILE
    assert TOP_K == 4
    return pl.pallas_call(
        functools.partial(_combine_kernel, final=final),
        out_shape=jax.ShapeDtypeStruct((t, d), F32),
        grid=(nt,),
        in_specs=[pl.BlockSpec((ROW_TILE, d), lambda i: (i, 0))] + [
            pl.BlockSpec((ROW_TILE, d), functools.partial(lambda i, k: (k * nt + i, 0), k=k))
            for k in range(TOP_K)] + [
            pl.BlockSpec((ROW_TILE, 128), lambda i: (i, 0)),
            pl.BlockSpec((1, 1, d), lambda i: (i // tpb, 0, 0)),
            pl.BlockSpec((1, d), lambda i: (0, 0)),
        ],
        out_specs=pl.BlockSpec((ROW_TILE, d), lambda i: (i, 0)),
        compiler_params=_params(("parallel",)),
        name="combine_final" if final else "combine",
    )(x, y_slots, y_slots, y_slots, y_slots, gates, gate_f, g_final.reshape(1, d))


def _moe(x, h, logits, gate_f, w_gu, b_gu, w_down, b_down, g_final, seq, layer, final):
    t, d = x.shape
    dest, tok_rows, gates, block_expert, n_used = _route(logits)
    ybuf = _experts(h, tok_rows, w_gu, b_gu, w_down, b_down, block_expert, n_used, layer)
    y_slots = jnp.take(ybuf, dest, axis=0, mode="clip")
    return _combine(x, y_slots, gates, gate_f, g_final, seq, final)


def kernel(x, c, ada_w, ada_b, norm_mix_g, norm_ffn_g, a_w_qkv, a_w_o, rel_bias, kv_norm_g, w_kvf, b_f,
           b_w_q, b_w_o, router_w, router_b, w_gu, b_gu, w_down, b_down, final_norm_g):
    n_batch, seq, d = x.shape
    t = n_batch * seq
    width = N_HEADS * HEAD_DIM
    xf = x.reshape(t, d)

    ada = _ada(c, ada_w, ada_b)
    mods = [[ada[l, :, i * d:(i + 1) * d].reshape(n_batch, 1, d) for i in range(6)] for l in range(2)]

    shift_m, scale_m, gate_m, shift_f, scale_f, gate_f = mods[0]
    qscale = np.ones((3, 3, 1), np.float32)
    qscale[:, 0] = HEAD_DIM ** -0.5
    w_qkv = (a_w_qkv[0].reshape(d, 3, 3, width) * qscale).astype(BF16)

    qi = np.arange(DIL_BLOCK, dtype=np.int32)[:, None]
    kj = np.arange(2 * DIL_BLOCK, dtype=np.int32)[None, :]
    delta = qi + DIL_BLOCK - kj
    in_band = (delta >= 0) & (delta <= DIL_BLOCK)
    outs, lses = [], []
    for g, (window, dilation) in enumerate(DIL_PAIRS):
        assert window // dilation == DIL_BLOCK
        qkv = _proj(xf, norm_mix_g[0], shift_m, scale_m, w_qkv[:, g].reshape(d, 3 * width), seq, dilation, g)
        bucket = _t5_bucket_np(np.clip(delta, 0, None) * dilation)
        tab = rel_bias[:, g * N_HEADS:(g + 1) * N_HEADS].astype(F32)
        bias = jnp.where(in_band, jnp.transpose(tab[bucket], (2, 0, 1)), NEG)
        bias_first = jnp.where(kj >= DIL_BLOCK, bias, NEG)
        o, lse = _dilated_group(qkv, jnp.stack([bias_first, bias]), g, dilation, n_batch, seq)
        outs.append(o)
        lses.append(_unpermute_rows(lse, dilation)[:, :N_HEADS])
    lse_all = jnp.concatenate(lses, axis=1)
    x1, h1, logits1 = _mixer_tail((outs[0], outs[1], outs[2], lse_all), a_w_o[0].astype(BF16), xf, gate_m,
                                  norm_ffn_g[0], shift_f, scale_f, router_w[0], router_b[0], seq, True)
    x2 = _moe(x1, h1, logits1, gate_f, w_gu, b_gu, w_down, b_down, final_norm_g, seq, 0, False)

    w_f = jnp.pad(jnp.tile(w_kvf[:, 2 * width:], (1, 3)), ((0, 0), (0, 128 - 3 * N_HEADS))).astype(BF16)
    b_fp = jnp.pad(jnp.tile(b_f, 3), (0, 128 - 3 * N_HEADS)).reshape(1, 128)
    k_aug, v_sh, fcum = _kvf(x2, kv_norm_g, _aug_weight(w_kvf[:, :width]), w_kvf[:, width:2 * width].astype(BF16),
                             w_f, b_fp, n_batch)

    shift_m, scale_m, gate_m, shift_f, scale_f, gate_f = mods[1]
    q_aug = _qaug(x2, norm_mix_g[1], shift_m, scale_m, _aug_weight(b_w_q[0] * (HEAD_DIM ** -0.5 * LOG2E)), fcum, seq)
    o1 = _fox(q_aug, k_aug, v_sh, n_batch, seq)

    x3, h3, logits3 = _mixer_tail((o1,), b_w_o[0].astype(BF16), x2, gate_m, norm_ffn_g[1], shift_f, scale_f,
                                  router_w[1], router_b[1], seq, False)
    out = _moe(x3, h3, logits3, gate_f, w_gu, b_gu, w_down, b_down, final_norm_g, seq, 1, True)
    return out.reshape(n_batch, seq, d)
```

```python
import functools
import math

import numpy as np
import jax
import jax.numpy as jnp
from jax import lax
from jax.experimental import pallas as pl
from jax.experimental.pallas import tpu as pltpu

F32 = jnp.float32
BF16 = jnp.bfloat16

D_MODEL = 1024
HEAD_DIM = 64
N_HEADS = 16
DIL_PAIRS = ((128, 1), (512, 4), (2048, 16))
DIL_BLOCK = 128
NUM_BUCKETS = 32
MAX_DISTANCE = 2048
N_EXPERTS = 32
TOP_K = 4
SWIGLU_LIMIT = 7.0
SWIGLU_ALPHA = 1.702
RMS_EPS = 1e-6
NEG = -1e30
LOG2E = math.log2(math.e)

ROW_TILE = 512
MOE_TILE = 512
FOX_TILE = 512
AUG = 128
FOX_SUM_ROWS = 8
VMEM_LIMIT = 56 * 1024 * 1024


def _params(sem, vmem=VMEM_LIMIT):
    return pltpu.CompilerParams(dimension_semantics=sem, vmem_limit_bytes=vmem)


def _dot(a, b):
    return jnp.dot(a, b, preferred_element_type=F32)


def _dot_nt(a, b):
    return lax.dot_general(a, b, (((1,), (1,)), ((), ())), preferred_element_type=F32)


def _dot_tn(a, b):
    return lax.dot_general(a, b, (((0,), (0,)), ((), ())), preferred_element_type=F32)


def _norm_mod(x, g, shift, scale):
    ms = jnp.mean(x * x, axis=-1, keepdims=True)
    y = x * lax.rsqrt(ms + RMS_EPS) * g
    if scale is not None:
        y = y * (1.0 + scale) + shift
    return y


def _ada_kernel(c_ref, w_ref, b_ref, o_ref):
    c = c_ref[...]
    act = c * (1.0 / (1.0 + jnp.exp(-c)))
    o_ref[0] = jnp.dot(act, w_ref[0], preferred_element_type=F32,
                       precision=lax.Precision.HIGHEST) + b_ref[0]


def _ada(c, ada_w, ada_b):
    depth, d, n = ada_w.shape
    bsz = c.shape[0]
    tn = 1536
    return pl.pallas_call(
        _ada_kernel,
        out_shape=jax.ShapeDtypeStruct((depth, bsz, n), F32),
        grid=(depth, n // tn),
        in_specs=[
            pl.BlockSpec((bsz, d), lambda l, j: (0, 0)),
            pl.BlockSpec((1, d, tn), lambda l, j: (l, 0, j)),
            pl.BlockSpec((1, 1, tn), lambda l, j: (l, 0, j)),
        ],
        out_specs=pl.BlockSpec((1, bsz, tn), lambda l, j: (l, 0, j)),
        compiler_params=_params(("parallel", "parallel")),
        name="ada",
    )(c, ada_w, ada_b.reshape(depth, 1, n))


def _perm_tile(dilation):
    return max(ROW_TILE, DIL_BLOCK * dilation)


def _proj_kernel(x_ref, g_ref, sh_ref, sc_ref, w_ref, o_ref, h_scr, *xs_scr, dilation):
    @pl.when(pl.program_id(1) == 0)
    def _():
        if dilation == 1:
            h_scr[...] = _norm_mod(x_ref[...], g_ref[...], sh_ref[0], sc_ref[0]).astype(BF16)
        else:
            (xs,) = xs_scr
            n_lane = xs.shape[0]
            for c in range(n_lane):
                xs[c] = x_ref[:, c * 128:(c + 1) * 128]
            chunk = x_ref.shape[0] // dilation
            for r in range(dilation):
                xr = jnp.concatenate([xs[c, pl.ds(r, chunk, stride=dilation), :] for c in range(n_lane)], axis=1)
                h_scr[r * chunk:(r + 1) * chunk, :] = _norm_mod(xr, g_ref[...], sh_ref[0], sc_ref[0]).astype(BF16)

    o_ref[...] = _dot(h_scr[...], w_ref[...]).astype(o_ref.dtype)


def _proj(x, g, shift, scale, w, seq, dilation, group):
    t, d = x.shape
    n = w.shape[1]
    tm = _perm_tile(dilation)
    tn = d
    tpb = seq // tm
    return pl.pallas_call(
        functools.partial(_proj_kernel, dilation=dilation),
        out_shape=jax.ShapeDtypeStruct((t, n), BF16),
        grid=(t // tm, n // tn),
        in_specs=[
            pl.BlockSpec((tm, d), lambda i, j: (i, 0)),
            pl.BlockSpec((1, d), lambda i, j: (0, 0)),
            pl.BlockSpec((1, 1, d), lambda i, j: (i // tpb, 0, 0)),
            pl.BlockSpec((1, 1, d), lambda i, j: (i // tpb, 0, 0)),
            pl.BlockSpec((d, tn), lambda i, j: (0, j)),
        ],
        out_specs=pl.BlockSpec((tm, tn), lambda i, j: (i, j)),
        scratch_shapes=[pltpu.VMEM((tm, d), BF16)] + (
            [pltpu.VMEM((d // 128, tm, 128), F32)] if dilation > 1 else []),
        compiler_params=_params(("parallel", "arbitrary")),
        name=f"proj{group}",
    )(x, g.reshape(1, d), shift, scale, w)


def _split3(f):
    def top(v):
        return lax.bitcast_convert_type(lax.bitcast_convert_type(v, jnp.uint32) & jnp.uint32(0xFFFF0000), F32)
    hi = top(f)
    r1 = f - hi
    mid = top(r1)
    return hi, mid, r1 - mid


def _forget_pieces(fc):
    hi, mid, lo = _split3(fc)
    lane = lax.broadcasted_iota(jnp.int32, fc.shape, 1)
    x = jnp.where(lane < N_HEADS, hi, jnp.where(lane < 2 * N_HEADS, mid, jnp.where(lane < 3 * N_HEADS, lo, 0.0)))
    return x.astype(BF16)


def _aug_tables(key_side):
    place = np.zeros((128, N_HEADS * AUG), np.float32)
    const = np.zeros((1, N_HEADS * AUG), np.float32)
    for h in range(N_HEADS):
        base = h * AUG + HEAD_DIM
        for piece in range(3):
            if key_side:
                place[piece * N_HEADS + h, base + piece] = 1.0
                const[0, base + 3 + piece] = 1.0
            else:
                place[piece * N_HEADS + h, base + 3 + piece] = 1.0
                const[0, base + piece] = -1.0
    return jnp.asarray(place, BF16), jnp.asarray(const, F32)


def _aug_weight(w):
    d = w.shape[0]
    w = jnp.pad(w.reshape(d, N_HEADS, HEAD_DIM), ((0, 0), (0, 0), (0, AUG - HEAD_DIM)))
    return w.reshape(d, N_HEADS * AUG).astype(BF16)


def _kvf_kernel(x_ref, g_ref, wk_ref, wv_ref, wf_ref, bf_ref, tri_ref, pl_ref, cr_ref,
                k_ref, v_ref, f_ref, carry, *, tpb):
    h = _norm_mod(x_ref[...], g_ref[...], None, None).astype(BF16)
    v_ref[...] = _dot(h, wv_ref[...]).astype(v_ref.dtype)
    z = _dot(h, wf_ref[...]) + bf_ref[...]
    lf = jnp.minimum(z, 0.0) - jnp.log(1.0 + jnp.exp(-jnp.abs(z)))
    hi, mid, lo = _split3(lf)
    tri = tri_ref[...]
    cs = _dot(tri, hi.astype(BF16)) + _dot(tri, mid.astype(BF16)) + _dot(tri, lo.astype(BF16))

    @pl.when(pl.program_id(0) % tpb == 0)
    def _():
        carry[...] = jnp.zeros_like(carry)

    cs = cs + carry[...]
    carry[...] = cs[ROW_TILE - 1:ROW_TILE, :]
    f2 = cs * LOG2E
    f_ref[...] = f2
    k_ref[...] = (_dot(h, wk_ref[...]) + _dot(_forget_pieces(f2), pl_ref[...]) + cr_ref[...]).astype(k_ref.dtype)


def _kvf(x, g, w_k, w_v, w_f, b_f, n_batch):
    t, d = x.shape
    n_aug = N_HEADS * AUG
    tri = jnp.asarray(np.tril(np.ones((ROW_TILE, ROW_TILE), np.float32)), BF16)
    place, const = _aug_tables(True)
    assert (t // ROW_TILE) % n_batch == 0
    full = lambda i: (0, 0)
    row = lambda i: (i, 0)
    return pl.pallas_call(
        functools.partial(_kvf_kernel, tpb=t // ROW_TILE // n_batch),
        out_shape=(jax.ShapeDtypeStruct((t, n_aug), BF16), jax.ShapeDtypeStruct((t, d), BF16),
                   jax.ShapeDtypeStruct((t, 128), F32)),
        grid=(t // ROW_TILE,),
        in_specs=[
            pl.BlockSpec((ROW_TILE, d), row),
            pl.BlockSpec((1, d), full),
            pl.BlockSpec((d, n_aug), full),
            pl.BlockSpec((d, d), full),
            pl.BlockSpec((d, 128), full),
            pl.BlockSpec((1, 128), full),
            pl.BlockSpec((ROW_TILE, ROW_TILE), full),
            pl.BlockSpec((128, n_aug), full),
            pl.BlockSpec((1, n_aug), full),
        ],
        out_specs=(pl.BlockSpec((ROW_TILE, n_aug), row), pl.BlockSpec((ROW_TILE, d), row),
                   pl.BlockSpec((ROW_TILE, 128), row)),
        scratch_shapes=[pltpu.VMEM((1, 128), F32)],
        compiler_params=_params(("arbitrary",)),
        name="kvf",
    )(x, g.reshape(1, d), w_k, w_v, w_f, b_f, tri, place, const)


def _qaug_kernel(x_ref, g_ref, sh_ref, sc_ref, w_ref, f_ref, pl_ref, cr_ref, o_ref):
    h = _norm_mod(x_ref[...], g_ref[...], sh_ref[0], sc_ref[0]).astype(BF16)
    o_ref[...] = (_dot(h, w_ref[...]) + _dot(_forget_pieces(f_ref[...]), pl_ref[...])
                  + cr_ref[...]).astype(o_ref.dtype)


def _qaug(x, g, shift, scale, w, fcum, seq):
    t, d = x.shape
    n_aug = N_HEADS * AUG
    tpb = seq // ROW_TILE
    place, const = _aug_tables(False)
    full = lambda i: (0, 0)
    row = lambda i: (i, 0)
    per_b = lambda i: (i // tpb, 0, 0)
    return pl.pallas_call(
        _qaug_kernel,
        out_shape=jax.ShapeDtypeStruct((t, n_aug), BF16),
        grid=(t // ROW_TILE,),
        in_specs=[
            pl.BlockSpec((ROW_TILE, d), row),
            pl.BlockSpec((1, d), full),
            pl.BlockSpec((1, 1, d), per_b),
            pl.BlockSpec((1, 1, d), per_b),
            pl.BlockSpec((d, n_aug), full),
            pl.BlockSpec((ROW_TILE, 128), row),
            pl.BlockSpec((128, n_aug), full),
            pl.BlockSpec((1, n_aug), full),
        ],
        out_specs=pl.BlockSpec((ROW_TILE, n_aug), row),
        compiler_params=_params(("parallel",)),
        name="qaug",
    )(x, g.reshape(1, d), shift, scale, w, fcum, place, const)


def _dil_kernel(q_ref, kp_ref, kc_ref, vp_ref, vc_ref, bias_ref, o_ref, lse_ref):
    table = jnp.minimum(pl.program_id(2), 1)
    pair_w = 2 * HEAD_DIM
    lane = lax.broadcasted_iota(jnp.int32, (DIL_BLOCK, pair_w), 1)
    first = lane < HEAD_DIM
    first2 = lax.broadcasted_iota(jnp.int32, (2 * DIL_BLOCK, pair_w), 1) < HEAD_DIM
    lse_tile = jnp.zeros((DIL_BLOCK, pair_w), F32)
    zero = jnp.zeros((), BF16)
    for pair in range(N_HEADS // 2):
        sl = slice(pair * pair_w, (pair + 1) * pair_w)
        q = q_ref[:, sl]
        k2 = jnp.concatenate([kp_ref[:, sl], kc_ref[:, sl]], axis=0)
        v2 = jnp.concatenate([vp_ref[:, sl], vc_ref[:, sl]], axis=0)
        probs, invs = [], []
        for which in range(2):
            h = 2 * pair + which
            qh = jnp.where(first, q, zero) if which == 0 else jnp.where(first, zero, q)
            s = _dot_nt(qh, k2) + bias_ref[table, h]
            m = jnp.max(s, axis=-1, keepdims=True)
            p = jnp.exp(s - m)
            den = jnp.sum(p, axis=-1, keepdims=True)
            probs.append(p.astype(BF16))
            invs.append(1.0 / den)
            lse_tile = jnp.where(lane == h, m + jnp.log(den), lse_tile)
        vcat = jnp.concatenate([jnp.where(first2, v2, zero), jnp.where(first2, zero, v2)], axis=0)
        o = _dot(jnp.concatenate(probs, axis=1), vcat)
        o_ref[:, sl] = (o * jnp.where(first, invs[0], invs[1])).astype(o_ref.dtype)
    lse_ref[...] = lse_tile


def _dilated_group(qkv, bias, group, dilation, n_batch, seq):
    t = qkv.shape[0]
    width = N_HEADS * HEAD_DIM
    nb = seq // dilation // DIL_BLOCK
    bpb = seq // DIL_BLOCK

    def blk(b, r, n):
        return b * bpb + n * dilation + r

    def spec(part, prev):
        if prev:
            return pl.BlockSpec((DIL_BLOCK, width), lambda b, r, n: (blk(b, r, jnp.maximum(n - 1, 0)), part))
        return pl.BlockSpec((DIL_BLOCK, width), lambda b, r, n: (blk(b, r, n), part))

    return pl.pallas_call(
        _dil_kernel,
        out_shape=(jax.ShapeDtypeStruct((t, width), BF16), jax.ShapeDtypeStruct((t, 2 * HEAD_DIM), F32)),
        grid=(n_batch, dilation, nb),
        in_specs=[spec(0, False), spec(1, True), spec(1, False), spec(2, True), spec(2, False),
                  pl.BlockSpec((2, N_HEADS, DIL_BLOCK, 2 * DIL_BLOCK), lambda b, r, n: (0, 0, 0, 0))],
        out_specs=(pl.BlockSpec((DIL_BLOCK, width), lambda b, r, n: (blk(b, r, n), 0)),
                   pl.BlockSpec((DIL_BLOCK, 2 * HEAD_DIM), lambda b, r, n: (blk(b, r, n), 0))),
        compiler_params=_params(("parallel", "parallel", "arbitrary")),
        name=f"dilated{group}",
    )(qkv, qkv, qkv, qkv, qkv, bias)


def _unpermute_rows(a, dilation):
    if dilation == 1:
        return a
    t, c = a.shape
    tm = _perm_tile(dilation)
    return a.reshape(t // tm, dilation, tm // dilation, c).transpose(0, 2, 1, 3).reshape(t, c)


def _perm_matrix(rows, dilation):
    per = rows // dilation
    p = np.zeros((rows, rows), np.float32)
    for r in range(dilation):
        for n in range(per):
            p[n * dilation + r, r * per + n] = 1.0
    return jnp.asarray(p, BF16)


def _t5_bucket_np(n):
    max_exact = NUM_BUCKETS // 2
    nf = np.maximum(n, 1).astype(np.float32)
    large = max_exact + (np.log(nf / np.float32(max_exact)) / np.float32(math.log(MAX_DISTANCE / max_exact))
                         * np.float32(NUM_BUCKETS - max_exact)).astype(np.int32)
    large = np.minimum(large, NUM_BUCKETS - 1)
    return np.where(n < max_exact, n, large)


def _mix_tail(mix_in, wo_ref, x_ref, gm_ref, g_ref, sh_ref, sc_ref, wr_ref, br_ref,
              x_out, h_out, lg_out):
    mix = _dot(mix_in, wo_ref[...])
    x_new = x_ref[...] + gm_ref[0] * mix
    x_out[...] = x_new
    h = _norm_mod(x_new, g_ref[...], sh_ref[0], sc_ref[0])
    bits = pltpu.bitcast(h.astype(BF16).astype(F32), jnp.uint32)
    half = h.shape[1] // 2
    h_out[...] = (bits[:, :half] >> 16) | (bits[:, half:] & jnp.uint32(0xFFFF0000))
    lg_out[...] = jnp.dot(h, wr_ref[...], preferred_element_type=F32,
                          precision=lax.Precision.HIGHEST) + br_ref[...]


def _merge_oproj_kernel(o0_ref, o1_ref, o2_ref, p1_ref, p2_ref, lse_ref, ex_ref, wo_ref, x_ref, gm_ref, g_ref,
                        sh_ref, sc_ref, wr_ref, br_ref, x_out, h_out, lg_out):
    lse = lse_ref[...]
    l0, l1, l2 = lse[:, 0:16], lse[:, 16:32], lse[:, 32:48]
    m = jnp.maximum(jnp.maximum(l0, l1), l2)
    e0, e1, e2 = jnp.exp(l0 - m), jnp.exp(l1 - m), jnp.exp(l2 - m)
    inv = 1.0 / (e0 + e1 + e2)
    ex = ex_ref[...]
    o1 = _dot(p1_ref[...], o1_ref[...])
    o2 = _dot(p2_ref[...], o2_ref[0].reshape(ROW_TILE, o2_ref.shape[-1]))
    merged = (_dot((e0 * inv).astype(BF16), ex) * o0_ref[...].astype(F32)
              + _dot((e1 * inv).astype(BF16), ex) * o1
              + _dot((e2 * inv).astype(BF16), ex) * o2)
    _mix_tail(merged.astype(BF16), wo_ref, x_ref, gm_ref, g_ref, sh_ref, sc_ref, wr_ref, br_ref,
              x_out, h_out, lg_out)


def _oproj_kernel(o_ref, wo_ref, x_ref, gm_ref, g_ref, sh_ref, sc_ref, wr_ref, br_ref,
                  x_out, h_out, lg_out):
    _mix_tail(o_ref[...], wo_ref, x_ref, gm_ref, g_ref, sh_ref, sc_ref, wr_ref, br_ref,
              x_out, h_out, lg_out)


def _mixer_tail(mix_inputs, w_o, x, gate_m, g_ffn, shift_f, scale_f, w_r, b_r, seq, merged):
    t, d = x.shape
    tpb = seq // ROW_TILE
    row = lambda i: (i, 0)
    full = lambda i: (0, 0)
    per_b = lambda i: (i // tpb, 0, 0)
    tail_specs = [
        pl.BlockSpec((d, d), full),
        pl.BlockSpec((ROW_TILE, d), row),
        pl.BlockSpec((1, 1, d), per_b),
        pl.BlockSpec((1, d), full),
        pl.BlockSpec((1, 1, d), per_b),
        pl.BlockSpec((1, 1, d), per_b),
        pl.BlockSpec((d, N_EXPERTS), full),
        pl.BlockSpec((1, N_EXPERTS), full),
    ]
    tail_args = (w_o, x, gate_m, g_ffn.reshape(1, d), shift_f, scale_f, w_r, b_r.reshape(1, N_EXPERTS))
    if merged:
        o0, o1, o2, lse = mix_inputs
        expand = jnp.asarray(np.kron(np.eye(N_HEADS, dtype=np.float32),
                                     np.ones((1, HEAD_DIM), np.float32)), BF16)
        d1, d2 = DIL_PAIRS[1][1], DIL_PAIRS[2][1]
        assert _perm_tile(d1) == ROW_TILE
        t2 = _perm_tile(d2)
        sub2 = ROW_TILE // d2
        o2v = o2.reshape(t // t2, d2, t2 // d2, d)
        per2 = t2 // ROW_TILE
        kern = _merge_oproj_kernel
        specs = [pl.BlockSpec((ROW_TILE, d), row), pl.BlockSpec((ROW_TILE, d), row),
                 pl.BlockSpec((1, d2, sub2, d), lambda i: (i // per2, 0, i % per2, 0)),
                 pl.BlockSpec((ROW_TILE, ROW_TILE), full), pl.BlockSpec((ROW_TILE, ROW_TILE), full),
                 pl.BlockSpec((ROW_TILE, 3 * N_HEADS), row), pl.BlockSpec((N_HEADS, d), full)]
        args = (o0, o1, o2v, _perm_matrix(ROW_TILE, d1), _perm_matrix(ROW_TILE, d2), lse, expand)
    else:
        kern = _oproj_kernel
        specs = [pl.BlockSpec((ROW_TILE, d), row)]
        args = mix_inputs
    return pl.pallas_call(
        kern,
        out_shape=(jax.ShapeDtypeStruct((t, d), F32), jax.ShapeDtypeStruct((t, d // 2), jnp.uint32),
                   jax.ShapeDtypeStruct((t, N_EXPERTS), F32)),
        grid=(t // ROW_TILE,),
        in_specs=specs + tail_specs,
        out_specs=(pl.BlockSpec((ROW_TILE, d), row), pl.BlockSpec((ROW_TILE, d // 2), row),
                   pl.BlockSpec((ROW_TILE, N_EXPERTS), row)),
        compiler_params=_params(("parallel",)),
        name="mixer_tail_merge" if merged else "mixer_tail",
    )(*args, *tail_args)


def _fox_kernel(q_ref, k_ref, v_ref, o_ref, vt_scr):
    qi = pl.program_id(2)
    seq = k_ref.shape[0]
    nk = seq // FOX_TILE
    half = HEAD_DIM

    ext = FOX_SUM_ROWS
    @pl.when(qi == 0)
    def _():
        lane = lax.broadcasted_iota(jnp.int32, (FOX_TILE, 2 * half), 1)
        erow = lax.broadcasted_iota(jnp.int32, (2 * ext, 2 * FOX_TILE), 0)
        ecol = lax.broadcasted_iota(jnp.int32, (2 * ext, 2 * FOX_TILE), 1)
        sum_rows = jnp.where((erow < ext) == (ecol < FOX_TILE), 1.0, 0.0).astype(BF16)
        for j in range(nk):
            vj = v_ref[j * FOX_TILE:(j + 1) * FOX_TILE, :].astype(F32)
            vt_scr[j, :2 * half, :FOX_TILE] = jnp.where(lane < half, vj, 0.0).T.astype(BF16)
            vt_scr[j, :2 * half, FOX_TILE:] = jnp.where(lane >= half, vj, 0.0).T.astype(BF16)
            vt_scr[j, 2 * half:, :] = sum_rows

    q0 = q_ref[:, :AUG]
    q1 = q_ref[:, AUG:]

    def softmax_step(s, m):
        m_new = jnp.maximum(m, jnp.max(s, axis=0, keepdims=True))
        return m_new, jnp.exp2(m - m_new), jnp.exp2((s - m_new).astype(BF16))

    def step(kj, carry, masked):
        m0, m1, acc = carry
        rows = pl.ds(pl.multiple_of(kj * FOX_TILE, FOX_TILE), FOX_TILE)
        s0 = _dot_nt(k_ref[rows, :AUG], q0)
        s1 = _dot_nt(k_ref[rows, AUG:], q1)
        if masked:
            kpos = lax.broadcasted_iota(jnp.int32, (FOX_TILE, FOX_TILE), 0)
            qpos = lax.broadcasted_iota(jnp.int32, (FOX_TILE, FOX_TILE), 1)
            keep = kpos <= qpos
            s0 = jnp.where(keep, s0, NEG)
            s1 = jnp.where(keep, s1, NEG)
        m0, a0, p0 = softmax_step(s0, m0)
        m1, a1, p1 = softmax_step(s1, m1)
        pv = _dot(vt_scr[kj], jnp.concatenate([p0, p1], axis=0))
        alpha = jnp.concatenate([jnp.broadcast_to(a0, (half, FOX_TILE)), jnp.broadcast_to(a1, (half, FOX_TILE)),
                                 jnp.broadcast_to(a0, (ext, FOX_TILE)), jnp.broadcast_to(a1, (ext, FOX_TILE))],
                                axis=0)
        return m0, m1, alpha * acc + pv

    neg = jnp.full((1, FOX_TILE), NEG, F32)
    init = (neg, neg, jnp.zeros((2 * half + 2 * ext, FOX_TILE), F32))
    carry = lax.fori_loop(0, qi, lambda kj, c: step(kj, c, False), init)
    _, _, acc = step(qi, carry, True)
    inv0 = 1.0 / acc[2 * half:2 * half + 1, :]
    inv1 = 1.0 / acc[2 * half + ext:2 * half + ext + 1, :]
    inv = jnp.concatenate([jnp.broadcast_to(inv0, (half, FOX_TILE)),
                           jnp.broadcast_to(inv1, (half, FOX_TILE))], axis=0)
    o_ref[...] = (acc[:2 * half, :] * inv).T.astype(o_ref.dtype)


def _fox(q_aug, k_aug, v, n_batch, seq):
    t = q_aug.shape[0]
    nq = seq // FOX_TILE
    pairs = N_HEADS // 2
    return pl.pallas_call(
        _fox_kernel,
        out_shape=jax.ShapeDtypeStruct((t, N_HEADS * HEAD_DIM), BF16),
        grid=(n_batch, pairs, nq),
        in_specs=[
            pl.BlockSpec((FOX_TILE, 2 * AUG), lambda b, p, qi: (b * nq + qi, p)),
            pl.BlockSpec((seq, 2 * AUG), lambda b, p, qi: (b, p)),
            pl.BlockSpec((seq, 2 * HEAD_DIM), lambda b, p, qi: (b, p)),
        ],
        out_specs=pl.BlockSpec((FOX_TILE, 2 * HEAD_DIM), lambda b, p, qi: (b * nq + qi, p)),
        scratch_shapes=[pltpu.VMEM((nq, 2 * HEAD_DIM + 2 * FOX_SUM_ROWS, 2 * FOX_TILE), BF16)],
        compiler_params=_params(("parallel", "parallel", "arbitrary")),
        name="fox",
    )(q_aug, k_aug, v)


def _expert_kernel(be_ref, nu_ref, tok_ref, h_hbm, wgu_ref, bgu_ref, wd_ref, bd_ref, o_ref,
                   wgu_bf, wd_bf, xbuf_a, xbuf_b, gsem):
    i = pl.program_id(0)
    d_ff = wd_ref.shape[1]
    odd = lax.rem(i, 2) == 1
    pairs = MOE_TILE // 2
    bufs = ((xbuf_a, gsem.at[0]), (xbuf_b, gsem.at[1]))

    def issue_gather(blk, buf, sem):
        base = blk * pairs
        for c in range(pairs):
            word = tok_ref[base + c]
            for r, tok in ((2 * c, word & 0xFFFF), (2 * c + 1, lax.shift_right_logical(word, 16))):
                pltpu.make_async_copy(h_hbm.at[pl.ds(tok, 1), :], buf.at[pl.ds(r, 1), :], sem).start()

    def wait_gather(buf, sem):
        pltpu.make_async_copy(h_hbm.at[pl.ds(0, MOE_TILE), :], buf, sem).wait()

    @pl.when(i == 0)
    def _():
        issue_gather(0, *bufs[0])

    @pl.when(jnp.logical_or(i == 0, be_ref[i] != be_ref[jnp.maximum(i - 1, 0)]))
    def _():
        wgu_bf[...] = wgu_ref[0].astype(BF16)
        wd_bf[...] = wd_ref[0].astype(BF16)

    def block(cur, nxt):
        @pl.when(i <= nu_ref[0])
        def _():
            wait_gather(*cur)

        @pl.when(i < nu_ref[0])
        def _():
            issue_gather(i + 1, *nxt)
            words = cur[0][...]
            x = jnp.concatenate([pltpu.bitcast(words << 16, F32),
                                 pltpu.bitcast(words & jnp.uint32(0xFFFF0000), F32)], axis=1).astype(BF16)
            gu = _dot(x, wgu_bf[...]) + bgu_ref[0]
            g = jnp.minimum(gu[:, :d_ff], SWIGLU_LIMIT)
            u = jnp.clip(gu[:, d_ff:], -SWIGLU_LIMIT, SWIGLU_LIMIT)
            act = (u + 1.0) * g * (1.0 / (1.0 + jnp.exp(-SWIGLU_ALPHA * g)))
            y = _dot(act.astype(BF16), wd_bf[...]) + bd_ref[0]
            o_ref[...] = y.astype(o_ref.dtype)

    @pl.when(jnp.logical_not(odd))
    def _():
        block(bufs[0], bufs[1])

    @pl.when(odd)
    def _():
        block(bufs[1], bufs[0])

    @pl.when(i >= nu_ref[0])
    def _():
        o_ref[...] = jnp.zeros_like(o_ref)


def _experts(h, tok_rows, w_gu, b_gu, w_down, b_down, block_expert, n_used, layer):
    t = h.shape[0]
    d = w_gu.shape[2]
    n_rows = tok_rows.shape[0]
    depth, e, _, n_gu = w_gu.shape
    d_ff = w_down.shape[2]
    n_blocks = n_rows // MOE_TILE
    assert t <= 65536
    packed = tok_rows[0::2] | (tok_rows[1::2] << 16)
    grid_spec = pltpu.PrefetchScalarGridSpec(
        num_scalar_prefetch=3,
        grid=(n_blocks,),
        in_specs=[
            pl.BlockSpec(memory_space=pl.ANY),
            pl.BlockSpec((None, 1, d, n_gu), lambda i, be, nu, tk: (layer, be[i], 0, 0)),
            pl.BlockSpec((None, 1, 1, n_gu), lambda i, be, nu, tk: (layer, be[i], 0, 0)),
            pl.BlockSpec((None, 1, d_ff, d), lambda i, be, nu, tk: (layer, be[i], 0, 0)),
            pl.BlockSpec((None, 1, 1, d), lambda i, be, nu, tk: (layer, be[i], 0, 0)),
        ],
        out_specs=pl.BlockSpec((MOE_TILE, d), lambda i, be, nu, tk: (i, 0)),
        scratch_shapes=[pltpu.VMEM((d, n_gu), BF16), pltpu.VMEM((d_ff, d), BF16),
                        pltpu.VMEM((MOE_TILE, d // 2), jnp.uint32), pltpu.VMEM((MOE_TILE, d // 2), jnp.uint32),
                        pltpu.SemaphoreType.DMA((2,))],
    )
    return pl.pallas_call(
        _expert_kernel,
        out_shape=jax.ShapeDtypeStruct((n_rows, d), BF16),
        grid_spec=grid_spec,
        compiler_params=_params(("arbitrary",)),
        name="experts",
    )(block_expert, n_used, packed, h, w_gu, b_gu.reshape(depth, e, 1, n_gu), w_down,
      b_down.reshape(depth, e, 1, d))


def _router_kernel(lg_ref, tri_ref, upper_ref, dest_ref, gate_ref, cnt_ref, carry, base):
    phase = pl.program_id(0)
    i = pl.program_id(1)
    tm = lg_ref.shape[0]
    lane = lax.broadcasted_iota(jnp.int32, (tm, N_EXPERTS), 1).astype(F32)
    wide = lax.broadcasted_iota(jnp.int32, (tm, 128), 1)

    work = lg_ref[...]
    hots, vals = [], []
    for _ in range(TOP_K):
        m = jnp.max(work, axis=1, keepdims=True)
        idx = jnp.min(jnp.where(work == m, lane, float(N_EXPERTS)), axis=1, keepdims=True)
        hot = lane == idx
        hots.append(hot)
        vals.append(m)
        work = jnp.where(hot, -jnp.inf, work)
    exps = [jnp.exp(v - vals[0]) for v in vals]
    inv = 1.0 / (exps[0] + exps[1] + exps[2] + exps[3])
    gate_tile = jnp.zeros((tm, 128), F32)
    for k in range(TOP_K):
        gate_tile = jnp.where(wide == k, exps[k] * inv, gate_tile)
    gate_ref[...] = gate_tile

    chosen = jnp.zeros((tm, N_EXPERTS), F32)
    for hot in hots:
        chosen = chosen + hot.astype(F32)

    @pl.when(jnp.logical_and(phase == 0, i == 0))
    def _():
        carry[...] = jnp.zeros_like(carry)
        base[...] = jnp.zeros_like(base)

    @pl.when(jnp.logical_and(phase == 1, i == 0))
    def _():
        counts = carry[...]
        padded = jnp.floor((counts + (MOE_TILE - 1)) * (1.0 / MOE_TILE)) * MOE_TILE
        base[...] = jnp.dot(jnp.broadcast_to(padded, (8, N_EXPERTS)), upper_ref[...],
                            preferred_element_type=F32, precision=lax.Precision.HIGHEST)[0:1, :]
        carry[...] = jnp.zeros_like(carry)

    ahead = _dot(tri_ref[...], chosen.astype(BF16))
    pos = ahead + carry[...] + base[...]
    dest_tile = jnp.zeros((tm, 128), F32)
    for k in range(TOP_K):
        row = jnp.sum(jnp.where(hots[k], pos, 0.0), axis=1, keepdims=True)
        dest_tile = jnp.where(wide == k, row, dest_tile)
    dest_ref[...] = dest_tile.astype(jnp.int32)
    carry[...] = carry[...] + jnp.sum(chosen, axis=0, keepdims=True)
    cnt_ref[...] = carry[...]


def _router(logits):
    t = logits.shape[0]
    nt = t // ROW_TILE
    tri = jnp.asarray(np.tril(np.ones((ROW_TILE, ROW_TILE), np.float32), -1), BF16)
    upper = jnp.asarray(np.triu(np.ones((N_EXPERTS, N_EXPERTS), np.float32), 1), F32)
    return pl.pallas_call(
        _router_kernel,
        out_shape=(jax.ShapeDtypeStruct((t + ROW_TILE, 128), jnp.int32),
                   jax.ShapeDtypeStruct((t + ROW_TILE, 128), F32),
                   jax.ShapeDtypeStruct((1, N_EXPERTS), F32)),
        grid=(2, nt),
        in_specs=[
            pl.BlockSpec((ROW_TILE, N_EXPERTS), lambda p, i: (i, 0)),
            pl.BlockSpec((ROW_TILE, ROW_TILE), lambda p, i: (0, 0)),
            pl.BlockSpec((N_EXPERTS, N_EXPERTS), lambda p, i: (0, 0)),
        ],
        out_specs=(pl.BlockSpec((ROW_TILE, 128), lambda p, i: (p * i + (1 - p) * nt, 0)),
                   pl.BlockSpec((ROW_TILE, 128), lambda p, i: (p * i + (1 - p) * nt, 0)),
                   pl.BlockSpec((1, N_EXPERTS), lambda p, i: (0, 0))),
        scratch_shapes=[pltpu.VMEM((1, N_EXPERTS), F32), pltpu.VMEM((1, N_EXPERTS), F32)],
        compiler_params=_params(("arbitrary", "arbitrary")),
        name="router",
    )(logits, tri, upper)


def _route(logits):
    t = logits.shape[0]
    n_slots = t * TOP_K
    n_rows = n_slots + N_EXPERTS * MOE_TILE
    n_blocks = n_rows // MOE_TILE
    dest_w, gates_w, counts = _router(logits)
    counts = counts.reshape(N_EXPERTS).astype(jnp.int32)
    pad_ends = jnp.cumsum((counts + MOE_TILE - 1) // MOE_TILE * MOE_TILE)
    block_start = jnp.arange(n_blocks, dtype=jnp.int32) * MOE_TILE
    block_expert = jnp.minimum(jnp.sum(block_start[:, None] >= pad_ends[None, :], axis=1),
                               N_EXPERTS - 1).astype(jnp.int32)
    n_used = (pad_ends[-1] // MOE_TILE).astype(jnp.int32).reshape(1)
    dest = dest_w[:t, :TOP_K].T.reshape(-1)
    tok = jnp.tile(jnp.arange(t, dtype=jnp.int32), TOP_K)
    tok_rows = (jnp.arange(n_rows, dtype=jnp.int32) % t).at[dest].set(tok, unique_indices=True)
    return dest, tok_rows, gates_w, block_expert, n_used


def _combine_kernel(x_ref, y0_ref, y1_ref, y2_ref, y3_ref, gt_ref, gf_ref, g_ref, o_ref, *, final):
    gt = gt_ref[...]
    moe = (gt[:, 0:1] * y0_ref[...].astype(F32) + gt[:, 1:2] * y1_ref[...].astype(F32)
           + gt[:, 2:3] * y2_ref[...].astype(F32) + gt[:, 3:4] * y3_ref[...].astype(F32))
    x_new = x_ref[...] + gf_ref[0] * moe
    if final:
        x_new = _norm_mod(x_new, g_ref[...], None, None)
    o_ref[...] = x_new


def _combine(x, y_slots, gates, gate_f, g_final, seq, final):
    t, d = x.shape
    tpb = seq // ROW_TILE
    nt = t // ROW_TILE
    assert TOP_K == 4
    return pl.pallas_call(
        functools.partial(_combine_kernel, final=final),
        out_shape=jax.ShapeDtypeStruct((t, d), F32),
        grid=(nt,),
        in_specs=[pl.BlockSpec((ROW_TILE, d), lambda i: (i, 0))] + [
            pl.BlockSpec((ROW_TILE, d), functools.partial(lambda i, k: (k * nt + i, 0), k=k))
            for k in range(TOP_K)] + [
            pl.BlockSpec((ROW_TILE, 128), lambda i: (i, 0)),
            pl.BlockSpec((1, 1, d), lambda i: (i // tpb, 0, 0)),
            pl.BlockSpec((1, d), lambda i: (0, 0)),
        ],
        out_specs=pl.BlockSpec((ROW_TILE, d), lambda i: (i, 0)),
        compiler_params=_params(("parallel",)),
        name="combine_final" if final else "combine",
    )(x, y_slots, y_slots, y_slots, y_slots, gates, gate_f, g_final.reshape(1, d))


def _moe(x, h, logits, gate_f, w_gu, b_gu, w_down, b_down, g_final, seq, layer, final):
    t, d = x.shape
    dest, tok_rows, gates, block_expert, n_used = _route(logits)
    ybuf = _experts(h, tok_rows, w_gu, b_gu, w_down, b_down, block_expert, n_used, layer)
    y_slots = jnp.take(ybuf, dest, axis=0, mode="clip")
    return _combine(x, y_slots, gates, gate_f, g_final, seq, final)


def kernel(x, c, ada_w, ada_b, norm_mix_g, norm_ffn_g, a_w_qkv, a_w_o, rel_bias, kv_norm_g, w_kvf, b_f,
           b_w_q, b_w_o, router_w, router_b, w_gu, b_gu, w_down, b_down, final_norm_g):
    n_batch, seq, d = x.shape
    t = n_batch * seq
    width = N_HEADS * HEAD_DIM
    xf = x.reshape(t, d)

    ada = _ada(c, ada_w, ada_b)
    mods = [[ada[l, :, i * d:(i + 1) * d].reshape(n_batch, 1, d) for i in range(6)] for l in range(2)]

    shift_m, scale_m, gate_m, shift_f, scale_f, gate_f = mods[0]
    qscale = np.ones((3, 3, 1), np.float32)
    qscale[:, 0] = HEAD_DIM ** -0.5
    w_qkv = (a_w_qkv[0].reshape(d, 3, 3, width) * qscale).astype(BF16)

    qi = np.arange(DIL_BLOCK, dtype=np.int32)[:, None]
    kj = np.arange(2 * DIL_BLOCK, dtype=np.int32)[None, :]
    delta = qi + DIL_BLOCK - kj
    in_band = (delta >= 0) & (delta <= DIL_BLOCK)
    outs, lses = [], []
    for g, (window, dilation) in enumerate(DIL_PAIRS):
        assert window // dilation == DIL_BLOCK
        qkv = _proj(xf, norm_mix_g[0], shift_m, scale_m, w_qkv[:, g].reshape(d, 3 * width), seq, dilation, g)
        bucket = _t5_bucket_np(np.clip(delta, 0, None) * dilation)
        tab = rel_bias[:, g * N_HEADS:(g + 1) * N_HEADS].astype(F32)
        onehot = jnp.asarray(np.eye(NUM_BUCKETS, dtype=np.float32)[bucket])
        bias = jnp.einsum('ijb,bh->hij', onehot, tab, precision=lax.Precision.HIGHEST)
        bias = jnp.where(in_band, bias, NEG)
        bias_first = jnp.where(kj >= DIL_BLOCK, bias, NEG)
        o, lse = _dilated_group(qkv, jnp.stack([bias_first, bias]), g, dilation, n_batch, seq)
        outs.append(o)
        lses.append(_unpermute_rows(lse, dilation)[:, :N_HEADS])
    lse_all = jnp.concatenate(lses, axis=1)
    x1, h1, logits1 = _mixer_tail((outs[0], outs[1], outs[2], lse_all), a_w_o[0].astype(BF16), xf, gate_m,
                                  norm_ffn_g[0], shift_f, scale_f, router_w[0], router_b[0], seq, True)
    x2 = _moe(x1, h1, logits1, gate_f, w_gu, b_gu, w_down, b_down, final_norm_g, seq, 0, False)

    w_f = jnp.pad(jnp.tile(w_kvf[:, 2 * width:], (1, 3)), ((0, 0), (0, 128 - 3 * N_HEADS))).astype(BF16)
    b_fp = jnp.pad(jnp.tile(b_f, 3), (0, 128 - 3 * N_HEADS)).reshape(1, 128)
    k_aug, v_sh, fcum = _kvf(x2, kv_norm_g, _aug_weight(w_kvf[:, :width]), w_kvf[:, width:2 * width].astype(BF16),
                             w_f, b_fp, n_batch)

    shift_m, scale_m, gate_m, shift_f, scale_f, gate_f = mods[1]
    q_aug = _qaug(x2, norm_mix_g[1], shift_m, scale_m, _aug_weight(b_w_q[0] * (HEAD_DIM ** -0.5 * LOG2E)), fcum, seq)
    o1 = _fox(q_aug, k_aug, v_sh, n_batch, seq)

    x3, h3, logits3 = _mixer_tail((o1,), b_w_o[0].astype(BF16), x2, gate_m, norm_ffn_g[1], shift_f, scale_f,
                                  router_w[1], router_b[1], seq, False)
    out = _moe(x3, h3, logits3, gate_f, w_gu, b_gu, w_down, b_down, final_norm_g, seq, 1, True)
    return out.reshape(n_batch, seq, d)
```

```python
import functools
import math

import numpy as np
import jax
import jax.numpy as jnp
from jax import lax
from jax.experimental import pallas as pl
from jax.experimental.pallas import tpu as pltpu

F32 = jnp.float32
BF16 = jnp.bfloat16

D_MODEL = 1024
HEAD_DIM = 64
N_HEADS = 16
DIL_PAIRS = ((128, 1), (512, 4), (2048, 16))
DIL_BLOCK = 128
NUM_BUCKETS = 32
MAX_DISTANCE = 2048
N_EXPERTS = 32
TOP_K = 4
SWIGLU_LIMIT = 7.0
SWIGLU_ALPHA = 1.702
RMS_EPS = 1e-6
NEG = -1e30
LOG2E = math.log2(math.e)

ROW_TILE = 512
MOE_TILE = 512
FOX_TILE = 512
AUG = 128
FOX_SUM_ROWS = 8
VMEM_LIMIT = 56 * 1024 * 1024


def _params(sem, vmem=VMEM_LIMIT):
    return pltpu.CompilerParams(dimension_semantics=sem, vmem_limit_bytes=vmem)


def _dot(a, b):
    return jnp.dot(a, b, preferred_element_type=F32)


def _dot_nt(a, b):
    return lax.dot_general(a, b, (((1,), (1,)), ((), ())), preferred_element_type=F32)


def _dot_tn(a, b):
    return lax.dot_general(a, b, (((0,), (0,)), ((), ())), preferred_element_type=F32)


def _norm_mod(x, g, shift, scale):
    ms = jnp.mean(x * x, axis=-1, keepdims=True)
    y = x * lax.rsqrt(ms + RMS_EPS) * g
    if scale is not None:
        y = y * (1.0 + scale) + shift
    return y


def _ada_kernel(c_ref, w_ref, b_ref, o_ref):
    c = c_ref[...]
    act = c * (1.0 / (1.0 + jnp.exp(-c)))
    o_ref[0] = jnp.dot(act, w_ref[0], preferred_element_type=F32,
                       precision=lax.Precision.HIGHEST) + b_ref[0]


def _ada(c, ada_w, ada_b):
    depth, d, n = ada_w.shape
    bsz = c.shape[0]
    tn = 1536
    return pl.pallas_call(
        _ada_kernel,
        out_shape=jax.ShapeDtypeStruct((depth, bsz, n), F32),
        grid=(depth, n // tn),
        in_specs=[
            pl.BlockSpec((bsz, d), lambda l, j: (0, 0)),
            pl.BlockSpec((1, d, tn), lambda l, j: (l, 0, j)),
            pl.BlockSpec((1, 1, tn), lambda l, j: (l, 0, j)),
        ],
        out_specs=pl.BlockSpec((1, bsz, tn), lambda l, j: (l, 0, j)),
        compiler_params=_params(("parallel", "parallel")),
        name="ada",
    )(c, ada_w, ada_b.reshape(depth, 1, n))


def _perm_tile(dilation):
    return max(ROW_TILE, DIL_BLOCK * dilation)


def _proj_kernel(x_ref, g_ref, sh_ref, sc_ref, w_ref, o_ref, h_scr, *xs_scr, dilation):
    @pl.when(pl.program_id(1) == 0)
    def _():
        if dilation == 1:
            h_scr[...] = _norm_mod(x_ref[...], g_ref[...], sh_ref[0], sc_ref[0]).astype(BF16)
        else:
            (xs,) = xs_scr
            n_lane = xs.shape[0]
            for c in range(n_lane):
                xs[c] = x_ref[:, c * 128:(c + 1) * 128]
            chunk = x_ref.shape[0] // dilation
            for r in range(dilation):
                xr = jnp.concatenate([xs[c, pl.ds(r, chunk, stride=dilation), :] for c in range(n_lane)], axis=1)
                h_scr[r * chunk:(r + 1) * chunk, :] = _norm_mod(xr, g_ref[...], sh_ref[0], sc_ref[0]).astype(BF16)

    o_ref[...] = _dot(h_scr[...], w_ref[...]).astype(o_ref.dtype)


def _proj(x, g, shift, scale, w, seq, dilation, group):
    t, d = x.shape
    n = w.shape[1]
    tm = _perm_tile(dilation)
    tn = d
    tpb = seq // tm
    return pl.pallas_call(
        functools.partial(_proj_kernel, dilation=dilation),
        out_shape=jax.ShapeDtypeStruct((t, n), BF16),
        grid=(t // tm, n // tn),
        in_specs=[
            pl.BlockSpec((tm, d), lambda i, j: (i, 0)),
            pl.BlockSpec((1, d), lambda i, j: (0, 0)),
            pl.BlockSpec((1, 1, d), lambda i, j: (i // tpb, 0, 0)),
            pl.BlockSpec((1, 1, d), lambda i, j: (i // tpb, 0, 0)),
            pl.BlockSpec((d, tn), lambda i, j: (0, j)),
        ],
        out_specs=pl.BlockSpec((tm, tn), lambda i, j: (i, j)),
        scratch_shapes=[pltpu.VMEM((tm, d), BF16)] + (
            [pltpu.VMEM((d // 128, tm, 128), F32)] if dilation > 1 else []),
        compiler_params=_params(("parallel", "arbitrary")),
        name=f"proj{group}",
    )(x, g.reshape(1, d), shift, scale, w)


def _split3(f):
    def top(v):
        return lax.bitcast_convert_type(lax.bitcast_convert_type(v, jnp.uint32) & jnp.uint32(0xFFFF0000), F32)
    hi = top(f)
    r1 = f - hi
    mid = top(r1)
    return hi, mid, r1 - mid


def _forget_pieces(fc):
    hi, mid, lo = _split3(fc)
    lane = lax.broadcasted_iota(jnp.int32, fc.shape, 1)
    x = jnp.where(lane < N_HEADS, hi, jnp.where(lane < 2 * N_HEADS, mid, jnp.where(lane < 3 * N_HEADS, lo, 0.0)))
    return x.astype(BF16)


def _aug_tables(key_side):
    place = np.zeros((128, N_HEADS * AUG), np.float32)
    const = np.zeros((1, N_HEADS * AUG), np.float32)
    for h in range(N_HEADS):
        base = h * AUG + HEAD_DIM
        for piece in range(3):
            if key_side:
                place[piece * N_HEADS + h, base + piece] = 1.0
                const[0, base + 3 + piece] = 1.0
            else:
                place[piece * N_HEADS + h, base + 3 + piece] = 1.0
                const[0, base + piece] = -1.0
    return jnp.asarray(place, BF16), jnp.asarray(const, F32)


def _aug_weight(w):
    d = w.shape[0]
    w = jnp.pad(w.reshape(d, N_HEADS, HEAD_DIM), ((0, 0), (0, 0), (0, AUG - HEAD_DIM)))
    return w.reshape(d, N_HEADS * AUG).astype(BF16)


def _kvf_kernel(x_ref, g_ref, wk_ref, wv_ref, wf_ref, bf_ref, tri_ref, pl_ref, cr_ref,
                k_ref, v_ref, f_ref, carry, *, tpb):
    h = _norm_mod(x_ref[...], g_ref[...], None, None).astype(BF16)
    v_ref[...] = _dot(h, wv_ref[...]).astype(v_ref.dtype)
    z = _dot(h, wf_ref[...]) + bf_ref[...]
    lf = jnp.minimum(z, 0.0) - jnp.log(1.0 + jnp.exp(-jnp.abs(z)))
    hi, mid, lo = _split3(lf)
    tri = tri_ref[...]
    cs = _dot(tri, hi.astype(BF16)) + _dot(tri, mid.astype(BF16)) + _dot(tri, lo.astype(BF16))

    @pl.when(pl.program_id(0) % tpb == 0)
    def _():
        carry[...] = jnp.zeros_like(carry)

    cs = cs + carry[...]
    carry[...] = cs[ROW_TILE - 1:ROW_TILE, :]
    f2 = cs * LOG2E
    f_ref[...] = f2
    k_ref[...] = (_dot(h, wk_ref[...]) + _dot(_forget_pieces(f2), pl_ref[...]) + cr_ref[...]).astype(k_ref.dtype)


def _kvf(x, g, w_k, w_v, w_f, b_f, n_batch):
    t, d = x.shape
    n_aug = N_HEADS * AUG
    tri = jnp.asarray(np.tril(np.ones((ROW_TILE, ROW_TILE), np.float32)), BF16)
    place, const = _aug_tables(True)
    assert (t // ROW_TILE) % n_batch == 0
    full = lambda i: (0, 0)
    row = lambda i: (i, 0)
    return pl.pallas_call(
        functools.partial(_kvf_kernel, tpb=t // ROW_TILE // n_batch),
        out_shape=(jax.ShapeDtypeStruct((t, n_aug), BF16), jax.ShapeDtypeStruct((t, d), BF16),
                   jax.ShapeDtypeStruct((t, 128), F32)),
        grid=(t // ROW_TILE,),
        in_specs=[
            pl.BlockSpec((ROW_TILE, d), row),
            pl.BlockSpec((1, d), full),
            pl.BlockSpec((d, n_aug), full),
            pl.BlockSpec((d, d), full),
            pl.BlockSpec((d, 128), full),
            pl.BlockSpec((1, 128), full),
            pl.BlockSpec((ROW_TILE, ROW_TILE), full),
            pl.BlockSpec((128, n_aug), full),
            pl.BlockSpec((1, n_aug), full),
        ],
        out_specs=(pl.BlockSpec((ROW_TILE, n_aug), row), pl.BlockSpec((ROW_TILE, d), row),
                   pl.BlockSpec((ROW_TILE, 128), row)),
        scratch_shapes=[pltpu.VMEM((1, 128), F32)],
        compiler_params=_params(("arbitrary",)),
        name="kvf",
    )(x, g.reshape(1, d), w_k, w_v, w_f, b_f, tri, place, const)


def _qaug_kernel(x_ref, g_ref, sh_ref, sc_ref, w_ref, f_ref, pl_ref, cr_ref, o_ref):
    h = _norm_mod(x_ref[...], g_ref[...], sh_ref[0], sc_ref[0]).astype(BF16)
    o_ref[...] = (_dot(h, w_ref[...]) + _dot(_forget_pieces(f_ref[...]), pl_ref[...])
                  + cr_ref[...]).astype(o_ref.dtype)


def _qaug(x, g, shift, scale, w, fcum, seq):
    t, d = x.shape
    n_aug = N_HEADS * AUG
    tpb = seq // ROW_TILE
    place, const = _aug_tables(False)
    full = lambda i: (0, 0)
    row = lambda i: (i, 0)
    per_b = lambda i: (i // tpb, 0, 0)
    return pl.pallas_call(
        _qaug_kernel,
        out_shape=jax.ShapeDtypeStruct((t, n_aug), BF16),
        grid=(t // ROW_TILE,),
        in_specs=[
            pl.BlockSpec((ROW_TILE, d), row),
            pl.BlockSpec((1, d), full),
            pl.BlockSpec((1, 1, d), per_b),
            pl.BlockSpec((1, 1, d), per_b),
            pl.BlockSpec((d, n_aug), full),
            pl.BlockSpec((ROW_TILE, 128), row),
            pl.BlockSpec((128, n_aug), full),
            pl.BlockSpec((1, n_aug), full),
        ],
        out_specs=pl.BlockSpec((ROW_TILE, n_aug), row),
        compiler_params=_params(("parallel",)),
        name="qaug",
    )(x, g.reshape(1, d), shift, scale, w, fcum, place, const)


def _dil_kernel(q_ref, kp_ref, kc_ref, vp_ref, vc_ref, bias_ref, o_ref, lse_ref):
    table = jnp.minimum(pl.program_id(2), 1)
    pair_w = 2 * HEAD_DIM
    lane = lax.broadcasted_iota(jnp.int32, (DIL_BLOCK, pair_w), 1)
    first = lane < HEAD_DIM
    first2 = lax.broadcasted_iota(jnp.int32, (2 * DIL_BLOCK, pair_w), 1) < HEAD_DIM
    lse_tile = jnp.zeros((DIL_BLOCK, pair_w), F32)
    zero = jnp.zeros((), BF16)
    n_pairs = N_HEADS // 2
    slices = [slice(pair * pair_w, (pair + 1) * pair_w) for pair in range(n_pairs)]
    scores = []
    for pair, sl in enumerate(slices):
        q = q_ref[:, sl]
        k2 = jnp.concatenate([kp_ref[:, sl], kc_ref[:, sl]], axis=0)
        for which in range(2):
            qh = jnp.where(first, q, zero) if which == 0 else jnp.where(first, zero, q)
            scores.append(_dot_nt(qh, k2) + bias_ref[table, 2 * pair + which])
    probs, invs = [], []
    for h, s in enumerate(scores):
        m = jnp.max(s, axis=-1, keepdims=True)
        p = jnp.exp(s - m)
        den = jnp.sum(p, axis=-1, keepdims=True)
        probs.append(p.astype(BF16))
        invs.append(1.0 / den)
        lse_tile = jnp.where(lane == h, m + jnp.log(den), lse_tile)
    for pair, sl in enumerate(slices):
        v2 = jnp.concatenate([vp_ref[:, sl], vc_ref[:, sl]], axis=0)
        vcat = jnp.concatenate([jnp.where(first2, v2, zero), jnp.where(first2, zero, v2)], axis=0)
        o = _dot(jnp.concatenate(probs[2 * pair:2 * pair + 2], axis=1), vcat)
        o_ref[:, sl] = (o * jnp.where(first, invs[2 * pair], invs[2 * pair + 1])).astype(o_ref.dtype)
    lse_ref[...] = lse_tile


def _dilated_group(qkv, bias, group, dilation, n_batch, seq):
    t = qkv.shape[0]
    width = N_HEADS * HEAD_DIM
    nb = seq // dilation // DIL_BLOCK
    bpb = seq // DIL_BLOCK

    def blk(b, r, n):
        return b * bpb + n * dilation + r

    def spec(part, prev):
        if prev:
            return pl.BlockSpec((DIL_BLOCK, width), lambda b, r, n: (blk(b, r, jnp.maximum(n - 1, 0)), part))
        return pl.BlockSpec((DIL_BLOCK, width), lambda b, r, n: (blk(b, r, n), part))

    return pl.pallas_call(
        _dil_kernel,
        out_shape=(jax.ShapeDtypeStruct((t, width), BF16), jax.ShapeDtypeStruct((t, 2 * HEAD_DIM), F32)),
        grid=(n_batch, dilation, nb),
        in_specs=[spec(0, False), spec(1, True), spec(1, False), spec(2, True), spec(2, False),
                  pl.BlockSpec((2, N_HEADS, DIL_BLOCK, 2 * DIL_BLOCK), lambda b, r, n: (0, 0, 0, 0))],
        out_specs=(pl.BlockSpec((DIL_BLOCK, width), lambda b, r, n: (blk(b, r, n), 0)),
                   pl.BlockSpec((DIL_BLOCK, 2 * HEAD_DIM), lambda b, r, n: (blk(b, r, n), 0))),
        compiler_params=_params(("parallel", "parallel", "arbitrary")),
        name=f"dilated{group}",
    )(qkv, qkv, qkv, qkv, qkv, bias)


def _unpermute_rows(a, dilation):
    if dilation == 1:
        return a
    t, c = a.shape
    tm = _perm_tile(dilation)
    return a.reshape(t // tm, dilation, tm // dilation, c).transpose(0, 2, 1, 3).reshape(t, c)


def _perm_matrix(rows, dilation):
    per = rows // dilation
    p = np.zeros((rows, rows), np.float32)
    for r in range(dilation):
        for n in range(per):
            p[n * dilation + r, r * per + n] = 1.0
    return jnp.asarray(p, BF16)


def _t5_bucket_np(n):
    max_exact = NUM_BUCKETS // 2
    nf = np.maximum(n, 1).astype(np.float32)
    large = max_exact + (np.log(nf / np.float32(max_exact)) / np.float32(math.log(MAX_DISTANCE / max_exact))
                         * np.float32(NUM_BUCKETS - max_exact)).astype(np.int32)
    large = np.minimum(large, NUM_BUCKETS - 1)
    return np.where(n < max_exact, n, large)


def _mix_tail(mix_in, wo_ref, x_ref, gm_ref, g_ref, sh_ref, sc_ref, wr_ref, br_ref,
              x_out, h_out, lg_out):
    mix = _dot(mix_in, wo_ref[...])
    x_new = x_ref[...] + gm_ref[0] * mix
    x_out[...] = x_new
    h = _norm_mod(x_new, g_ref[...], sh_ref[0], sc_ref[0])
    bits = pltpu.bitcast(h.astype(BF16).astype(F32), jnp.uint32)
    half = h.shape[1] // 2
    h_out[...] = (bits[:, :half] >> 16) | (bits[:, half:] & jnp.uint32(0xFFFF0000))
    lg_out[...] = jnp.dot(h, wr_ref[...], preferred_element_type=F32,
                          precision=lax.Precision.HIGHEST) + br_ref[...]


def _merge_oproj_kernel(o0_ref, o1_ref, o2_ref, p1_ref, p2_ref, lse_ref, ex_ref, wo_ref, x_ref, gm_ref, g_ref,
                        sh_ref, sc_ref, wr_ref, br_ref, x_out, h_out, lg_out):
    lse = lse_ref[...]
    l0, l1, l2 = lse[:, 0:16], lse[:, 16:32], lse[:, 32:48]
    m = jnp.maximum(jnp.maximum(l0, l1), l2)
    e0, e1, e2 = jnp.exp(l0 - m), jnp.exp(l1 - m), jnp.exp(l2 - m)
    inv = 1.0 / (e0 + e1 + e2)
    ex = ex_ref[...]
    o1 = _dot(p1_ref[...], o1_ref[...])
    o2 = _dot(p2_ref[...], o2_ref[0].reshape(ROW_TILE, o2_ref.shape[-1]))
    merged = (_dot((e0 * inv).astype(BF16), ex) * o0_ref[...].astype(F32)
              + _dot((e1 * inv).astype(BF16), ex) * o1
              + _dot((e2 * inv).astype(BF16), ex) * o2)
    _mix_tail(merged.astype(BF16), wo_ref, x_ref, gm_ref, g_ref, sh_ref, sc_ref, wr_ref, br_ref,
              x_out, h_out, lg_out)


def _oproj_kernel(o_ref, wo_ref, x_ref, gm_ref, g_ref, sh_ref, sc_ref, wr_ref, br_ref,
                  x_out, h_out, lg_out):
    _mix_tail(o_ref[...], wo_ref, x_ref, gm_ref, g_ref, sh_ref, sc_ref, wr_ref, br_ref,
              x_out, h_out, lg_out)


def _mixer_tail(mix_inputs, w_o, x, gate_m, g_ffn, shift_f, scale_f, w_r, b_r, seq, merged):
    t, d = x.shape
    tpb = seq // ROW_TILE
    row = lambda i: (i, 0)
    full = lambda i: (0, 0)
    per_b = lambda i: (i // tpb, 0, 0)
    tail_specs = [
        pl.BlockSpec((d, d), full),
        pl.BlockSpec((ROW_TILE, d), row),
        pl.BlockSpec((1, 1, d), per_b),
        pl.BlockSpec((1, d), full),
        pl.BlockSpec((1, 1, d), per_b),
        pl.BlockSpec((1, 1, d), per_b),
        pl.BlockSpec((d, N_EXPERTS), full),
        pl.BlockSpec((1, N_EXPERTS), full),
    ]
    tail_args = (w_o, x, gate_m, g_ffn.reshape(1, d), shift_f, scale_f, w_r, b_r.reshape(1, N_EXPERTS))
    if merged:
        o0, o1, o2, lse = mix_inputs
        expand = jnp.asarray(np.kron(np.eye(N_HEADS, dtype=np.float32),
                                     np.ones((1, HEAD_DIM), np.float32)), BF16)
        d1, d2 = DIL_PAIRS[1][1], DIL_PAIRS[2][1]
        assert _perm_tile(d1) == ROW_TILE
        t2 = _perm_tile(d2)
        sub2 = ROW_TILE // d2
        o2v = o2.reshape(t // t2, d2, t2 // d2, d)
        per2 = t2 // ROW_TILE
        kern = _merge_oproj_kernel
        specs = [pl.BlockSpec((ROW_TILE, d), row), pl.BlockSpec((ROW_TILE, d), row),
                 pl.BlockSpec((1, d2, sub2, d), lambda i: (i // per2, 0, i % per2, 0)),
                 pl.BlockSpec((ROW_TILE, ROW_TILE), full), pl.BlockSpec((ROW_TILE, ROW_TILE), full),
                 pl.BlockSpec((ROW_TILE, 3 * N_HEADS), row), pl.BlockSpec((N_HEADS, d), full)]
        args = (o0, o1, o2v, _perm_matrix(ROW_TILE, d1), _perm_matrix(ROW_TILE, d2), lse, expand)
    else:
        kern = _oproj_kernel
        specs = [pl.BlockSpec((ROW_TILE, d), row)]
        args = mix_inputs
    return pl.pallas_call(
        kern,
        out_shape=(jax.ShapeDtypeStruct((t, d), F32), jax.ShapeDtypeStruct((t, d // 2), jnp.uint32),
                   jax.ShapeDtypeStruct((t, N_EXPERTS), F32)),
        grid=(t // ROW_TILE,),
        in_specs=specs + tail_specs,
        out_specs=(pl.BlockSpec((ROW_TILE, d), row), pl.BlockSpec((ROW_TILE, d // 2), row),
                   pl.BlockSpec((ROW_TILE, N_EXPERTS), row)),
        compiler_params=_params(("parallel",)),
        name="mixer_tail_merge" if merged else "mixer_tail",
    )(*args, *tail_args)


def _fox_kernel(q_ref, k_ref, v_ref, o_ref, vt_scr):
    qi = pl.program_id(2)
    seq = k_ref.shape[0]
    nk = seq // FOX_TILE
    half = HEAD_DIM

    ext = FOX_SUM_ROWS
    @pl.when(qi == 0)
    def _():
        lane = lax.broadcasted_iota(jnp.int32, (FOX_TILE, 2 * half), 1)
        erow = lax.broadcasted_iota(jnp.int32, (2 * ext, 2 * FOX_TILE), 0)
        ecol = lax.broadcasted_iota(jnp.int32, (2 * ext, 2 * FOX_TILE), 1)
        sum_rows = jnp.where((erow < ext) == (ecol < FOX_TILE), 1.0, 0.0).astype(BF16)
        for j in range(nk):
            vj = v_ref[j * FOX_TILE:(j + 1) * FOX_TILE, :].astype(F32)
            vt_scr[j, :2 * half, :FOX_TILE] = jnp.where(lane < half, vj, 0.0).T.astype(BF16)
            vt_scr[j, :2 * half, FOX_TILE:] = jnp.where(lane >= half, vj, 0.0).T.astype(BF16)
            vt_scr[j, 2 * half:, :] = sum_rows

    q0 = q_ref[:, :AUG]
    q1 = q_ref[:, AUG:]

    def softmax_step(s, m):
        m_new = jnp.maximum(m, jnp.max(s, axis=0, keepdims=True))
        return m_new, jnp.exp2(m - m_new), jnp.exp2((s - m_new).astype(BF16))

    def step(kj, carry, masked):
        m0, m1, acc = carry
        rows = pl.ds(pl.multiple_of(kj * FOX_TILE, FOX_TILE), FOX_TILE)
        s0 = _dot_nt(k_ref[rows, :AUG], q0)
        s1 = _dot_nt(k_ref[rows, AUG:], q1)
        if masked:
            kpos = lax.broadcasted_iota(jnp.int32, (FOX_TILE, FOX_TILE), 0)
            qpos = lax.broadcasted_iota(jnp.int32, (FOX_TILE, FOX_TILE), 1)
            keep = kpos <= qpos
            s0 = jnp.where(keep, s0, NEG)
            s1 = jnp.where(keep, s1, NEG)
        m0, a0, p0 = softmax_step(s0, m0)
        m1, a1, p1 = softmax_step(s1, m1)
        pv = _dot(vt_scr[kj], jnp.concatenate([p0, p1], axis=0))
        alpha = jnp.concatenate([jnp.broadcast_to(a0, (half, FOX_TILE)), jnp.broadcast_to(a1, (half, FOX_TILE)),
                                 jnp.broadcast_to(a0, (ext, FOX_TILE)), jnp.broadcast_to(a1, (ext, FOX_TILE))],
                                axis=0)
        return m0, m1, alpha * acc + pv

    neg = jnp.full((1, FOX_TILE), NEG, F32)
    init = (neg, neg, jnp.zeros((2 * half + 2 * ext, FOX_TILE), F32))
    carry = lax.fori_loop(0, qi, lambda kj, c: step(kj, c, False), init)
    _, _, acc = step(qi, carry, True)
    inv0 = 1.0 / acc[2 * half:2 * half + 1, :]
    inv1 = 1.0 / acc[2 * half + ext:2 * half + ext + 1, :]
    inv = jnp.concatenate([jnp.broadcast_to(inv0, (half, FOX_TILE)),
                           jnp.broadcast_to(inv1, (half, FOX_TILE))], axis=0)
    o_ref[...] = (acc[:2 * half, :] * inv).T.astype(o_ref.dtype)


def _fox(q_aug, k_aug, v, n_batch, seq):
    t = q_aug.shape[0]
    nq = seq // FOX_TILE
    pairs = N_HEADS // 2
    return pl.pallas_call(
        _fox_kernel,
        out_shape=jax.ShapeDtypeStruct((t, N_HEADS * HEAD_DIM), BF16),
        grid=(n_batch, pairs, nq),
        in_specs=[
            pl.BlockSpec((FOX_TILE, 2 * AUG), lambda b, p, qi: (b * nq + qi, p)),
            pl.BlockSpec((seq, 2 * AUG), lambda b, p, qi: (b, p)),
            pl.BlockSpec((seq, 2 * HEAD_DIM), lambda b, p, qi: (b, p)),
        ],
        out_specs=pl.BlockSpec((FOX_TILE, 2 * HEAD_DIM), lambda b, p, qi: (b * nq + qi, p)),
        scratch_shapes=[pltpu.VMEM((nq, 2 * HEAD_DIM + 2 * FOX_SUM_ROWS, 2 * FOX_TILE), BF16)],
        compiler_params=_params(("parallel", "parallel", "arbitrary")),
        name="fox",
    )(q_aug, k_aug, v)


def _expert_kernel(be_ref, nu_ref, tok_ref, h_hbm, wgu_ref, bgu_ref, wd_ref, bd_ref, o_ref,
                   wgu_bf, wd_bf, xbuf_a, xbuf_b, gsem):
    i = pl.program_id(0)
    d_ff = wd_ref.shape[1]
    odd = lax.rem(i, 2) == 1
    pairs = MOE_TILE // 2
    bufs = ((xbuf_a, gsem.at[0]), (xbuf_b, gsem.at[1]))

    def issue_gather(blk, buf, sem):
        base = blk * pairs
        for c in range(pairs):
            word = tok_ref[base + c]
            for r, tok in ((2 * c, word & 0xFFFF), (2 * c + 1, lax.shift_right_logical(word, 16))):
                pltpu.async_copy(h_hbm.at[pl.ds(tok, 1), :], buf.at[pl.ds(r, 1), :], sem, priority=r % 2)

    def wait_gather(buf, sem):
        pltpu.make_async_copy(h_hbm.at[pl.ds(0, MOE_TILE), :], buf, sem).wait()

    @pl.when(i == 0)
    def _():
        issue_gather(0, *bufs[0])

    @pl.when(jnp.logical_or(i == 0, be_ref[i] != be_ref[jnp.maximum(i - 1, 0)]))
    def _():
        wgu_bf[...] = wgu_ref[0].astype(BF16)
        wd_bf[...] = wd_ref[0].astype(BF16)

    def block(cur, nxt):
        @pl.when(i <= nu_ref[0])
        def _():
            wait_gather(*cur)

        @pl.when(i < nu_ref[0])
        def _():
            issue_gather(i + 1, *nxt)
            words = cur[0][...]
            x = jnp.concatenate([pltpu.bitcast(words << 16, F32),
                                 pltpu.bitcast(words & jnp.uint32(0xFFFF0000), F32)], axis=1).astype(BF16)
            gu = _dot(x, wgu_bf[...]) + bgu_ref[0]
            g = jnp.minimum(gu[:, :d_ff], SWIGLU_LIMIT)
            u = jnp.clip(gu[:, d_ff:], -SWIGLU_LIMIT, SWIGLU_LIMIT)
            act = (u + 1.0) * g * (1.0 / (1.0 + jnp.exp(-SWIGLU_ALPHA * g)))
            y = _dot(act.astype(BF16), wd_bf[...]) + bd_ref[0]
            o_ref[...] = y.astype(o_ref.dtype)

    @pl.when(jnp.logical_not(odd))
    def _():
        block(bufs[0], bufs[1])

    @pl.when(odd)
    def _():
        block(bufs[1], bufs[0])

    @pl.when(i >= nu_ref[0])
    def _():
        o_ref[...] = jnp.zeros_like(o_ref)


def _experts(h, tok_rows, w_gu, b_gu, w_down, b_down, block_expert, n_used, layer):
    t = h.shape[0]
    d = w_gu.shape[2]
    n_rows = tok_rows.shape[0]
    depth, e, _, n_gu = w_gu.shape
    d_ff = w_down.shape[2]
    n_blocks = n_rows // MOE_TILE
    assert t <= 65536
    packed = tok_rows[0::2] | (tok_rows[1::2] << 16)
    grid_spec = pltpu.PrefetchScalarGridSpec(
        num_scalar_prefetch=3,
        grid=(n_blocks,),
        in_specs=[
            pl.BlockSpec(memory_space=pl.ANY),
            pl.BlockSpec((None, 1, d, n_gu), lambda i, be, nu, tk: (layer, be[i], 0, 0)),
            pl.BlockSpec((None, 1, 1, n_gu), lambda i, be, nu, tk: (layer, be[i], 0, 0)),
            pl.BlockSpec((None, 1, d_ff, d), lambda i, be, nu, tk: (layer, be[i], 0, 0)),
            pl.BlockSpec((None, 1, 1, d), lambda i, be, nu, tk: (layer, be[i], 0, 0)),
        ],
        out_specs=pl.BlockSpec((MOE_TILE, d), lambda i, be, nu, tk: (i, 0)),
        scratch_shapes=[pltpu.VMEM((d, n_gu), BF16), pltpu.VMEM((d_ff, d), BF16),
                        pltpu.VMEM((MOE_TILE, d // 2), jnp.uint32), pltpu.VMEM((MOE_TILE, d // 2), jnp.uint32),
                        pltpu.SemaphoreType.DMA((2,))],
    )
    return pl.pallas_call(
        _expert_kernel,
        out_shape=jax.ShapeDtypeStruct((n_rows, d), BF16),
        grid_spec=grid_spec,
        compiler_params=_params(("arbitrary",)),
        name="experts",
    )(block_expert, n_used, packed, h, w_gu, b_gu.reshape(depth, e, 1, n_gu), w_down,
      b_down.reshape(depth, e, 1, d))


def _router_kernel(lg_ref, tri_ref, upper_ref, dest_ref, gate_ref, cnt_ref, carry, base):
    phase = pl.program_id(0)
    i = pl.program_id(1)
    tm = lg_ref.shape[0]
    lane = lax.broadcasted_iota(jnp.int32, (tm, N_EXPERTS), 1).astype(F32)
    wide = lax.broadcasted_iota(jnp.int32, (tm, 128), 1)

    work = lg_ref[...]
    hots, vals = [], []
    for _ in range(TOP_K):
        m = jnp.max(work, axis=1, keepdims=True)
        idx = jnp.min(jnp.where(work == m, lane, float(N_EXPERTS)), axis=1, keepdims=True)
        hot = lane == idx
        hots.append(hot)
        vals.append(m)
        work = jnp.where(hot, -jnp.inf, work)
    exps = [jnp.exp(v - vals[0]) for v in vals]
    inv = 1.0 / (exps[0] + exps[1] + exps[2] + exps[3])
    gate_tile = jnp.zeros((tm, 128), F32)
    for k in range(TOP_K):
        gate_tile = jnp.where(wide == k, exps[k] * inv, gate_tile)
    gate_ref[...] = gate_tile

    chosen = jnp.zeros((tm, N_EXPERTS), F32)
    for hot in hots:
        chosen = chosen + hot.astype(F32)

    @pl.when(jnp.logical_and(phase == 0, i == 0))
    def _():
        carry[...] = jnp.zeros_like(carry)
        base[...] = jnp.zeros_like(base)

    @pl.when(jnp.logical_and(phase == 1, i == 0))
    def _():
        counts = carry[...]
        padded = jnp.floor((counts + (MOE_TILE - 1)) * (1.0 / MOE_TILE)) * MOE_TILE
        base[...] = jnp.dot(jnp.broadcast_to(padded, (8, N_EXPERTS)), upper_ref[...],
                            preferred_element_type=F32, precision=lax.Precision.HIGHEST)[0:1, :]
        carry[...] = jnp.zeros_like(carry)

    ahead = _dot(tri_ref[...], chosen.astype(BF16))
    pos = ahead + carry[...] + base[...]
    dest_tile = jnp.zeros((tm, 128), F32)
    for k in range(TOP_K):
        row = jnp.sum(jnp.where(hots[k], pos, 0.0), axis=1, keepdims=True)
        dest_tile = jnp.where(wide == k, row, dest_tile)
    dest_ref[...] = dest_tile.astype(jnp.int32)
    carry[...] = carry[...] + jnp.sum(chosen, axis=0, keepdims=True)
    cnt_ref[...] = carry[...]


def _router(logits):
    t = logits.shape[0]
    nt = t // ROW_TILE
    tri = jnp.asarray(np.tril(np.ones((ROW_TILE, ROW_TILE), np.float32), -1), BF16)
    upper = jnp.asarray(np.triu(np.ones((N_EXPERTS, N_EXPERTS), np.float32), 1), F32)
    return pl.pallas_call(
        _router_kernel,
        out_shape=(jax.ShapeDtypeStruct((t + ROW_TILE, 128), jnp.int32),
                   jax.ShapeDtypeStruct((t + ROW_TILE, 128), F32),
                   jax.ShapeDtypeStruct((1, N_EXPERTS), F32)),
        grid=(2, nt),
        in_specs=[
            pl.BlockSpec((ROW_TILE, N_EXPERTS), lambda p, i: (i, 0)),
            pl.BlockSpec((ROW_TILE, ROW_TILE), lambda p, i: (0, 0)),
            pl.BlockSpec((N_EXPERTS, N_EXPERTS), lambda p, i: (0, 0)),
        ],
        out_specs=(pl.BlockSpec((ROW_TILE, 128), lambda p, i: (p * i + (1 - p) * nt, 0)),
                   pl.BlockSpec((ROW_TILE, 128), lambda p, i: (p * i + (1 - p) * nt, 0)),
                   pl.BlockSpec((1, N_EXPERTS), lambda p, i: (0, 0))),
        scratch_shapes=[pltpu.VMEM((1, N_EXPERTS), F32), pltpu.VMEM((1, N_EXPERTS), F32)],
        compiler_params=_params(("arbitrary", "arbitrary")),
        name="router",
    )(logits, tri, upper)


def _route(logits):
    t = logits.shape[0]
    n_slots = t * TOP_K
    n_rows = n_slots + N_EXPERTS * MOE_TILE
    n_blocks = n_rows // MOE_TILE
    dest_w, gates_w, counts = _router(logits)
    counts = counts.reshape(N_EXPERTS).astype(jnp.int32)
    pad_ends = jnp.cumsum((counts + MOE_TILE - 1) // MOE_TILE * MOE_TILE)
    block_start = jnp.arange(n_blocks, dtype=jnp.int32) * MOE_TILE
    block_expert = jnp.minimum(jnp.sum(block_start[:, None] >= pad_ends[None, :], axis=1),
                               N_EXPERTS - 1).astype(jnp.int32)
    n_used = (pad_ends[-1] // MOE_TILE).astype(jnp.int32).reshape(1)
    dest = dest_w[:t, :TOP_K].T.reshape(-1)
    tok = jnp.tile(jnp.arange(t, dtype=jnp.int32), TOP_K)
    tok_rows = (jnp.arange(n_rows, dtype=jnp.int32) % t).at[dest].set(tok, unique_indices=True)
    return dest, tok_rows, gates_w, block_expert, n_used


def _combine_kernel(x_ref, y0_ref, y1_ref, y2_ref, y3_ref, gt_ref, gf_ref, g_ref, o_ref, *, final):
    gt = gt_ref[...]
    moe = (gt[:, 0:1] * y0_ref[...].astype(F32) + gt[:, 1:2] * y1_ref[...].astype(F32)
           + gt[:, 2:3] * y2_ref[...].astype(F32) + gt[:, 3:4] * y3_ref[...].astype(F32))
    x_new = x_ref[...] + gf_ref[0] * moe
    if final:
        x_new = _norm_mod(x_new, g_ref[...], None, None)
    o_ref[...] = x_new


def _combine(x, y_slots, gates, gate_f, g_final, seq, final):
    t, d = x.shape
    tpb = seq // ROW_TILE
    nt = t // ROW_TILE
    assert TOP_K == 4
    return pl.pallas_call(
        functools.partial(_combine_kernel, final=final),
        out_shape=jax.ShapeDtypeStruct((t, d), F32),
        grid=(nt,),
        in_specs=[pl.BlockSpec((ROW_TILE, d), lambda i: (i, 0))] + [
            pl.BlockSpec((ROW_TILE, d), functools.partial(lambda i, k: (k * nt + i, 0), k=k))
            for k in range(TOP_K)] + [
            pl.BlockSpec((ROW_TILE, 128), lambda i: (i, 0)),
            pl.BlockSpec((1, 1, d), lambda i: (i // tpb, 0, 0)),
            pl.BlockSpec((1, d), lambda i: (0, 0)),
        ],
        out_specs=pl.BlockSpec((ROW_TILE, d), lambda i: (i, 0)),
        compiler_params=_params(("parallel",)),
        name="combine_final" if final else "combine",
    )(x, y_slots, y_slots, y_slots, y_slots, gates, gate_f, g_final.reshape(1, d))


def _moe(x, h, logits, gate_f, w_gu, b_gu, w_down, b_down, g_final, seq, layer, final):
    t, d = x.shape
    dest, tok_rows, gates, block_expert, n_used = _route(logits)
    ybuf = _experts(h, tok_rows, w_gu, b_gu, w_down, b_down, block_expert, n_used, layer)
    y_slots = jnp.take(ybuf, dest, axis=0, mode="clip")
    return _combine(x, y_slots, gates, gate_f, g_final, seq, final)


def kernel(x, c, ada_w, ada_b, norm_mix_g, norm_ffn_g, a_w_qkv, a_w_o, rel_bias, kv_norm_g, w_kvf, b_f,
           b_w_q, b_w_o, router_w, router_b, w_gu, b_gu, w_down, b_down, final_norm_g):
    n_batch, seq, d = x.shape
    t = n_batch * seq
    width = N_HEADS * HEAD_DIM
    xf = x.reshape(t, d)

    ada = _ada(c, ada_w, ada_b)
    mods = [[ada[l, :, i * d:(i + 1) * d].reshape(n_batch, 1, d) for i in range(6)] for l in range(2)]

    shift_m, scale_m, gate_m, shift_f, scale_f, gate_f = mods[0]
    qscale = np.ones((3, 3, 1), np.float32)
    qscale[:, 0] = HEAD_DIM ** -0.5
    w_qkv = (a_w_qkv[0].reshape(d, 3, 3, width) * qscale).astype(BF16)

    qi = np.arange(DIL_BLOCK, dtype=np.int32)[:, None]
    kj = np.arange(2 * DIL_BLOCK, dtype=np.int32)[None, :]
    delta = qi + DIL_BLOCK - kj
    in_band = (delta >= 0) & (delta <= DIL_BLOCK)
    outs, lses = [], []
    for g, (window, dilation) in enumerate(DIL_PAIRS):
        assert window // dilation == DIL_BLOCK
        qkv = _proj(xf, norm_mix_g[0], shift_m, scale_m, w_qkv[:, g].reshape(d, 3 * width), seq, dilation, g)
        bucket = _t5_bucket_np(np.clip(delta, 0, None) * dilation)
        tab = rel_bias[:, g * N_HEADS:(g + 1) * N_HEADS].astype(F32)
        onehot = jnp.asarray(np.eye(NUM_BUCKETS, dtype=np.float32)[bucket])
        bias = jnp.einsum('ijb,bh->hij', onehot, tab, precision=lax.Precision.HIGHEST)
        bias = jnp.where(in_band, bias, NEG)
        bias_first = jnp.where(kj >= DIL_BLOCK, bias, NEG)
        o, lse = _dilated_group(qkv, jnp.stack([bias_first, bias]), g, dilation, n_batch, seq)
        outs.append(o)
        lses.append(_unpermute_rows(lse, dilation)[:, :N_HEADS])
    lse_all = jnp.concatenate(lses, axis=1)
    x1, h1, logits1 = _mixer_tail((outs[0], outs[1], outs[2], lse_all), a_w_o[0].astype(BF16), xf, gate_m,
                                  norm_ffn_g[0], shift_f, scale_f, router_w[0], router_b[0], seq, True)
    x2 = _moe(x1, h1, logits1, gate_f, w_gu, b_gu, w_down, b_down, final_norm_g, seq, 0, False)

    w_f = jnp.pad(jnp.tile(w_kvf[:, 2 * width:], (1, 3)), ((0, 0), (0, 128 - 3 * N_HEADS))).astype(BF16)
    b_fp = jnp.pad(jnp.tile(b_f, 3), (0, 128 - 3 * N_HEADS)).reshape(1, 128)
    k_aug, v_sh, fcum = _kvf(x2, kv_norm_g, _aug_weight(w_kvf[:, :width]), w_kvf[:, width:2 * width].astype(BF16),
                             w_f, b_fp, n_batch)

    shift_m, scale_m, gate_m, shift_f, scale_f, gate_f = mods[1]
    q_aug = _qaug(x2, norm_mix_g[1], shift_m, scale_m, _aug_weight(b_w_q[0] * (HEAD_DIM ** -0.5 * LOG2E)), fcum, seq)
    o1 = _fox(q_aug, k_aug, v_sh, n_batch, seq)

    x3, h3, logits3 = _mixer_tail((o1,), b_w_o[0].astype(BF16), x2, gate_m, norm_ffn_g[1], shift_f, scale_f,
                                  router_w[1], router_b[1], seq, False)
    out = _moe(x3, h3, logits3, gate_f, w_gu, b_gu, w_down, b_down, final_norm_g, seq, 1, True)
    return out.reshape(n_batch, seq, d)
```

```python
import functools
import math

import numpy as np
import jax
import jax.numpy as jnp
from jax import lax
from jax.experimental import pallas as pl
from jax.experimental.pallas import tpu as pltpu

F32 = jnp.float32
BF16 = jnp.bfloat16

D_MODEL = 1024
HEAD_DIM = 64
N_HEADS = 16
DIL_PAIRS = ((128, 1), (512, 4), (2048, 16))
DIL_BLOCK = 128
NUM_BUCKETS = 32
MAX_DISTANCE = 2048
N_EXPERTS = 32
TOP_K = 4
SWIGLU_LIMIT = 7.0
SWIGLU_ALPHA = 1.702
RMS_EPS = 1e-6
NEG = -1e30
LOG2E = math.log2(math.e)

ROW_TILE = 512
MOE_TILE = 512
FOX_TILE = 512
AUG = 128
FOX_SUM_ROWS = 8
VMEM_LIMIT = 56 * 1024 * 1024


def _params(sem, vmem=VMEM_LIMIT):
    return pltpu.CompilerParams(dimension_semantics=sem, vmem_limit_bytes=vmem)


def _dot(a, b):
    return jnp.dot(a, b, preferred_element_type=F32)


def _dot_nt(a, b):
    return lax.dot_general(a, b, (((1,), (1,)), ((), ())), preferred_element_type=F32)


def _dot_tn(a, b):
    return lax.dot_general(a, b, (((0,), (0,)), ((), ())), preferred_element_type=F32)


def _norm_mod(x, g, shift, scale):
    ms = jnp.mean(x * x, axis=-1, keepdims=True)
    y = x * lax.rsqrt(ms + RMS_EPS) * g
    if scale is not None:
        y = y * (1.0 + scale) + shift
    return y


def _ada_kernel(c_ref, w_ref, b_ref, o_ref):
    c = c_ref[...]
    act = c * (1.0 / (1.0 + jnp.exp(-c)))
    o_ref[0] = jnp.dot(act, w_ref[0], preferred_element_type=F32,
                       precision=lax.Precision.HIGHEST) + b_ref[0]


def _ada(c, ada_w, ada_b):
    depth, d, n = ada_w.shape
    bsz = c.shape[0]
    tn = 1536
    return pl.pallas_call(
        _ada_kernel,
        out_shape=jax.ShapeDtypeStruct((depth, bsz, n), F32),
        grid=(depth, n // tn),
        in_specs=[
            pl.BlockSpec((bsz, d), lambda l, j: (0, 0)),
            pl.BlockSpec((1, d, tn), lambda l, j: (l, 0, j)),
            pl.BlockSpec((1, 1, tn), lambda l, j: (l, 0, j)),
        ],
        out_specs=pl.BlockSpec((1, bsz, tn), lambda l, j: (l, 0, j)),
        compiler_params=_params(("parallel", "parallel")),
        name="ada",
    )(c, ada_w, ada_b.reshape(depth, 1, n))


def _perm_tile(dilation):
    return max(ROW_TILE, DIL_BLOCK * dilation)


def _proj_kernel(x_ref, g_ref, sh_ref, sc_ref, w_ref, o_ref, h_scr, *xs_scr, dilation):
    @pl.when(pl.program_id(1) == 0)
    def _():
        if dilation == 1:
            h_scr[...] = _norm_mod(x_ref[...], g_ref[...], sh_ref[0], sc_ref[0]).astype(BF16)
        else:
            (xs,) = xs_scr
            n_lane = xs.shape[0]
            for c in range(n_lane):
                xs[c] = x_ref[:, c * 128:(c + 1) * 128]
            chunk = x_ref.shape[0] // dilation
            for r in range(dilation):
                xr = jnp.concatenate([xs[c, pl.ds(r, chunk, stride=dilation), :] for c in range(n_lane)], axis=1)
                h_scr[r * chunk:(r + 1) * chunk, :] = _norm_mod(xr, g_ref[...], sh_ref[0], sc_ref[0]).astype(BF16)

    o_ref[...] = _dot(h_scr[...], w_ref[...]).astype(o_ref.dtype)


def _proj(x, g, shift, scale, w, seq, dilation, group):
    t, d = x.shape
    n = w.shape[1]
    tm = _perm_tile(dilation)
    tn = d
    tpb = seq // tm
    return pl.pallas_call(
        functools.partial(_proj_kernel, dilation=dilation),
        out_shape=jax.ShapeDtypeStruct((t, n), BF16),
        grid=(t // tm, n // tn),
        in_specs=[
            pl.BlockSpec((tm, d), lambda i, j: (i, 0)),
            pl.BlockSpec((1, d), lambda i, j: (0, 0)),
            pl.BlockSpec((1, 1, d), lambda i, j: (i // tpb, 0, 0)),
            pl.BlockSpec((1, 1, d), lambda i, j: (i // tpb, 0, 0)),
            pl.BlockSpec((d, tn), lambda i, j: (0, j)),
        ],
        out_specs=pl.BlockSpec((tm, tn), lambda i, j: (i, j)),
        scratch_shapes=[pltpu.VMEM((tm, d), BF16)] + (
            [pltpu.VMEM((d // 128, tm, 128), F32)] if dilation > 1 else []),
        compiler_params=_params(("parallel", "arbitrary")),
        name=f"proj{group}",
    )(x, g.reshape(1, d), shift, scale, w)


def _split3(f):
    def top(v):
        return lax.bitcast_convert_type(lax.bitcast_convert_type(v, jnp.uint32) & jnp.uint32(0xFFFF0000), F32)
    hi = top(f)
    r1 = f - hi
    mid = top(r1)
    return hi, mid, r1 - mid


def _forget_pieces(fc):
    hi, mid, lo = _split3(fc)
    lane = lax.broadcasted_iota(jnp.int32, fc.shape, 1)
    x = jnp.where(lane < N_HEADS, hi, jnp.where(lane < 2 * N_HEADS, mid, jnp.where(lane < 3 * N_HEADS, lo, 0.0)))
    return x.astype(BF16)


def _aug_tables(key_side):
    place = np.zeros((128, N_HEADS * AUG), np.float32)
    const = np.zeros((1, N_HEADS * AUG), np.float32)
    for h in range(N_HEADS):
        base = h * AUG + HEAD_DIM
        for piece in range(3):
            if key_side:
                place[piece * N_HEADS + h, base + piece] = 1.0
                const[0, base + 3 + piece] = 1.0
            else:
                place[piece * N_HEADS + h, base + 3 + piece] = 1.0
                const[0, base + piece] = -1.0
    return jnp.asarray(place, BF16), jnp.asarray(const, F32)


def _aug_weight(w):
    d = w.shape[0]
    w = jnp.pad(w.reshape(d, N_HEADS, HEAD_DIM), ((0, 0), (0, 0), (0, AUG - HEAD_DIM)))
    return w.reshape(d, N_HEADS * AUG).astype(BF16)


def _kvf_kernel(x_ref, g_ref, wk_ref, wv_ref, wf_ref, bf_ref, tri_ref, pl_ref, cr_ref,
                k_ref, v_ref, f_ref, carry, *, tpb):
    h = _norm_mod(x_ref[...], g_ref[...], None, None).astype(BF16)
    v_ref[...] = _dot(h, wv_ref[...]).astype(v_ref.dtype)
    z = _dot(h, wf_ref[...]) + bf_ref[...]
    lf = jnp.minimum(z, 0.0) - jnp.log(1.0 + jnp.exp(-jnp.abs(z)))
    hi, mid, lo = _split3(lf)
    tri = tri_ref[...]
    cs = _dot(tri, hi.astype(BF16)) + _dot(tri, mid.astype(BF16)) + _dot(tri, lo.astype(BF16))

    @pl.when(pl.program_id(0) % tpb == 0)
    def _():
        carry[...] = jnp.zeros_like(carry)

    cs = cs + carry[...]
    carry[...] = cs[ROW_TILE - 1:ROW_TILE, :]
    f2 = cs * LOG2E
    f_ref[...] = f2
    k_ref[...] = (_dot(h, wk_ref[...]) + _dot(_forget_pieces(f2), pl_ref[...]) + cr_ref[...]).astype(k_ref.dtype)


def _kvf(x, g, w_k, w_v, w_f, b_f, n_batch):
    t, d = x.shape
    n_aug = N_HEADS * AUG
    tri = jnp.asarray(np.tril(np.ones((ROW_TILE, ROW_TILE), np.float32)), BF16)
    place, const = _aug_tables(True)
    assert (t // ROW_TILE) % n_batch == 0
    full = lambda i: (0, 0)
    row = lambda i: (i, 0)
    return pl.pallas_call(
        functools.partial(_kvf_kernel, tpb=t // ROW_TILE // n_batch),
        out_shape=(jax.ShapeDtypeStruct((t, n_aug), BF16), jax.ShapeDtypeStruct((t, d), BF16),
                   jax.ShapeDtypeStruct((t, 128), F32)),
        grid=(t // ROW_TILE,),
        in_specs=[
            pl.BlockSpec((ROW_TILE, d), row),
            pl.BlockSpec((1, d), full),
            pl.BlockSpec((d, n_aug), full),
            pl.BlockSpec((d, d), full),
            pl.BlockSpec((d, 128), full),
            pl.BlockSpec((1, 128), full),
            pl.BlockSpec((ROW_TILE, ROW_TILE), full),
            pl.BlockSpec((128, n_aug), full),
            pl.BlockSpec((1, n_aug), full),
        ],
        out_specs=(pl.BlockSpec((ROW_TILE, n_aug), row), pl.BlockSpec((ROW_TILE, d), row),
                   pl.BlockSpec((ROW_TILE, 128), row)),
        scratch_shapes=[pltpu.VMEM((1, 128), F32)],
        compiler_params=_params(("arbitrary",)),
        name="kvf",
    )(x, g.reshape(1, d), w_k, w_v, w_f, b_f, tri, place, const)


def _qaug_kernel(x_ref, g_ref, sh_ref, sc_ref, w_ref, f_ref, pl_ref, cr_ref, o_ref):
    h = _norm_mod(x_ref[...], g_ref[...], sh_ref[0], sc_ref[0]).astype(BF16)
    o_ref[...] = (_dot(h, w_ref[...]) + _dot(_forget_pieces(f_ref[...]), pl_ref[...])
                  + cr_ref[...]).astype(o_ref.dtype)


def _qaug(x, g, shift, scale, w, fcum, seq):
    t, d = x.shape
    n_aug = N_HEADS * AUG
    tpb = seq // ROW_TILE
    place, const = _aug_tables(False)
    full = lambda i: (0, 0)
    row = lambda i: (i, 0)
    per_b = lambda i: (i // tpb, 0, 0)
    return pl.pallas_call(
        _qaug_kernel,
        out_shape=jax.ShapeDtypeStruct((t, n_aug), BF16),
        grid=(t // ROW_TILE,),
        in_specs=[
            pl.BlockSpec((ROW_TILE, d), row),
            pl.BlockSpec((1, d), full),
            pl.BlockSpec((1, 1, d), per_b),
            pl.BlockSpec((1, 1, d), per_b),
            pl.BlockSpec((d, n_aug), full),
            pl.BlockSpec((ROW_TILE, 128), row),
            pl.BlockSpec((128, n_aug), full),
            pl.BlockSpec((1, n_aug), full),
        ],
        out_specs=pl.BlockSpec((ROW_TILE, n_aug), row),
        compiler_params=_params(("parallel",)),
        name="qaug",
    )(x, g.reshape(1, d), shift, scale, w, fcum, place, const)


def _dil_kernel(q_ref, kp_ref, kc_ref, vp_ref, vc_ref, bias_ref, o_ref, lse_ref):
    table = jnp.minimum(pl.program_id(2), 1)
    pair_w = 2 * HEAD_DIM
    lane = lax.broadcasted_iota(jnp.int32, (DIL_BLOCK, pair_w), 1)
    first = lane < HEAD_DIM
    first2 = lax.broadcasted_iota(jnp.int32, (2 * DIL_BLOCK, pair_w), 1) < HEAD_DIM
    lse_tile = jnp.zeros((DIL_BLOCK, pair_w), F32)
    zero = jnp.zeros((), BF16)
    n_pairs = N_HEADS // 2
    slices = [slice(pair * pair_w, (pair + 1) * pair_w) for pair in range(n_pairs)]
    scores = []
    for pair, sl in enumerate(slices):
        q = q_ref[:, sl]
        k2 = jnp.concatenate([kp_ref[:, sl], kc_ref[:, sl]], axis=0)
        for which in range(2):
            qh = jnp.where(first, q, zero) if which == 0 else jnp.where(first, zero, q)
            scores.append(_dot_nt(qh, k2) + bias_ref[table, 2 * pair + which])
    probs, invs = [], []
    for h, s in enumerate(scores):
        m = jnp.max(s, axis=-1, keepdims=True)
        p = jnp.exp(s - m)
        den = jnp.sum(p, axis=-1, keepdims=True)
        probs.append(p.astype(BF16))
        invs.append(1.0 / den)
        lse_tile = jnp.where(lane == h, m + jnp.log(den), lse_tile)
    for pair, sl in enumerate(slices):
        v2 = jnp.concatenate([vp_ref[:, sl], vc_ref[:, sl]], axis=0)
        vcat = jnp.concatenate([jnp.where(first2, v2, zero), jnp.where(first2, zero, v2)], axis=0)
        o = _dot(jnp.concatenate(probs[2 * pair:2 * pair + 2], axis=1), vcat)
        o_ref[:, sl] = (o * jnp.where(first, invs[2 * pair], invs[2 * pair + 1])).astype(o_ref.dtype)
    lse_ref[...] = lse_tile


def _dilated_group(qkv, bias, group, dilation, n_batch, seq):
    t = qkv.shape[0]
    width = N_HEADS * HEAD_DIM
    nb = seq // dilation // DIL_BLOCK
    bpb = seq // DIL_BLOCK

    def blk(b, r, n):
        return b * bpb + n * dilation + r

    def spec(part, prev):
        if prev:
            return pl.BlockSpec((DIL_BLOCK, width), lambda b, r, n: (blk(b, r, jnp.maximum(n - 1, 0)), part))
        return pl.BlockSpec((DIL_BLOCK, width), lambda b, r, n: (blk(b, r, n), part))

    return pl.pallas_call(
        _dil_kernel,
        out_shape=(jax.ShapeDtypeStruct((t, width), BF16), jax.ShapeDtypeStruct((t, 2 * HEAD_DIM), F32)),
        grid=(n_batch, dilation, nb),
        in_specs=[spec(0, False), spec(1, True), spec(1, False), spec(2, True), spec(2, False),
                  pl.BlockSpec((2, N_HEADS, DIL_BLOCK, 2 * DIL_BLOCK), lambda b, r, n: (0, 0, 0, 0))],
        out_specs=(pl.BlockSpec((DIL_BLOCK, width), lambda b, r, n: (blk(b, r, n), 0)),
                   pl.BlockSpec((DIL_BLOCK, 2 * HEAD_DIM), lambda b, r, n: (blk(b, r, n), 0))),
        compiler_params=_params(("parallel", "parallel", "arbitrary")),
        name=f"dilated{group}",
    )(qkv, qkv, qkv, qkv, qkv, bias)


def _unpermute_rows(a, dilation):
    if dilation == 1:
        return a
    t, c = a.shape
    tm = _perm_tile(dilation)
    return a.reshape(t // tm, dilation, tm // dilation, c).transpose(0, 2, 1, 3).reshape(t, c)


def _perm_matrix(rows, dilation):
    per = rows // dilation
    p = np.zeros((rows, rows), np.float32)
    for r in range(dilation):
        for n in range(per):
            p[n * dilation + r, r * per + n] = 1.0
    return jnp.asarray(p, BF16)


def _t5_bucket_np(n):
    max_exact = NUM_BUCKETS // 2
    nf = np.maximum(n, 1).astype(np.float32)
    large = max_exact + (np.log(nf / np.float32(max_exact)) / np.float32(math.log(MAX_DISTANCE / max_exact))
                         * np.float32(NUM_BUCKETS - max_exact)).astype(np.int32)
    large = np.minimum(large, NUM_BUCKETS - 1)
    return np.where(n < max_exact, n, large)


def _mix_tail(mix_in, wo_ref, x_ref, gm_ref, g_ref, sh_ref, sc_ref, wr_ref, br_ref,
              x_out, h_out, lg_out):
    mix = _dot(mix_in, wo_ref[...])
    x_new = x_ref[...] + gm_ref[0] * mix
    x_out[...] = x_new
    h = _norm_mod(x_new, g_ref[...], sh_ref[0], sc_ref[0])
    bits = pltpu.bitcast(h.astype(BF16).astype(F32), jnp.uint32)
    half = h.shape[1] // 2
    h_out[...] = (bits[:, :half] >> 16) | (bits[:, half:] & jnp.uint32(0xFFFF0000))
    h_hi, h_mid, _ = _split3(h)
    h_hi = h_hi.astype(BF16)
    lg_out[...] = (_dot(h_hi, wr_ref[0]) + (_dot(h_mid.astype(BF16), wr_ref[0]) + _dot(h_hi, wr_ref[1]))
                   + br_ref[...])


def _merge_oproj_kernel(o0_ref, o1_ref, o2_ref, p1_ref, p2_ref, lse_ref, ex_ref, wo_ref, x_ref, gm_ref, g_ref,
                        sh_ref, sc_ref, wr_ref, br_ref, x_out, h_out, lg_out):
    lse = lse_ref[...]
    l0, l1, l2 = lse[:, 0:16], lse[:, 16:32], lse[:, 32:48]
    m = jnp.maximum(jnp.maximum(l0, l1), l2)
    e0, e1, e2 = jnp.exp(l0 - m), jnp.exp(l1 - m), jnp.exp(l2 - m)
    inv = 1.0 / (e0 + e1 + e2)
    ex = ex_ref[...]
    o1 = _dot(p1_ref[...], o1_ref[...])
    o2 = _dot(p2_ref[...], o2_ref[0].reshape(ROW_TILE, o2_ref.shape[-1]))
    merged = (_dot((e0 * inv).astype(BF16), ex) * o0_ref[...].astype(F32)
              + _dot((e1 * inv).astype(BF16), ex) * o1
              + _dot((e2 * inv).astype(BF16), ex) * o2)
    _mix_tail(merged.astype(BF16), wo_ref, x_ref, gm_ref, g_ref, sh_ref, sc_ref, wr_ref, br_ref,
              x_out, h_out, lg_out)


def _oproj_kernel(o_ref, wo_ref, x_ref, gm_ref, g_ref, sh_ref, sc_ref, wr_ref, br_ref,
                  x_out, h_out, lg_out):
    _mix_tail(o_ref[...], wo_ref, x_ref, gm_ref, g_ref, sh_ref, sc_ref, wr_ref, br_ref,
              x_out, h_out, lg_out)


def _mixer_tail(mix_inputs, w_o, x, gate_m, g_ffn, shift_f, scale_f, w_r, b_r, seq, merged):
    t, d = x.shape
    tpb = seq // ROW_TILE
    row = lambda i: (i, 0)
    full = lambda i: (0, 0)
    per_b = lambda i: (i // tpb, 0, 0)
    tail_specs = [
        pl.BlockSpec((d, d), full),
        pl.BlockSpec((ROW_TILE, d), row),
        pl.BlockSpec((1, 1, d), per_b),
        pl.BlockSpec((1, d), full),
        pl.BlockSpec((1, 1, d), per_b),
        pl.BlockSpec((1, 1, d), per_b),
        pl.BlockSpec((2, d, N_EXPERTS), lambda i: (0, 0, 0)),
        pl.BlockSpec((1, N_EXPERTS), full),
    ]
    w_hi, w_mid, _ = _split3(w_r)
    tail_args = (w_o, x, gate_m, g_ffn.reshape(1, d), shift_f, scale_f, jnp.stack([w_hi, w_mid]).astype(BF16),
                 b_r.reshape(1, N_EXPERTS))
    if merged:
        o0, o1, o2, lse = mix_inputs
        expand = jnp.asarray(np.kron(np.eye(N_HEADS, dtype=np.float32),
                                     np.ones((1, HEAD_DIM), np.float32)), BF16)
        d1, d2 = DIL_PAIRS[1][1], DIL_PAIRS[2][1]
        assert _perm_tile(d1) == ROW_TILE
        t2 = _perm_tile(d2)
        sub2 = ROW_TILE // d2
        o2v = o2.reshape(t // t2, d2, t2 // d2, d)
        per2 = t2 // ROW_TILE
        kern = _merge_oproj_kernel
        specs = [pl.BlockSpec((ROW_TILE, d), row), pl.BlockSpec((ROW_TILE, d), row),
                 pl.BlockSpec((1, d2, sub2, d), lambda i: (i // per2, 0, i % per2, 0)),
                 pl.BlockSpec((ROW_TILE, ROW_TILE), full), pl.BlockSpec((ROW_TILE, ROW_TILE), full),
                 pl.BlockSpec((ROW_TILE, 3 * N_HEADS), row), pl.BlockSpec((N_HEADS, d), full)]
        args = (o0, o1, o2v, _perm_matrix(ROW_TILE, d1), _perm_matrix(ROW_TILE, d2), lse, expand)
    else:
        kern = _oproj_kernel
        specs = [pl.BlockSpec((ROW_TILE, d), row)]
        args = mix_inputs
    return pl.pallas_call(
        kern,
        out_shape=(jax.ShapeDtypeStruct((t, d), F32), jax.ShapeDtypeStruct((t, d // 2), jnp.uint32),
                   jax.ShapeDtypeStruct((t, N_EXPERTS), F32)),
        grid=(t // ROW_TILE,),
        in_specs=specs + tail_specs,
        out_specs=(pl.BlockSpec((ROW_TILE, d), row), pl.BlockSpec((ROW_TILE, d // 2), row),
                   pl.BlockSpec((ROW_TILE, N_EXPERTS), row)),
        compiler_params=_params(("parallel",)),
        name="mixer_tail_merge" if merged else "mixer_tail",
    )(*args, *tail_args)


def _fox_kernel(q_ref, k_ref, v_ref, o_ref, vt_scr):
    qi = pl.program_id(2)
    seq = k_ref.shape[0]
    nk = seq // FOX_TILE
    half = HEAD_DIM

    ext = FOX_SUM_ROWS
    @pl.when(qi == 0)
    def _():
        lane = lax.broadcasted_iota(jnp.int32, (FOX_TILE, 2 * half), 1)
        erow = lax.broadcasted_iota(jnp.int32, (2 * ext, 2 * FOX_TILE), 0)
        ecol = lax.broadcasted_iota(jnp.int32, (2 * ext, 2 * FOX_TILE), 1)
        sum_rows = jnp.where((erow < ext) == (ecol < FOX_TILE), 1.0, 0.0).astype(BF16)
        for j in range(nk):
            vj = v_ref[j * FOX_TILE:(j + 1) * FOX_TILE, :].astype(F32)
            vt_scr[j, :2 * half, :FOX_TILE] = jnp.where(lane < half, vj, 0.0).T.astype(BF16)
            vt_scr[j, :2 * half, FOX_TILE:] = jnp.where(lane >= half, vj, 0.0).T.astype(BF16)
            vt_scr[j, 2 * half:, :] = sum_rows

    q0 = q_ref[:, :AUG]
    q1 = q_ref[:, AUG:]

    def softmax_step(s, m):
        m_new = jnp.maximum(m, jnp.max(s, axis=0, keepdims=True))
        return m_new, jnp.exp2(m - m_new), jnp.exp2((s - m_new).astype(BF16))

    def step(kj, carry, masked):
        m0, m1, acc = carry
        rows = pl.ds(pl.multiple_of(kj * FOX_TILE, FOX_TILE), FOX_TILE)
        s0 = _dot_nt(k_ref[rows, :AUG], q0)
        s1 = _dot_nt(k_ref[rows, AUG:], q1)
        if masked:
            kpos = lax.broadcasted_iota(jnp.int32, (FOX_TILE, FOX_TILE), 0)
            qpos = lax.broadcasted_iota(jnp.int32, (FOX_TILE, FOX_TILE), 1)
            keep = kpos <= qpos
            s0 = jnp.where(keep, s0, NEG)
            s1 = jnp.where(keep, s1, NEG)
        m0, a0, p0 = softmax_step(s0, m0)
        m1, a1, p1 = softmax_step(s1, m1)
        pv = _dot(vt_scr[kj], jnp.concatenate([p0, p1], axis=0))
        alpha = jnp.concatenate([jnp.broadcast_to(a0, (half, FOX_TILE)), jnp.broadcast_to(a1, (half, FOX_TILE)),
                                 jnp.broadcast_to(a0, (ext, FOX_TILE)), jnp.broadcast_to(a1, (ext, FOX_TILE))],
                                axis=0)
        return m0, m1, alpha * acc + pv

    neg = jnp.full((1, FOX_TILE), NEG, F32)
    init = (neg, neg, jnp.zeros((2 * half + 2 * ext, FOX_TILE), F32))
    carry = lax.fori_loop(0, qi, lambda kj, c: step(kj, c, False), init)
    _, _, acc = step(qi, carry, True)
    inv0 = 1.0 / acc[2 * half:2 * half + 1, :]
    inv1 = 1.0 / acc[2 * half + ext:2 * half + ext + 1, :]
    inv = jnp.concatenate([jnp.broadcast_to(inv0, (half, FOX_TILE)),
                           jnp.broadcast_to(inv1, (half, FOX_TILE))], axis=0)
    o_ref[...] = (acc[:2 * half, :] * inv).T.astype(o_ref.dtype)


def _fox(q_aug, k_aug, v, n_batch, seq):
    t = q_aug.shape[0]
    nq = seq // FOX_TILE
    pairs = N_HEADS // 2
    return pl.pallas_call(
        _fox_kernel,
        out_shape=jax.ShapeDtypeStruct((t, N_HEADS * HEAD_DIM), BF16),
        grid=(n_batch, pairs, nq),
        in_specs=[
            pl.BlockSpec((FOX_TILE, 2 * AUG), lambda b, p, qi: (b * nq + qi, p)),
            pl.BlockSpec((seq, 2 * AUG), lambda b, p, qi: (b, p)),
            pl.BlockSpec((seq, 2 * HEAD_DIM), lambda b, p, qi: (b, p)),
        ],
        out_specs=pl.BlockSpec((FOX_TILE, 2 * HEAD_DIM), lambda b, p, qi: (b * nq + qi, p)),
        scratch_shapes=[pltpu.VMEM((nq, 2 * HEAD_DIM + 2 * FOX_SUM_ROWS, 2 * FOX_TILE), BF16)],
        compiler_params=_params(("parallel", "parallel", "arbitrary")),
        name="fox",
    )(q_aug, k_aug, v)


def _expert_kernel(be_ref, nu_ref, tok_ref, h_hbm, wgu_ref, bgu_ref, wd_ref, bd_ref, o_ref,
                   wgu_bf, wd_bf, xbuf_a, xbuf_b, gsem):
    i = pl.program_id(0)
    d_ff = wd_ref.shape[1]
    odd = lax.rem(i, 2) == 1
    pairs = MOE_TILE // 2
    bufs = ((xbuf_a, gsem.at[0]), (xbuf_b, gsem.at[1]))

    def issue_gather(blk, buf, sem):
        base = blk * pairs
        for c in range(pairs):
            word = tok_ref[base + c]
            for r, tok in ((2 * c, word & 0xFFFF), (2 * c + 1, lax.shift_right_logical(word, 16))):
                pltpu.async_copy(h_hbm.at[pl.ds(tok, 1), :], buf.at[pl.ds(r, 1), :], sem, priority=r % 2)

    def wait_gather(buf, sem):
        pltpu.make_async_copy(h_hbm.at[pl.ds(0, MOE_TILE), :], buf, sem).wait()

    @pl.when(i == 0)
    def _():
        issue_gather(0, *bufs[0])

    @pl.when(jnp.logical_or(i == 0, be_ref[i] != be_ref[jnp.maximum(i - 1, 0)]))
    def _():
        wgu_bf[...] = wgu_ref[0].astype(BF16)
        wd_bf[...] = wd_ref[0].astype(BF16)

    def block(cur, nxt):
        @pl.when(i <= nu_ref[0])
        def _():
            wait_gather(*cur)

        @pl.when(i < nu_ref[0])
        def _():
            issue_gather(i + 1, *nxt)
            words = cur[0][...]
            x = jnp.concatenate([pltpu.bitcast(words << 16, F32),
                                 pltpu.bitcast(words & jnp.uint32(0xFFFF0000), F32)], axis=1).astype(BF16)
            gu = _dot(x, wgu_bf[...]) + bgu_ref[0]
            g = jnp.minimum(gu[:, :d_ff], SWIGLU_LIMIT)
            u = jnp.clip(gu[:, d_ff:], -SWIGLU_LIMIT, SWIGLU_LIMIT)
            act = (u + 1.0) * g * (1.0 / (1.0 + jnp.exp(-SWIGLU_ALPHA * g)))
            y = _dot(act.astype(BF16), wd_bf[...]) + bd_ref[0]
            o_ref[...] = y.astype(o_ref.dtype)

    @pl.when(jnp.logical_not(odd))
    def _():
        block(bufs[0], bufs[1])

    @pl.when(odd)
    def _():
        block(bufs[1], bufs[0])

    @pl.when(i >= nu_ref[0])
    def _():
        o_ref[...] = jnp.zeros_like(o_ref)


def _experts(h, tok_rows, w_gu, b_gu, w_down, b_down, block_expert, n_used, layer):
    t = h.shape[0]
    d = w_gu.shape[2]
    n_rows = tok_rows.shape[0]
    depth, e, _, n_gu = w_gu.shape
    d_ff = w_down.shape[2]
    n_blocks = n_rows // MOE_TILE
    assert t <= 65536
    packed = tok_rows[0::2] | (tok_rows[1::2] << 16)
    grid_spec = pltpu.PrefetchScalarGridSpec(
        num_scalar_prefetch=3,
        grid=(n_blocks,),
        in_specs=[
            pl.BlockSpec(memory_space=pl.ANY),
            pl.BlockSpec((None, 1, d, n_gu), lambda i, be, nu, tk: (layer, be[i], 0, 0)),
            pl.BlockSpec((None, 1, 1, n_gu), lambda i, be, nu, tk: (layer, be[i], 0, 0)),
            pl.BlockSpec((None, 1, d_ff, d), lambda i, be, nu, tk: (layer, be[i], 0, 0)),
            pl.BlockSpec((None, 1, 1, d), lambda i, be, nu, tk: (layer, be[i], 0, 0)),
        ],
        out_specs=pl.BlockSpec((MOE_TILE, d), lambda i, be, nu, tk: (i, 0)),
        scratch_shapes=[pltpu.VMEM((d, n_gu), BF16), pltpu.VMEM((d_ff, d), BF16),
                        pltpu.VMEM((MOE_TILE, d // 2), jnp.uint32), pltpu.VMEM((MOE_TILE, d // 2), jnp.uint32),
                        pltpu.SemaphoreType.DMA((2,))],
    )
    return pl.pallas_call(
        _expert_kernel,
        out_shape=jax.ShapeDtypeStruct((n_rows, d), BF16),
        grid_spec=grid_spec,
        compiler_params=_params(("arbitrary",)),
        name="experts",
    )(block_expert, n_used, packed, h, w_gu, b_gu.reshape(depth, e, 1, n_gu), w_down,
      b_down.reshape(depth, e, 1, d))


def _router_kernel(lg_ref, tri_ref, upper_ref, dest_ref, gate_ref, cnt_ref, carry, base):
    phase = pl.program_id(0)
    i = pl.program_id(1)
    tm = lg_ref.shape[0]
    lane = lax.broadcasted_iota(jnp.int32, (tm, N_EXPERTS), 1).astype(F32)
    wide = lax.broadcasted_iota(jnp.int32, (tm, 128), 1)

    work = lg_ref[...]
    hots, vals = [], []
    for _ in range(TOP_K):
        m = jnp.max(work, axis=1, keepdims=True)
        idx = jnp.min(jnp.where(work == m, lane, float(N_EXPERTS)), axis=1, keepdims=True)
        hot = lane == idx
        hots.append(hot)
        vals.append(m)
        work = jnp.where(hot, -jnp.inf, work)
    exps = [jnp.exp(v - vals[0]) for v in vals]
    inv = 1.0 / (exps[0] + exps[1] + exps[2] + exps[3])
    gate_tile = jnp.zeros((tm, 128), F32)
    for k in range(TOP_K):
        gate_tile = jnp.where(wide == k, exps[k] * inv, gate_tile)
    gate_ref[...] = gate_tile

    chosen = jnp.zeros((tm, N_EXPERTS), F32)
    for hot in hots:
        chosen = chosen + hot.astype(F32)

    @pl.when(jnp.logical_and(phase == 0, i == 0))
    def _():
        carry[...] = jnp.zeros_like(carry)
        base[...] = jnp.zeros_like(base)

    @pl.when(jnp.logical_and(phase == 1, i == 0))
    def _():
        counts = carry[...]
        padded = jnp.floor((counts + (MOE_TILE - 1)) * (1.0 / MOE_TILE)) * MOE_TILE
        base[...] = jnp.dot(jnp.broadcast_to(padded, (8, N_EXPERTS)), upper_ref[...],
                            preferred_element_type=F32, precision=lax.Precision.HIGHEST)[0:1, :]
        carry[...] = jnp.zeros_like(carry)

    ahead = _dot(tri_ref[...], chosen.astype(BF16))
    pos = ahead + carry[...] + base[...]
    dest_tile = jnp.zeros((tm, 128), F32)
    for k in range(TOP_K):
        row = jnp.sum(jnp.where(hots[k], pos, 0.0), axis=1, keepdims=True)
        dest_tile = jnp.where(wide == k, row, dest_tile)
    dest_ref[...] = dest_tile.astype(jnp.int32)
    carry[...] = carry[...] + jnp.sum(chosen, axis=0, keepdims=True)
    cnt_ref[...] = carry[...]


def _router(logits):
    t = logits.shape[0]
    nt = t // ROW_TILE
    tri = jnp.asarray(np.tril(np.ones((ROW_TILE, ROW_TILE), np.float32), -1), BF16)
    upper = jnp.asarray(np.triu(np.ones((N_EXPERTS, N_EXPERTS), np.float32), 1), F32)
    return pl.pallas_call(
        _router_kernel,
        out_shape=(jax.ShapeDtypeStruct((t + ROW_TILE, 128), jnp.int32),
                   jax.ShapeDtypeStruct((t + ROW_TILE, 128), F32),
                   jax.ShapeDtypeStruct((1, N_EXPERTS), F32)),
        grid=(2, nt),
        in_specs=[
            pl.BlockSpec((ROW_TILE, N_EXPERTS), lambda p, i: (i, 0)),
            pl.BlockSpec((ROW_TILE, ROW_TILE), lambda p, i: (0, 0)),
            pl.BlockSpec((N_EXPERTS, N_EXPERTS), lambda p, i: (0, 0)),
        ],
        out_specs=(pl.BlockSpec((ROW_TILE, 128), lambda p, i: (p * i + (1 - p) * nt, 0)),
                   pl.BlockSpec((ROW_TILE, 128), lambda p, i: (p * i + (1 - p) * nt, 0)),
                   pl.BlockSpec((1, N_EXPERTS), lambda p, i: (0, 0))),
        scratch_shapes=[pltpu.VMEM((1, N_EXPERTS), F32), pltpu.VMEM((1, N_EXPERTS), F32)],
        compiler_params=_params(("arbitrary", "arbitrary")),
        name="router",
    )(logits, tri, upper)


def _route(logits):
    t = logits.shape[0]
    n_slots = t * TOP_K
    n_rows = n_slots + N_EXPERTS * MOE_TILE
    n_blocks = n_rows // MOE_TILE
    dest_w, gates_w, counts = _router(logits)
    counts = counts.reshape(N_EXPERTS).astype(jnp.int32)
    pad_ends = jnp.cumsum((counts + MOE_TILE - 1) // MOE_TILE * MOE_TILE)
    block_start = jnp.arange(n_blocks, dtype=jnp.int32) * MOE_TILE
    block_expert = jnp.minimum(jnp.sum(block_start[:, None] >= pad_ends[None, :], axis=1),
                               N_EXPERTS - 1).astype(jnp.int32)
    n_used = (pad_ends[-1] // MOE_TILE).astype(jnp.int32).reshape(1)
    dest = dest_w[:t, :TOP_K].T.reshape(-1)
    tok = jnp.tile(jnp.arange(t, dtype=jnp.int32), TOP_K)
    tok_rows = (jnp.arange(n_rows, dtype=jnp.int32) % t).at[dest].set(tok, unique_indices=True)
    return dest, tok_rows, gates_w, block_expert, n_used


def _combine_kernel(x_ref, y0_ref, y1_ref, y2_ref, y3_ref, gt_ref, gf_ref, g_ref, o_ref, *, final):
    gt = gt_ref[...]
    moe = (gt[:, 0:1] * y0_ref[...].astype(F32) + gt[:, 1:2] * y1_ref[...].astype(F32)
           + gt[:, 2:3] * y2_ref[...].astype(F32) + gt[:, 3:4] * y3_ref[...].astype(F32))
    x_new = x_ref[...] + gf_ref[0] * moe
    if final:
        x_new = _norm_mod(x_new, g_ref[...], None, None)
    o_ref[...] = x_new


def _combine(x, y_slots, gates, gate_f, g_final, seq, final):
    t, d = x.shape
    tpb = seq // ROW_TILE
    nt = t // ROW_TILE
    assert TOP_K == 4
    return pl.pallas_call(
        functools.partial(_combine_kernel, final=final),
        out_shape=jax.ShapeDtypeStruct((t, d), F32),
        grid=(nt,),
        in_specs=[pl.BlockSpec((ROW_TILE, d), lambda i: (i, 0))] + [
            pl.BlockSpec((ROW_TILE, d), functools.partial(lambda i, k: (k * nt + i, 0), k=k))
            for k in range(TOP_K)] + [
            pl.BlockSpec((ROW_TILE, 128), lambda i: (i, 0)),
            pl.BlockSpec((1, 1, d), lambda i: (i // tpb, 0, 0)),
            pl.BlockSpec((1, d), lambda i: (0, 0)),
        ],
        out_specs=pl.BlockSpec((ROW_TILE, d), lambda i: (i, 0)),
        compiler_params=_params(("parallel",)),
        name="combine_final" if final else "combine",
    )(x, y_slots, y_slots, y_slots, y_slots, gates, gate_f, g_final.reshape(1, d))


def _moe(x, h, logits, gate_f, w_gu, b_gu, w_down, b_down, g_final, seq, layer, final):
    t, d = x.shape
    dest, tok_rows, gates, block_expert, n_used = _route(logits)
    ybuf = _experts(h, tok_rows, w_gu, b_gu, w_down, b_down, block_expert, n_used, layer)
    y_slots = jnp.take(ybuf, dest, axis=0, mode="clip")
    return _combine(x, y_slots, gates, gate_f, g_final, seq, final)


def kernel(x, c, ada_w, ada_b, norm_mix_g, norm_ffn_g, a_w_qkv, a_w_o, rel_bias, kv_norm_g, w_kvf, b_f,
           b_w_q, b_w_o, router_w, router_b, w_gu, b_gu, w_down, b_down, final_norm_g):
    n_batch, seq, d = x.shape
    t = n_batch * seq
    width = N_HEADS * HEAD_DIM
    xf = x.reshape(t, d)

    ada = _ada(c, ada_w, ada_b)
    mods = [[ada[l, :, i * d:(i + 1) * d].reshape(n_batch, 1, d) for i in range(6)] for l in range(2)]

    shift_m, scale_m, gate_m, shift_f, scale_f, gate_f = mods[0]
    qscale = np.ones((3, 3, 1), np.float32)
    qscale[:, 0] = HEAD_DIM ** -0.5
    w_qkv = (a_w_qkv[0].reshape(d, 3, 3, width) * qscale).astype(BF16)

    qi = np.arange(DIL_BLOCK, dtype=np.int32)[:, None]
    kj = np.arange(2 * DIL_BLOCK, dtype=np.int32)[None, :]
    delta = qi + DIL_BLOCK - kj
    in_band = (delta >= 0) & (delta <= DIL_BLOCK)
    outs, lses = [], []
    for g, (window, dilation) in enumerate(DIL_PAIRS):
        assert window // dilation == DIL_BLOCK
        qkv = _proj(xf, norm_mix_g[0], shift_m, scale_m, w_qkv[:, g].reshape(d, 3 * width), seq, dilation, g)
        bucket = _t5_bucket_np(np.clip(delta, 0, None) * dilation)
        tab = rel_bias[:, g * N_HEADS:(g + 1) * N_HEADS].astype(F32)
        onehot = jnp.asarray(np.eye(NUM_BUCKETS, dtype=np.float32)[bucket])
        bias = jnp.einsum('ijb,bh->hij', onehot, tab, precision=lax.Precision.HIGHEST)
        bias = jnp.where(in_band, bias, NEG)
        bias_first = jnp.where(kj >= DIL_BLOCK, bias, NEG)
        o, lse = _dilated_group(qkv, jnp.stack([bias_first, bias]), g, dilation, n_batch, seq)
        outs.append(o)
        lses.append(_unpermute_rows(lse, dilation)[:, :N_HEADS])
    lse_all = jnp.concatenate(lses, axis=1)
    x1, h1, logits1 = _mixer_tail((outs[0], outs[1], outs[2], lse_all), a_w_o[0].astype(BF16), xf, gate_m,
                                  norm_ffn_g[0], shift_f, scale_f, router_w[0], router_b[0], seq, True)
    x2 = _moe(x1, h1, logits1, gate_f, w_gu, b_gu, w_down, b_down, final_norm_g, seq, 0, False)

    w_f = jnp.pad(jnp.tile(w_kvf[:, 2 * width:], (1, 3)), ((0, 0), (0, 128 - 3 * N_HEADS))).astype(BF16)
    b_fp = jnp.pad(jnp.tile(b_f, 3), (0, 128 - 3 * N_HEADS)).reshape(1, 128)
    k_aug, v_sh, fcum = _kvf(x2, kv_norm_g, _aug_weight(w_kvf[:, :width]), w_kvf[:, width:2 * width].astype(BF16),
                             w_f, b_fp, n_batch)

    shift_m, scale_m, gate_m, shift_f, scale_f, gate_f = mods[1]
    q_aug = _qaug(x2, norm_mix_g[1], shift_m, scale_m, _aug_weight(b_w_q[0] * (HEAD_DIM ** -0.5 * LOG2E)), fcum, seq)
    o1 = _fox(q_aug, k_aug, v_sh, n_batch, seq)

    x3, h3, logits3 = _mixer_tail((o1,), b_w_o[0].astype(BF16), x2, gate_m, norm_ffn_g[1], shift_f, scale_f,
                                  router_w[1], router_b[1], seq, False)
    out = _moe(x3, h3, logits3, gate_f, w_gu, b_gu, w_down, b_down, final_norm_g, seq, 1, True)
    return out.reshape(n_batch, seq, d)
```

```python
import functools
import math

import numpy as np
import jax
import jax.numpy as jnp
from jax import lax
from jax.experimental import pallas as pl
from jax.experimental.pallas import tpu as pltpu

F32 = jnp.float32
BF16 = jnp.bfloat16

D_MODEL = 1024
HEAD_DIM = 64
N_HEADS = 16
DIL_PAIRS = ((128, 1), (512, 4), (2048, 16))
DIL_BLOCK = 128
NUM_BUCKETS = 32
MAX_DISTANCE = 2048
N_EXPERTS = 32
TOP_K = 4
SWIGLU_LIMIT = 7.0
SWIGLU_ALPHA = 1.702
RMS_EPS = 1e-6
NEG = -1e30
LOG2E = math.log2(math.e)

ROW_TILE = 512
MOE_TILE = 512
FOX_TILE = 512
AUG = 128
FOX_SUM_ROWS = 8
VMEM_LIMIT = 56 * 1024 * 1024


def _params(sem, vmem=VMEM_LIMIT):
    return pltpu.CompilerParams(dimension_semantics=sem, vmem_limit_bytes=vmem)


def _dot(a, b):
    return jnp.dot(a, b, preferred_element_type=F32)


def _dot_nt(a, b):
    return lax.dot_general(a, b, (((1,), (1,)), ((), ())), preferred_element_type=F32)


def _dot_tn(a, b):
    return lax.dot_general(a, b, (((0,), (0,)), ((), ())), preferred_element_type=F32)


def _norm_mod(x, g, shift, scale):
    ms = jnp.mean(x * x, axis=-1, keepdims=True)
    y = x * lax.rsqrt(ms + RMS_EPS) * g
    if scale is not None:
        y = y * (1.0 + scale) + shift
    return y


def _ada_kernel(c_ref, w_ref, b_ref, o_ref):
    c = c_ref[...]
    act = c * (1.0 / (1.0 + jnp.exp(-c)))
    o_ref[0] = jnp.dot(act, w_ref[0], preferred_element_type=F32,
                       precision=lax.Precision.HIGHEST) + b_ref[0]


def _ada(c, ada_w, ada_b):
    depth, d, n = ada_w.shape
    bsz = c.shape[0]
    tn = 1536
    return pl.pallas_call(
        _ada_kernel,
        out_shape=jax.ShapeDtypeStruct((depth, bsz, n), F32),
        grid=(depth, n // tn),
        in_specs=[
            pl.BlockSpec((bsz, d), lambda l, j: (0, 0)),
            pl.BlockSpec((1, d, tn), lambda l, j: (l, 0, j)),
            pl.BlockSpec((1, 1, tn), lambda l, j: (l, 0, j)),
        ],
        out_specs=pl.BlockSpec((1, bsz, tn), lambda l, j: (l, 0, j)),
        compiler_params=_params(("parallel", "parallel")),
        name="ada",
    )(c, ada_w, ada_b.reshape(depth, 1, n))


def _perm_tile(dilation):
    return max(ROW_TILE, DIL_BLOCK * dilation)


def _proj_kernel(x_ref, g_ref, sh_ref, sc_ref, w_ref, o_ref, h_scr, *xs_scr, dilation):
    @pl.when(pl.program_id(1) == 0)
    def _():
        if dilation == 1:
            h_scr[...] = _norm_mod(x_ref[...], g_ref[...], sh_ref[0], sc_ref[0]).astype(BF16)
        else:
            (xs,) = xs_scr
            n_lane = xs.shape[0]
            for c in range(n_lane):
                xs[c] = x_ref[:, c * 128:(c + 1) * 128]
            chunk = x_ref.shape[0] // dilation
            for r in range(dilation):
                xr = jnp.concatenate([xs[c, pl.ds(r, chunk, stride=dilation), :] for c in range(n_lane)], axis=1)
                h_scr[r * chunk:(r + 1) * chunk, :] = _norm_mod(xr, g_ref[...], sh_ref[0], sc_ref[0]).astype(BF16)

    o_ref[...] = _dot(h_scr[...], w_ref[...]).astype(o_ref.dtype)


def _proj(x, g, shift, scale, w, seq, dilation, group):
    t, d = x.shape
    n = w.shape[1]
    tm = _perm_tile(dilation)
    tn = d
    tpb = seq // tm
    return pl.pallas_call(
        functools.partial(_proj_kernel, dilation=dilation),
        out_shape=jax.ShapeDtypeStruct((t, n), BF16),
        grid=(t // tm, n // tn),
        in_specs=[
            pl.BlockSpec((tm, d), lambda i, j: (i, 0)),
            pl.BlockSpec((1, d), lambda i, j: (0, 0)),
            pl.BlockSpec((1, 1, d), lambda i, j: (i // tpb, 0, 0)),
            pl.BlockSpec((1, 1, d), lambda i, j: (i // tpb, 0, 0)),
            pl.BlockSpec((d, tn), lambda i, j: (0, j)),
        ],
        out_specs=pl.BlockSpec((tm, tn), lambda i, j: (i, j)),
        scratch_shapes=[pltpu.VMEM((tm, d), BF16)] + (
            [pltpu.VMEM((d // 128, tm, 128), F32)] if dilation > 1 else []),
        compiler_params=_params(("parallel", "arbitrary")),
        name=f"proj{group}",
    )(x, g.reshape(1, d), shift, scale, w)


def _split3(f):
    def top(v):
        return lax.bitcast_convert_type(lax.bitcast_convert_type(v, jnp.uint32) & jnp.uint32(0xFFFF0000), F32)
    hi = top(f)
    r1 = f - hi
    mid = top(r1)
    return hi, mid, r1 - mid


def _forget_pieces(fc):
    hi, mid, lo = _split3(fc)
    lane = lax.broadcasted_iota(jnp.int32, fc.shape, 1)
    x = jnp.where(lane < N_HEADS, hi, jnp.where(lane < 2 * N_HEADS, mid, jnp.where(lane < 3 * N_HEADS, lo, 0.0)))
    return x.astype(BF16)


def _aug_tables(key_side):
    place = np.zeros((128, N_HEADS * AUG), np.float32)
    const = np.zeros((1, N_HEADS * AUG), np.float32)
    for h in range(N_HEADS):
        base = h * AUG + HEAD_DIM
        for piece in range(3):
            if key_side:
                place[piece * N_HEADS + h, base + piece] = 1.0
                const[0, base + 3 + piece] = 1.0
            else:
                place[piece * N_HEADS + h, base + 3 + piece] = 1.0
                const[0, base + piece] = -1.0
    return jnp.asarray(place, BF16), jnp.asarray(const, F32)


def _aug_weight(w):
    d = w.shape[0]
    w = jnp.pad(w.reshape(d, N_HEADS, HEAD_DIM), ((0, 0), (0, 0), (0, AUG - HEAD_DIM)))
    return w.reshape(d, N_HEADS * AUG).astype(BF16)


def _kvf_kernel(x_ref, g_ref, wk_ref, wv_ref, wf_ref, bf_ref, tri_ref, pl_ref, cr_ref,
                k_ref, v_ref, f_ref, carry, *, tpb):
    h = _norm_mod(x_ref[...], g_ref[...], None, None).astype(BF16)
    v_ref[...] = _dot(h, wv_ref[...]).astype(v_ref.dtype)
    z = _dot(h, wf_ref[...]) + bf_ref[...]
    lf = jnp.minimum(z, 0.0) - jnp.log(1.0 + jnp.exp(-jnp.abs(z)))
    hi, mid, lo = _split3(lf)
    tri = tri_ref[...]
    cs = _dot(tri, hi.astype(BF16)) + _dot(tri, mid.astype(BF16)) + _dot(tri, lo.astype(BF16))

    @pl.when(pl.program_id(0) % tpb == 0)
    def _():
        carry[...] = jnp.zeros_like(carry)

    cs = cs + carry[...]
    carry[...] = cs[ROW_TILE - 1:ROW_TILE, :]
    f2 = cs * LOG2E
    f_ref[...] = f2
    k_ref[...] = (_dot(h, wk_ref[...]) + _dot(_forget_pieces(f2), pl_ref[...]) + cr_ref[...]).astype(k_ref.dtype)


def _kvf(x, g, w_k, w_v, w_f, b_f, n_batch):
    t, d = x.shape
    n_aug = N_HEADS * AUG
    tri = jnp.asarray(np.tril(np.ones((ROW_TILE, ROW_TILE), np.float32)), BF16)
    place, const = _aug_tables(True)
    assert (t // ROW_TILE) % n_batch == 0
    full = lambda i: (0, 0)
    row = lambda i: (i, 0)
    return pl.pallas_call(
        functools.partial(_kvf_kernel, tpb=t // ROW_TILE // n_batch),
        out_shape=(jax.ShapeDtypeStruct((t, n_aug), BF16), jax.ShapeDtypeStruct((t, d), BF16),
                   jax.ShapeDtypeStruct((t, 128), F32)),
        grid=(t // ROW_TILE,),
        in_specs=[
            pl.BlockSpec((ROW_TILE, d), row),
            pl.BlockSpec((1, d), full),
            pl.BlockSpec((d, n_aug), full),
            pl.BlockSpec((d, d), full),
            pl.BlockSpec((d, 128), full),
            pl.BlockSpec((1, 128), full),
            pl.BlockSpec((ROW_TILE, ROW_TILE), full),
            pl.BlockSpec((128, n_aug), full),
            pl.BlockSpec((1, n_aug), full),
        ],
        out_specs=(pl.BlockSpec((ROW_TILE, n_aug), row), pl.BlockSpec((ROW_TILE, d), row),
                   pl.BlockSpec((ROW_TILE, 128), row)),
        scratch_shapes=[pltpu.VMEM((1, 128), F32)],
        compiler_params=_params(("arbitrary",)),
        name="kvf",
    )(x, g.reshape(1, d), w_k, w_v, w_f, b_f, tri, place, const)


def _qaug_kernel(x_ref, g_ref, sh_ref, sc_ref, w_ref, f_ref, pl_ref, cr_ref, o_ref):
    h = _norm_mod(x_ref[...], g_ref[...], sh_ref[0], sc_ref[0]).astype(BF16)
    o_ref[...] = (_dot(h, w_ref[...]) + _dot(_forget_pieces(f_ref[...]), pl_ref[...])
                  + cr_ref[...]).astype(o_ref.dtype)


def _qaug(x, g, shift, scale, w, fcum, seq):
    t, d = x.shape
    n_aug = N_HEADS * AUG
    tpb = seq // ROW_TILE
    place, const = _aug_tables(False)
    full = lambda i: (0, 0)
    row = lambda i: (i, 0)
    per_b = lambda i: (i // tpb, 0, 0)
    return pl.pallas_call(
        _qaug_kernel,
        out_shape=jax.ShapeDtypeStruct((t, n_aug), BF16),
        grid=(t // ROW_TILE,),
        in_specs=[
            pl.BlockSpec((ROW_TILE, d), row),
            pl.BlockSpec((1, d), full),
            pl.BlockSpec((1, 1, d), per_b),
            pl.BlockSpec((1, 1, d), per_b),
            pl.BlockSpec((d, n_aug), full),
            pl.BlockSpec((ROW_TILE, 128), row),
            pl.BlockSpec((128, n_aug), full),
            pl.BlockSpec((1, n_aug), full),
        ],
        out_specs=pl.BlockSpec((ROW_TILE, n_aug), row),
        compiler_params=_params(("parallel",)),
        name="qaug",
    )(x, g.reshape(1, d), shift, scale, w, fcum, place, const)


def _dil_kernel(q_ref, kp_ref, kc_ref, vp_ref, vc_ref, bias_ref, o_ref, lse_ref):
    table = jnp.minimum(pl.program_id(2), 1)
    pair_w = 2 * HEAD_DIM
    lane = lax.broadcasted_iota(jnp.int32, (DIL_BLOCK, pair_w), 1)
    first = lane < HEAD_DIM
    first2 = lax.broadcasted_iota(jnp.int32, (2 * DIL_BLOCK, pair_w), 1) < HEAD_DIM
    lse_tile = jnp.zeros((DIL_BLOCK, pair_w), F32)
    zero = jnp.zeros((), BF16)
    n_pairs = N_HEADS // 2
    slices = [slice(pair * pair_w, (pair + 1) * pair_w) for pair in range(n_pairs)]
    scores = []
    for pair, sl in enumerate(slices):
        q = q_ref[:, sl]
        k2 = jnp.concatenate([kp_ref[:, sl], kc_ref[:, sl]], axis=0)
        for which in range(2):
            qh = jnp.where(first, q, zero) if which == 0 else jnp.where(first, zero, q)
            scores.append(_dot_nt(qh, k2) + bias_ref[table, 2 * pair + which])
    probs, invs = [], []
    for h, s in enumerate(scores):
        m = jnp.max(s, axis=-1, keepdims=True)
        p = jnp.exp(s - m)
        den = jnp.sum(p, axis=-1, keepdims=True)
        probs.append(p.astype(BF16))
        invs.append(1.0 / den)
        lse_tile = jnp.where(lane == h, m + jnp.log(den), lse_tile)
    for pair, sl in enumerate(slices):
        v2 = jnp.concatenate([vp_ref[:, sl], vc_ref[:, sl]], axis=0)
        vcat = jnp.concatenate([jnp.where(first2, v2, zero), jnp.where(first2, zero, v2)], axis=0)
        o = _dot(jnp.concatenate(probs[2 * pair:2 * pair + 2], axis=1), vcat)
        o_ref[:, sl] = (o * jnp.where(first, invs[2 * pair], invs[2 * pair + 1])).astype(o_ref.dtype)
    lse_ref[...] = lse_tile


def _dilated_group(qkv, bias, group, dilation, n_batch, seq):
    t = qkv.shape[0]
    width = N_HEADS * HEAD_DIM
    nb = seq // dilation // DIL_BLOCK
    bpb = seq // DIL_BLOCK

    def blk(b, r, n):
        return b * bpb + n * dilation + r

    def spec(part, prev):
        if prev:
            return pl.BlockSpec((DIL_BLOCK, width), lambda b, r, n: (blk(b, r, jnp.maximum(n - 1, 0)), part))
        return pl.BlockSpec((DIL_BLOCK, width), lambda b, r, n: (blk(b, r, n), part))

    return pl.pallas_call(
        _dil_kernel,
        out_shape=(jax.ShapeDtypeStruct((t, width), BF16), jax.ShapeDtypeStruct((t, 2 * HEAD_DIM), F32)),
        grid=(n_batch, dilation, nb),
        in_specs=[spec(0, False), spec(1, True), spec(1, False), spec(2, True), spec(2, False),
                  pl.BlockSpec((2, N_HEADS, DIL_BLOCK, 2 * DIL_BLOCK), lambda b, r, n: (0, 0, 0, 0))],
        out_specs=(pl.BlockSpec((DIL_BLOCK, width), lambda b, r, n: (blk(b, r, n), 0)),
                   pl.BlockSpec((DIL_BLOCK, 2 * HEAD_DIM), lambda b, r, n: (blk(b, r, n), 0))),
        compiler_params=_params(("parallel", "parallel", "arbitrary")),
        name=f"dilated{group}",
    )(qkv, qkv, qkv, qkv, qkv, bias)


def _unpermute_rows(a, dilation):
    if dilation == 1:
        return a
    t, c = a.shape
    tm = _perm_tile(dilation)
    return a.reshape(t // tm, dilation, tm // dilation, c).transpose(0, 2, 1, 3).reshape(t, c)


def _perm_matrix(rows, dilation):
    per = rows // dilation
    p = np.zeros((rows, rows), np.float32)
    for r in range(dilation):
        for n in range(per):
            p[n * dilation + r, r * per + n] = 1.0
    return jnp.asarray(p, BF16)


def _t5_bucket_np(n):
    max_exact = NUM_BUCKETS // 2
    nf = np.maximum(n, 1).astype(np.float32)
    large = max_exact + (np.log(nf / np.float32(max_exact)) / np.float32(math.log(MAX_DISTANCE / max_exact))
                         * np.float32(NUM_BUCKETS - max_exact)).astype(np.int32)
    large = np.minimum(large, NUM_BUCKETS - 1)
    return np.where(n < max_exact, n, large)


def _mix_tail(mix_in, wo_ref, x_ref, gm_ref, g_ref, sh_ref, sc_ref, wr_ref, br_ref,
              x_out, h_out, lg_out):
    mix = _dot(mix_in, wo_ref[...])
    x_new = x_ref[...] + gm_ref[0] * mix
    x_out[...] = x_new
    h = _norm_mod(x_new, g_ref[...], sh_ref[0], sc_ref[0])
    bits = pltpu.bitcast(h.astype(BF16).astype(F32), jnp.uint32)
    half = h.shape[1] // 2
    h_out[...] = (bits[:, :half] >> 16) | (bits[:, half:] & jnp.uint32(0xFFFF0000))
    h_hi, h_mid, _ = _split3(h)
    h_hi = h_hi.astype(BF16)
    lg_out[...] = (_dot(h_hi, wr_ref[0]) + (_dot(h_mid.astype(BF16), wr_ref[0]) + _dot(h_hi, wr_ref[1]))
                   + br_ref[...])


def _merge_oproj_kernel(o0_ref, o1_ref, o2_ref, p1_ref, p2_ref, lse_ref, ex_ref, wo_ref, x_ref, gm_ref, g_ref,
                        sh_ref, sc_ref, wr_ref, br_ref, x_out, h_out, lg_out):
    lse = lse_ref[...]
    l0, l1, l2 = lse[:, 0:16], lse[:, 16:32], lse[:, 32:48]
    m = jnp.maximum(jnp.maximum(l0, l1), l2)
    e0, e1, e2 = jnp.exp(l0 - m), jnp.exp(l1 - m), jnp.exp(l2 - m)
    inv = 1.0 / (e0 + e1 + e2)
    ex = ex_ref[...]
    o1 = _dot(p1_ref[...], o1_ref[...])
    o2 = _dot(p2_ref[...], o2_ref[0].reshape(ROW_TILE, o2_ref.shape[-1]))
    merged = (_dot((e0 * inv).astype(BF16), ex) * o0_ref[...].astype(F32)
              + _dot((e1 * inv).astype(BF16), ex) * o1
              + _dot((e2 * inv).astype(BF16), ex) * o2)
    _mix_tail(merged.astype(BF16), wo_ref, x_ref, gm_ref, g_ref, sh_ref, sc_ref, wr_ref, br_ref,
              x_out, h_out, lg_out)


def _oproj_kernel(o_ref, wo_ref, x_ref, gm_ref, g_ref, sh_ref, sc_ref, wr_ref, br_ref,
                  x_out, h_out, lg_out):
    _mix_tail(o_ref[...], wo_ref, x_ref, gm_ref, g_ref, sh_ref, sc_ref, wr_ref, br_ref,
              x_out, h_out, lg_out)


def _mixer_tail(mix_inputs, w_o, x, gate_m, g_ffn, shift_f, scale_f, w_r, b_r, seq, merged):
    t, d = x.shape
    tpb = seq // ROW_TILE
    row = lambda i: (i, 0)
    full = lambda i: (0, 0)
    per_b = lambda i: (i // tpb, 0, 0)
    tail_specs = [
        pl.BlockSpec((d, d), full),
        pl.BlockSpec((ROW_TILE, d), row),
        pl.BlockSpec((1, 1, d), per_b),
        pl.BlockSpec((1, d), full),
        pl.BlockSpec((1, 1, d), per_b),
        pl.BlockSpec((1, 1, d), per_b),
        pl.BlockSpec((2, d, N_EXPERTS), lambda i: (0, 0, 0)),
        pl.BlockSpec((1, N_EXPERTS), full),
    ]
    w_hi, w_mid, _ = _split3(w_r)
    tail_args = (w_o, x, gate_m, g_ffn.reshape(1, d), shift_f, scale_f, jnp.stack([w_hi, w_mid]).astype(BF16),
                 b_r.reshape(1, N_EXPERTS))
    if merged:
        o0, o1, o2, lse = mix_inputs
        expand = jnp.asarray(np.kron(np.eye(N_HEADS, dtype=np.float32),
                                     np.ones((1, HEAD_DIM), np.float32)), BF16)
        d1, d2 = DIL_PAIRS[1][1], DIL_PAIRS[2][1]
        assert _perm_tile(d1) == ROW_TILE
        t2 = _perm_tile(d2)
        sub2 = ROW_TILE // d2
        o2v = o2.reshape(t // t2, d2, t2 // d2, d)
        per2 = t2 // ROW_TILE
        kern = _merge_oproj_kernel
        specs = [pl.BlockSpec((ROW_TILE, d), row), pl.BlockSpec((ROW_TILE, d), row),
                 pl.BlockSpec((1, d2, sub2, d), lambda i: (i // per2, 0, i % per2, 0)),
                 pl.BlockSpec((ROW_TILE, ROW_TILE), full), pl.BlockSpec((ROW_TILE, ROW_TILE), full),
                 pl.BlockSpec((ROW_TILE, 3 * N_HEADS), row), pl.BlockSpec((N_HEADS, d), full)]
        args = (o0, o1, o2v, _perm_matrix(ROW_TILE, d1), _perm_matrix(ROW_TILE, d2), lse, expand)
    else:
        kern = _oproj_kernel
        specs = [pl.BlockSpec((ROW_TILE, d), row)]
        args = mix_inputs
    return pl.pallas_call(
        kern,
        out_shape=(jax.ShapeDtypeStruct((t, d), F32), jax.ShapeDtypeStruct((t, d // 2), jnp.uint32),
                   jax.ShapeDtypeStruct((t, N_EXPERTS), F32)),
        grid=(t // ROW_TILE,),
        in_specs=specs + tail_specs,
        out_specs=(pl.BlockSpec((ROW_TILE, d), row), pl.BlockSpec((ROW_TILE, d // 2), row),
                   pl.BlockSpec((ROW_TILE, N_EXPERTS), row)),
        compiler_params=_params(("parallel",)),
        name="mixer_tail_merge" if merged else "mixer_tail",
    )(*args, *tail_args)


def _fox_kernel(q_ref, k_ref, v_ref, o_ref, vt_scr):
    qi = pl.program_id(2)
    seq = k_ref.shape[0]
    nk = seq // FOX_TILE
    half = HEAD_DIM

    ext = FOX_SUM_ROWS
    @pl.when(qi == 0)
    def _():
        lane = lax.broadcasted_iota(jnp.int32, (FOX_TILE, 2 * half), 1)
        erow = lax.broadcasted_iota(jnp.int32, (2 * ext, 2 * FOX_TILE), 0)
        ecol = lax.broadcasted_iota(jnp.int32, (2 * ext, 2 * FOX_TILE), 1)
        sum_rows = jnp.where((erow < ext) == (ecol < FOX_TILE), 1.0, 0.0).astype(BF16)
        for j in range(nk):
            vj = v_ref[j * FOX_TILE:(j + 1) * FOX_TILE, :].astype(F32)
            vt_scr[j, :2 * half, :FOX_TILE] = jnp.where(lane < half, vj, 0.0).T.astype(BF16)
            vt_scr[j, :2 * half, FOX_TILE:] = jnp.where(lane >= half, vj, 0.0).T.astype(BF16)
            vt_scr[j, 2 * half:, :] = sum_rows

    q0 = q_ref[:, :AUG]
    q1 = q_ref[:, AUG:]

    def softmax_step(s, m):
        m_new = jnp.maximum(m, jnp.max(s, axis=0, keepdims=True))
        return m_new, jnp.exp2(m - m_new), jnp.exp2((s - m_new).astype(BF16))

    def step(kj, carry, masked):
        m0, m1, acc = carry
        rows = pl.ds(pl.multiple_of(kj * FOX_TILE, FOX_TILE), FOX_TILE)
        s0 = _dot_nt(k_ref[rows, :AUG], q0)
        s1 = _dot_nt(k_ref[rows, AUG:], q1)
        if masked:
            kpos = lax.broadcasted_iota(jnp.int32, (FOX_TILE, FOX_TILE), 0)
            qpos = lax.broadcasted_iota(jnp.int32, (FOX_TILE, FOX_TILE), 1)
            keep = kpos <= qpos
            s0 = jnp.where(keep, s0, NEG)
            s1 = jnp.where(keep, s1, NEG)
        m0, a0, p0 = softmax_step(s0, m0)
        m1, a1, p1 = softmax_step(s1, m1)
        pv = _dot(vt_scr[kj], jnp.concatenate([p0, p1], axis=0))
        alpha = jnp.concatenate([jnp.broadcast_to(a0, (half, FOX_TILE)), jnp.broadcast_to(a1, (half, FOX_TILE)),
                                 jnp.broadcast_to(a0, (ext, FOX_TILE)), jnp.broadcast_to(a1, (ext, FOX_TILE))],
                                axis=0)
        return m0, m1, alpha * acc + pv

    neg = jnp.full((1, FOX_TILE), NEG, F32)
    init = (neg, neg, jnp.zeros((2 * half + 2 * ext, FOX_TILE), F32))
    carry = lax.fori_loop(0, qi, lambda kj, c: step(kj, c, False), init)
    _, _, acc = step(qi, carry, True)
    inv0 = 1.0 / acc[2 * half:2 * half + 1, :]
    inv1 = 1.0 / acc[2 * half + ext:2 * half + ext + 1, :]
    inv = jnp.concatenate([jnp.broadcast_to(inv0, (half, FOX_TILE)),
                           jnp.broadcast_to(inv1, (half, FOX_TILE))], axis=0)
    o_ref[...] = (acc[:2 * half, :] * inv).T.astype(o_ref.dtype)


def _fox(q_aug, k_aug, v, n_batch, seq):
    t = q_aug.shape[0]
    nq = seq // FOX_TILE
    pairs = N_HEADS // 2
    return pl.pallas_call(
        _fox_kernel,
        out_shape=jax.ShapeDtypeStruct((t, N_HEADS * HEAD_DIM), BF16),
        grid=(n_batch, pairs, nq),
        in_specs=[
            pl.BlockSpec((FOX_TILE, 2 * AUG), lambda b, p, qi: (b * nq + qi, p)),
            pl.BlockSpec((seq, 2 * AUG), lambda b, p, qi: (b, p)),
            pl.BlockSpec((seq, 2 * HEAD_DIM), lambda b, p, qi: (b, p)),
        ],
        out_specs=pl.BlockSpec((FOX_TILE, 2 * HEAD_DIM), lambda b, p, qi: (b * nq + qi, p)),
        scratch_shapes=[pltpu.VMEM((nq, 2 * HEAD_DIM + 2 * FOX_SUM_ROWS, 2 * FOX_TILE), BF16)],
        compiler_params=_params(("parallel", "parallel", "arbitrary")),
        name="fox",
    )(q_aug, k_aug, v)


def _expert_kernel(be_ref, nu_ref, tok_ref, h_hbm, wgu_ref, bgu_ref, wd_ref, bd_ref, o_ref,
                   wgu_bf, wd_bf, xbuf_a, xbuf_b, gsem):
    i = pl.program_id(0)
    d_ff = wd_ref.shape[1]
    odd = lax.rem(i, 2) == 1
    pairs = MOE_TILE // 2
    bufs = ((xbuf_a, gsem.at[0]), (xbuf_b, gsem.at[1]))

    def issue_gather(blk, buf, sem):
        base = blk * pairs
        for c in range(pairs):
            word = tok_ref[base + c]
            for r, tok in ((2 * c, word & 0xFFFF), (2 * c + 1, lax.shift_right_logical(word, 16))):
                pltpu.async_copy(h_hbm.at[pl.ds(tok, 1), :], buf.at[pl.ds(r, 1), :], sem, priority=r % 2)

    def wait_gather(buf, sem):
        pltpu.make_async_copy(h_hbm.at[pl.ds(0, MOE_TILE), :], buf, sem).wait()

    @pl.when(i == 0)
    def _():
        issue_gather(0, *bufs[0])

    @pl.when(jnp.logical_or(i == 0, be_ref[i] != be_ref[jnp.maximum(i - 1, 0)]))
    def _():
        wgu_bf[...] = wgu_ref[0].astype(BF16)
        wd_bf[...] = wd_ref[0].astype(BF16)

    def block(cur, nxt):
        @pl.when(i <= nu_ref[0])
        def _():
            wait_gather(*cur)

        @pl.when(i < nu_ref[0])
        def _():
            issue_gather(i + 1, *nxt)
            words = cur[0][...]
            x = jnp.concatenate([pltpu.bitcast(words << 16, F32),
                                 pltpu.bitcast(words & jnp.uint32(0xFFFF0000), F32)], axis=1).astype(BF16)
            gu = _dot(x, wgu_bf[...]) + bgu_ref[0]
            g = jnp.minimum(gu[:, :d_ff], SWIGLU_LIMIT)
            u = jnp.clip(gu[:, d_ff:], -SWIGLU_LIMIT, SWIGLU_LIMIT)
            act = (u + 1.0) * g * (1.0 / (1.0 + jnp.exp(-SWIGLU_ALPHA * g)))
            y = _dot(act.astype(BF16), wd_bf[...]) + bd_ref[0]
            o_ref[...] = y.astype(o_ref.dtype)

    @pl.when(jnp.logical_not(odd))
    def _():
        block(bufs[0], bufs[1])

    @pl.when(odd)
    def _():
        block(bufs[1], bufs[0])

    @pl.when(i >= nu_ref[0])
    def _():
        o_ref[...] = jnp.zeros_like(o_ref)


def _experts(h, tok_rows, w_gu, b_gu, w_down, b_down, block_expert, n_used, layer):
    t = h.shape[0]
    d = w_gu.shape[2]
    n_rows = tok_rows.shape[0]
    depth, e, _, n_gu = w_gu.shape
    d_ff = w_down.shape[2]
    n_blocks = n_rows // MOE_TILE
    assert t <= 65536
    packed = tok_rows[0::2] | (tok_rows[1::2] << 16)
    grid_spec = pltpu.PrefetchScalarGridSpec(
        num_scalar_prefetch=3,
        grid=(n_blocks,),
        in_specs=[
            pl.BlockSpec(memory_space=pl.ANY),
            pl.BlockSpec((None, 1, d, n_gu), lambda i, be, nu, tk: (layer, be[i], 0, 0)),
            pl.BlockSpec((None, 1, 1, n_gu), lambda i, be, nu, tk: (layer, be[i], 0, 0)),
            pl.BlockSpec((None, 1, d_ff, d), lambda i, be, nu, tk: (layer, be[i], 0, 0)),
            pl.BlockSpec((None, 1, 1, d), lambda i, be, nu, tk: (layer, be[i], 0, 0)),
        ],
        out_specs=pl.BlockSpec((MOE_TILE, d), lambda i, be, nu, tk: (i, 0)),
        scratch_shapes=[pltpu.VMEM((d, n_gu), BF16), pltpu.VMEM((d_ff, d), BF16),
                        pltpu.VMEM((MOE_TILE, d // 2), jnp.uint32), pltpu.VMEM((MOE_TILE, d // 2), jnp.uint32),
                        pltpu.SemaphoreType.DMA((2,))],
    )
    return pl.pallas_call(
        _expert_kernel,
        out_shape=jax.ShapeDtypeStruct((n_rows, d), BF16),
        grid_spec=grid_spec,
        compiler_params=_params(("arbitrary",)),
        name="experts",
    )(block_expert, n_used, packed, h, w_gu, b_gu.reshape(depth, e, 1, n_gu), w_down,
      b_down.reshape(depth, e, 1, d))


def _router_kernel(lg_ref, tri_ref, upper_ref, dest_ref, gate_ref, cnt_ref, carry, base):
    phase = pl.program_id(0)
    i = pl.program_id(1)
    tm = lg_ref.shape[0]
    lane = lax.broadcasted_iota(jnp.int32, (tm, N_EXPERTS), 1).astype(F32)
    wide = lax.broadcasted_iota(jnp.int32, (tm, 128), 1)

    work = lg_ref[...]
    hots, vals = [], []
    for _ in range(TOP_K):
        m = jnp.max(work, axis=1, keepdims=True)
        idx = jnp.min(jnp.where(work == m, lane, float(N_EXPERTS)), axis=1, keepdims=True)
        hot = lane == idx
        hots.append(hot)
        vals.append(m)
        work = jnp.where(hot, -jnp.inf, work)
    exps = [jnp.exp(v - vals[0]) for v in vals]
    inv = 1.0 / (exps[0] + exps[1] + exps[2] + exps[3])
    gate_tile = jnp.zeros((tm, 128), F32)
    for k in range(TOP_K):
        gate_tile = jnp.where(wide == k, exps[k] * inv, gate_tile)
    gate_ref[...] = gate_tile

    chosen = jnp.zeros((tm, N_EXPERTS), F32)
    for hot in hots:
        chosen = chosen + hot.astype(F32)

    @pl.when(jnp.logical_and(phase == 0, i == 0))
    def _():
        carry[...] = jnp.zeros_like(carry)
        base[...] = jnp.zeros_like(base)

    @pl.when(jnp.logical_and(phase == 1, i == 0))
    def _():
        counts = carry[...]
        padded = jnp.floor((counts + (MOE_TILE - 1)) * (1.0 / MOE_TILE)) * MOE_TILE
        base[...] = jnp.dot(jnp.broadcast_to(padded, (8, N_EXPERTS)), upper_ref[...],
                            preferred_element_type=F32, precision=lax.Precision.HIGHEST)[0:1, :]
        carry[...] = jnp.zeros_like(carry)

    ahead = _dot(tri_ref[...], chosen.astype(BF16))
    pos = ahead + carry[...] + base[...]
    dest_tile = jnp.zeros((tm, 128), F32)
    for k in range(TOP_K):
        row = jnp.sum(jnp.where(hots[k], pos, 0.0), axis=1, keepdims=True)
        dest_tile = jnp.where(wide == k, row, dest_tile)
    dest_ref[...] = dest_tile.astype(jnp.int32)
    carry[...] = carry[...] + jnp.sum(chosen, axis=0, keepdims=True)
    cnt_ref[...] = carry[...]


def _router(logits):
    t = logits.shape[0]
    nt = t // ROW_TILE
    tri = jnp.asarray(np.tril(np.ones((ROW_TILE, ROW_TILE), np.float32), -1), BF16)
    upper = jnp.asarray(np.triu(np.ones((N_EXPERTS, N_EXPERTS), np.float32), 1), F32)
    return pl.pallas_call(
        _router_kernel,
        out_shape=(jax.ShapeDtypeStruct((t + ROW_TILE, 128), jnp.int32),
                   jax.ShapeDtypeStruct((t + ROW_TILE, 128), F32),
                   jax.ShapeDtypeStruct((1, N_EXPERTS), F32)),
        grid=(2, nt),
        in_specs=[
            pl.BlockSpec((ROW_TILE, N_EXPERTS), lambda p, i: (i, 0)),
            pl.BlockSpec((ROW_TILE, ROW_TILE), lambda p, i: (0, 0)),
            pl.BlockSpec((N_EXPERTS, N_EXPERTS), lambda p, i: (0, 0)),
        ],
        out_specs=(pl.BlockSpec((ROW_TILE, 128), lambda p, i: (p * i + (1 - p) * nt, 0)),
                   pl.BlockSpec((ROW_TILE, 128), lambda p, i: (p * i + (1 - p) * nt, 0)),
                   pl.BlockSpec((1, N_EXPERTS), lambda p, i: (0, 0))),
        scratch_shapes=[pltpu.VMEM((1, N_EXPERTS), F32), pltpu.VMEM((1, N_EXPERTS), F32)],
        compiler_params=_params(("arbitrary", "arbitrary")),
        name="router",
    )(logits, tri, upper)


def _route(logits):
    t = logits.shape[0]
    n_slots = t * TOP_K
    n_rows = n_slots + N_EXPERTS * MOE_TILE
    n_blocks = n_rows // MOE_TILE
    dest_w, gates_w, counts = _router(logits)
    counts = counts.reshape(N_EXPERTS).astype(jnp.int32)
    pad_ends = jnp.cumsum((counts + MOE_TILE - 1) // MOE_TILE * MOE_TILE)
    block_start = jnp.arange(n_blocks, dtype=jnp.int32) * MOE_TILE
    block_expert = jnp.minimum(jnp.sum(block_start[:, None] >= pad_ends[None, :], axis=1),
                               N_EXPERTS - 1).astype(jnp.int32)
    n_used = (pad_ends[-1] // MOE_TILE).astype(jnp.int32).reshape(1)
    dest = dest_w[:t, :TOP_K].T.reshape(-1)
    tok = jnp.tile(jnp.arange(t, dtype=jnp.int32), TOP_K)
    _, tok_sorted = lax.sort((dest, tok), num_keys=1)
    padded = (counts + MOE_TILE - 1) // MOE_TILE * MOE_TILE
    pad_starts = pad_ends - padded
    starts = jnp.cumsum(counts) - counts
    rows = jnp.arange(n_rows, dtype=jnp.int32)
    filled_end = jnp.repeat((pad_starts + counts)[block_expert], MOE_TILE)
    compact = rows + jnp.repeat((starts - pad_starts)[block_expert], MOE_TILE)
    tok_rows = jnp.where(rows < filled_end, tok_sorted[jnp.clip(compact, 0, n_slots - 1)], rows % t)
    return dest, tok_rows, gates_w, block_expert, n_used


def _combine_kernel(x_ref, y0_ref, y1_ref, y2_ref, y3_ref, gt_ref, gf_ref, g_ref, o_ref, *, final):
    gt = gt_ref[...]
    moe = (gt[:, 0:1] * y0_ref[...].astype(F32) + gt[:, 1:2] * y1_ref[...].astype(F32)
           + gt[:, 2:3] * y2_ref[...].astype(F32) + gt[:, 3:4] * y3_ref[...].astype(F32))
    x_new = x_ref[...] + gf_ref[0] * moe
    if final:
        x_new = _norm_mod(x_new, g_ref[...], None, None)
    o_ref[...] = x_new


def _combine(x, y_slots, gates, gate_f, g_final, seq, final):
    t, d = x.shape
    tpb = seq // ROW_TILE
    nt = t // ROW_TILE
    assert TOP_K == 4
    return pl.pallas_call(
        functools.partial(_combine_kernel, final=final),
        out_shape=jax.ShapeDtypeStruct((t, d), F32),
        grid=(nt,),
        in_specs=[pl.BlockSpec((ROW_TILE, d), lambda i: (i, 0))] + [
            pl.BlockSpec((ROW_TILE, d), functools.partial(lambda i, k: (k * nt + i, 0), k=k))
            for k in range(TOP_K)] + [
            pl.BlockSpec((ROW_TILE, 128), lambda i: (i, 0)),
            pl.BlockSpec((1, 1, d), lambda i: (i // tpb, 0, 0)),
            pl.BlockSpec((1, d), lambda i: (0, 0)),
        ],
        out_specs=pl.BlockSpec((ROW_TILE, d), lambda i: (i, 0)),
        compiler_params=_params(("parallel",)),
        name="combine_final" if final else "combine",
    )(x, y_slots, y_slots, y_slots, y_slots, gates, gate_f, g_final.reshape(1, d))


def _moe(x, h, logits, gate_f, w_gu, b_gu, w_down, b_down, g_final, seq, layer, final):
    t, d = x.shape
    dest, tok_rows, gates, block_expert, n_used = _route(logits)
    ybuf = _experts(h, tok_rows, w_gu, b_gu, w_down, b_down, block_expert, n_used, layer)
    y_slots = jnp.take(ybuf, dest, axis=0, mode="clip")
    return _combine(x, y_slots, gates, gate_f, g_final, seq, final)


def kernel(x, c, ada_w, ada_b, norm_mix_g, norm_ffn_g, a_w_qkv, a_w_o, rel_bias, kv_norm_g, w_kvf, b_f,
           b_w_q, b_w_o, router_w, router_b, w_gu, b_gu, w_down, b_down, final_norm_g):
    n_batch, seq, d = x.shape
    t = n_batch * seq
    width = N_HEADS * HEAD_DIM
    xf = x.reshape(t, d)

    ada = _ada(c, ada_w, ada_b)
    mods = [[ada[l, :, i * d:(i + 1) * d].reshape(n_batch, 1, d) for i in range(6)] for l in range(2)]

    shift_m, scale_m, gate_m, shift_f, scale_f, gate_f = mods[0]
    qscale = np.ones((3, 3, 1), np.float32)
    qscale[:, 0] = HEAD_DIM ** -0.5
    w_qkv = (a_w_qkv[0].reshape(d, 3, 3, width) * qscale).astype(BF16)

    qi = np.arange(DIL_BLOCK, dtype=np.int32)[:, None]
    kj = np.arange(2 * DIL_BLOCK, dtype=np.int32)[None, :]
    delta = qi + DIL_BLOCK - kj
    in_band = (delta >= 0) & (delta <= DIL_BLOCK)
    outs, lses = [], []
    for g, (window, dilation) in enumerate(DIL_PAIRS):
        assert window // dilation == DIL_BLOCK
        qkv = _proj(xf, norm_mix_g[0], shift_m, scale_m, w_qkv[:, g].reshape(d, 3 * width), seq, dilation, g)
        bucket = _t5_bucket_np(np.clip(delta, 0, None) * dilation)
        tab = rel_bias[:, g * N_HEADS:(g + 1) * N_HEADS].astype(F32)
        onehot = jnp.asarray(np.eye(NUM_BUCKETS, dtype=np.float32)[bucket])
        bias = jnp.einsum('ijb,bh->hij', onehot, tab, precision=lax.Precision.HIGHEST)
        bias = jnp.where(in_band, bias, NEG)
        bias_first = jnp.where(kj >= DIL_BLOCK, bias, NEG)
        o, lse = _dilated_group(qkv, jnp.stack([bias_first, bias]), g, dilation, n_batch, seq)
        outs.append(o)
        lses.append(_unpermute_rows(lse, dilation)[:, :N_HEADS])
    lse_all = jnp.concatenate(lses, axis=1)
    x1, h1, logits1 = _mixer_tail((outs[0], outs[1], outs[2], lse_all), a_w_o[0].astype(BF16), xf, gate_m,
                                  norm_ffn_g[0], shift_f, scale_f, router_w[0], router_b[0], seq, True)
    x2 = _moe(x1, h1, logits1, gate_f, w_gu, b_gu, w_down, b_down, final_norm_g, seq, 0, False)

    w_f = jnp.pad(jnp.tile(w_kvf[:, 2 * width:], (1, 3)), ((0, 0), (0, 128 - 3 * N_HEADS))).astype(BF16)
    b_fp = jnp.pad(jnp.tile(b_f, 3), (0, 128 - 3 * N_HEADS)).reshape(1, 128)
    k_aug, v_sh, fcum = _kvf(x2, kv_norm_g, _aug_weight(w_kvf[:, :width]), w_kvf[:, width:2 * width].astype(BF16),
                             w_f, b_fp, n_batch)

    shift_m, scale_m, gate_m, shift_f, scale_f, gate_f = mods[1]
    q_aug = _qaug(x2, norm_mix_g[1], shift_m, scale_m, _aug_weight(b_w_q[0] * (HEAD_DIM ** -0.5 * LOG2E)), fcum, seq)
    o1 = _fox(q_aug, k_aug, v_sh, n_batch, seq)

    x3, h3, logits3 = _mixer_tail((o1,), b_w_o[0].astype(BF16), x2, gate_m, norm_ffn_g[1], shift_f, scale_f,
                                  router_w[1], router_b[1], seq, False)
    out = _moe(x3, h3, logits3, gate_f, w_gu, b_gu, w_down, b_down, final_norm_g, seq, 1, True)
    return out.reshape(n_batch, seq, d)
```

```python
import functools
import math

import numpy as np
import jax
import jax.numpy as jnp
from jax import lax
from jax.experimental import pallas as pl
from jax.experimental.pallas import tpu as pltpu

F32 = jnp.float32
BF16 = jnp.bfloat16

D_MODEL = 1024
HEAD_DIM = 64
N_HEADS = 16
DIL_PAIRS = ((128, 1), (512, 4), (2048, 16))
DIL_BLOCK = 128
NUM_BUCKETS = 32
MAX_DISTANCE = 2048
N_EXPERTS = 32
TOP_K = 4
SWIGLU_LIMIT = 7.0
SWIGLU_ALPHA = 1.702
RMS_EPS = 1e-6
NEG = -1e30
LOG2E = math.log2(math.e)

ROW_TILE = 512
MOE_TILE = 512
FOX_TILE = 512
AUG = 128
FOX_SUM_ROWS = 8
VMEM_LIMIT = 56 * 1024 * 1024


def _params(sem, vmem=VMEM_LIMIT):
    return pltpu.CompilerParams(dimension_semantics=sem, vmem_limit_bytes=vmem)


def _dot(a, b):
    return jnp.dot(a, b, preferred_element_type=F32)


def _dot_nt(a, b):
    return lax.dot_general(a, b, (((1,), (1,)), ((), ())), preferred_element_type=F32)


def _dot_tn(a, b):
    return lax.dot_general(a, b, (((0,), (0,)), ((), ())), preferred_element_type=F32)


def _norm_mod(x, g, shift, scale):
    ms = jnp.mean(x * x, axis=-1, keepdims=True)
    y = x * lax.rsqrt(ms + RMS_EPS) * g
    if scale is not None:
        y = y * (1.0 + scale) + shift
    return y


def _ada_kernel(c_ref, w_ref, b_ref, o_ref):
    c = c_ref[...]
    act = c * (1.0 / (1.0 + jnp.exp(-c)))
    o_ref[0] = jnp.dot(act, w_ref[0], preferred_element_type=F32,
                       precision=lax.Precision.HIGHEST) + b_ref[0]


def _ada(c, ada_w, ada_b):
    depth, d, n = ada_w.shape
    bsz = c.shape[0]
    tn = 1536
    return pl.pallas_call(
        _ada_kernel,
        out_shape=jax.ShapeDtypeStruct((depth, bsz, n), F32),
        grid=(depth, n // tn),
        in_specs=[
            pl.BlockSpec((bsz, d), lambda l, j: (0, 0)),
            pl.BlockSpec((1, d, tn), lambda l, j: (l, 0, j)),
            pl.BlockSpec((1, 1, tn), lambda l, j: (l, 0, j)),
        ],
        out_specs=pl.BlockSpec((1, bsz, tn), lambda l, j: (l, 0, j)),
        compiler_params=_params(("parallel", "parallel")),
        name="ada",
    )(c, ada_w, ada_b.reshape(depth, 1, n))


def _perm_tile(dilation):
    return max(ROW_TILE, DIL_BLOCK * dilation)


def _proj_kernel(x_ref, g_ref, sh_ref, sc_ref, w_ref, o_ref, h_scr, *xs_scr, dilation, single_step):
    def fill():
        if dilation == 1:
            h_scr[...] = _norm_mod(x_ref[...], g_ref[...], sh_ref[0], sc_ref[0]).astype(BF16)
        else:
            (xs,) = xs_scr
            n_lane = xs.shape[0]
            for c in range(n_lane):
                xs[c] = x_ref[:, c * 128:(c + 1) * 128]
            chunk = x_ref.shape[0] // dilation
            for r in range(dilation):
                xr = jnp.concatenate([xs[c, pl.ds(r, chunk, stride=dilation), :] for c in range(n_lane)], axis=1)
                h_scr[r * chunk:(r + 1) * chunk, :] = _norm_mod(xr, g_ref[...], sh_ref[0], sc_ref[0]).astype(BF16)

    if single_step:
        fill()
    else:
        pl.when(pl.program_id(1) == 0)(fill)

    o_ref[...] = _dot(h_scr[...], w_ref[...]).astype(o_ref.dtype)


def _proj(x, g, shift, scale, w, seq, dilation, group):
    t, d = x.shape
    n = w.shape[1]
    tm = _perm_tile(dilation)
    single_step = tm == ROW_TILE
    tn = n if single_step else d
    tpb = seq // tm
    return pl.pallas_call(
        functools.partial(_proj_kernel, dilation=dilation, single_step=single_step),
        out_shape=jax.ShapeDtypeStruct((t, n), BF16),
        grid=(t // tm, n // tn),
        in_specs=[
            pl.BlockSpec((tm, d), lambda i, j: (i, 0)),
            pl.BlockSpec((1, d), lambda i, j: (0, 0)),
            pl.BlockSpec((1, 1, d), lambda i, j: (i // tpb, 0, 0)),
            pl.BlockSpec((1, 1, d), lambda i, j: (i // tpb, 0, 0)),
            pl.BlockSpec((d, tn), lambda i, j: (0, j)),
        ],
        out_specs=pl.BlockSpec((tm, tn), lambda i, j: (i, j)),
        scratch_shapes=[pltpu.VMEM((tm, d), BF16)] + (
            [pltpu.VMEM((d // 128, tm, 128), F32)] if dilation > 1 else []),
        compiler_params=_params(("parallel", "arbitrary")),
        name=f"proj{group}",
    )(x, g.reshape(1, d), shift, scale, w)


def _split3(f):
    def top(v):
        return lax.bitcast_convert_type(lax.bitcast_convert_type(v, jnp.uint32) & jnp.uint32(0xFFFF0000), F32)
    hi = top(f)
    r1 = f - hi
    mid = top(r1)
    return hi, mid, r1 - mid


def _forget_pieces(fc):
    hi, mid, lo = _split3(fc)
    lane = lax.broadcasted_iota(jnp.int32, fc.shape, 1)
    x = jnp.where(lane < N_HEADS, hi, jnp.where(lane < 2 * N_HEADS, mid, jnp.where(lane < 3 * N_HEADS, lo, 0.0)))
    return x.astype(BF16)


def _aug_tables(key_side):
    place = np.zeros((128, N_HEADS * AUG), np.float32)
    const = np.zeros((1, N_HEADS * AUG), np.float32)
    for h in range(N_HEADS):
        base = h * AUG + HEAD_DIM
        for piece in range(3):
            if key_side:
                place[piece * N_HEADS + h, base + piece] = 1.0
                const[0, base + 3 + piece] = 1.0
            else:
                place[piece * N_HEADS + h, base + 3 + piece] = 1.0
                const[0, base + piece] = -1.0
    return jnp.asarray(place, BF16), jnp.asarray(const, F32)


def _aug_weight(w):
    d = w.shape[0]
    w = jnp.pad(w.reshape(d, N_HEADS, HEAD_DIM), ((0, 0), (0, 0), (0, AUG - HEAD_DIM)))
    return w.reshape(d, N_HEADS * AUG).astype(BF16)


def _kvf_kernel(x_ref, g_ref, wk_ref, wv_ref, wf_ref, bf_ref, tri_ref, pl_ref, cr_ref,
                k_ref, v_ref, f_ref, carry, *, tpb):
    h = _norm_mod(x_ref[...], g_ref[...], None, None).astype(BF16)
    v_ref[...] = _dot(h, wv_ref[...]).astype(v_ref.dtype)
    z = _dot(h, wf_ref[...]) + bf_ref[...]
    lf = jnp.minimum(z, 0.0) - jnp.log(1.0 + jnp.exp(-jnp.abs(z)))
    hi, mid, lo = _split3(lf)
    tri = tri_ref[...]
    cs = _dot(tri, hi.astype(BF16)) + _dot(tri, mid.astype(BF16)) + _dot(tri, lo.astype(BF16))

    @pl.when(pl.program_id(0) % tpb == 0)
    def _():
        carry[...] = jnp.zeros_like(carry)

    cs = cs + carry[...]
    carry[...] = cs[ROW_TILE - 1:ROW_TILE, :]
    f2 = cs * LOG2E
    f_ref[...] = f2
    k_ref[...] = (_dot(h, wk_ref[...]) + _dot(_forget_pieces(f2), pl_ref[...]) + cr_ref[...]).astype(k_ref.dtype)


def _kvf(x, g, w_k, w_v, w_f, b_f, n_batch):
    t, d = x.shape
    n_aug = N_HEADS * AUG
    tri = jnp.asarray(np.tril(np.ones((ROW_TILE, ROW_TILE), np.float32)), BF16)
    place, const = _aug_tables(True)
    assert (t // ROW_TILE) % n_batch == 0
    full = lambda i: (0, 0)
    row = lambda i: (i, 0)
    return pl.pallas_call(
        functools.partial(_kvf_kernel, tpb=t // ROW_TILE // n_batch),
        out_shape=(jax.ShapeDtypeStruct((t, n_aug), BF16), jax.ShapeDtypeStruct((t, d), BF16),
                   jax.ShapeDtypeStruct((t, 128), F32)),
        grid=(t // ROW_TILE,),
        in_specs=[
            pl.BlockSpec((ROW_TILE, d), row),
            pl.BlockSpec((1, d), full),
            pl.BlockSpec((d, n_aug), full),
            pl.BlockSpec((d, d), full),
            pl.BlockSpec((d, 128), full),
            pl.BlockSpec((1, 128), full),
            pl.BlockSpec((ROW_TILE, ROW_TILE), full),
            pl.BlockSpec((128, n_aug), full),
            pl.BlockSpec((1, n_aug), full),
        ],
        out_specs=(pl.BlockSpec((ROW_TILE, n_aug), row), pl.BlockSpec((ROW_TILE, d), row),
                   pl.BlockSpec((ROW_TILE, 128), row)),
        scratch_shapes=[pltpu.VMEM((1, 128), F32)],
        compiler_params=_params(("arbitrary",)),
        name="kvf",
    )(x, g.reshape(1, d), w_k, w_v, w_f, b_f, tri, place, const)


def _qaug_kernel(x_ref, g_ref, sh_ref, sc_ref, w_ref, f_ref, pl_ref, cr_ref, o_ref):
    h = _norm_mod(x_ref[...], g_ref[...], sh_ref[0], sc_ref[0]).astype(BF16)
    o_ref[...] = (_dot(h, w_ref[...]) + _dot(_forget_pieces(f_ref[...]), pl_ref[...])
                  + cr_ref[...]).astype(o_ref.dtype)


def _qaug(x, g, shift, scale, w, fcum, seq):
    t, d = x.shape
    n_aug = N_HEADS * AUG
    tpb = seq // ROW_TILE
    place, const = _aug_tables(False)
    full = lambda i: (0, 0)
    row = lambda i: (i, 0)
    per_b = lambda i: (i // tpb, 0, 0)
    return pl.pallas_call(
        _qaug_kernel,
        out_shape=jax.ShapeDtypeStruct((t, n_aug), BF16),
        grid=(t // ROW_TILE,),
        in_specs=[
            pl.BlockSpec((ROW_TILE, d), row),
            pl.BlockSpec((1, d), full),
            pl.BlockSpec((1, 1, d), per_b),
            pl.BlockSpec((1, 1, d), per_b),
            pl.BlockSpec((d, n_aug), full),
            pl.BlockSpec((ROW_TILE, 128), row),
            pl.BlockSpec((128, n_aug), full),
            pl.BlockSpec((1, n_aug), full),
        ],
        out_specs=pl.BlockSpec((ROW_TILE, n_aug), row),
        compiler_params=_params(("parallel",)),
        name="qaug",
    )(x, g.reshape(1, d), shift, scale, w, fcum, place, const)


def _dil_kernel(q_ref, kp_ref, kc_ref, vp_ref, vc_ref, bias_ref, o_ref, lse_ref):
    table = jnp.minimum(pl.program_id(2), 1)
    pair_w = 2 * HEAD_DIM
    lane = lax.broadcasted_iota(jnp.int32, (DIL_BLOCK, pair_w), 1)
    first = lane < HEAD_DIM
    first2 = lax.broadcasted_iota(jnp.int32, (2 * DIL_BLOCK, pair_w), 1) < HEAD_DIM
    lse_tile = jnp.zeros((DIL_BLOCK, pair_w), F32)
    zero = jnp.zeros((), BF16)
    n_pairs = N_HEADS // 2
    slices = [slice(pair * pair_w, (pair + 1) * pair_w) for pair in range(n_pairs)]
    scores = []
    for pair, sl in enumerate(slices):
        q = q_ref[:, sl]
        k2 = jnp.concatenate([kp_ref[:, sl], kc_ref[:, sl]], axis=0)
        for which in range(2):
            qh = jnp.where(first, q, zero) if which == 0 else jnp.where(first, zero, q)
            scores.append(_dot_nt(qh, k2) + bias_ref[table, 2 * pair + which])
    probs, invs = [], []
    for h, s in enumerate(scores):
        m = jnp.max(s, axis=-1, keepdims=True)
        p = jnp.exp(s - m)
        den = jnp.sum(p, axis=-1, keepdims=True)
        probs.append(p.astype(BF16))
        invs.append(1.0 / den)
        lse_tile = jnp.where(lane == h, m + jnp.log(den), lse_tile)
    for pair, sl in enumerate(slices):
        v2 = jnp.concatenate([vp_ref[:, sl], vc_ref[:, sl]], axis=0)
        vcat = jnp.concatenate([jnp.where(first2, v2, zero), jnp.where(first2, zero, v2)], axis=0)
        o = _dot(jnp.concatenate(probs[2 * pair:2 * pair + 2], axis=1), vcat)
        o_ref[:, sl] = (o * jnp.where(first, invs[2 * pair], invs[2 * pair + 1])).astype(o_ref.dtype)
    lse_ref[...] = lse_tile


def _dilated_group(qkv, bias, group, dilation, n_batch, seq):
    t = qkv.shape[0]
    width = N_HEADS * HEAD_DIM
    nb = seq // dilation // DIL_BLOCK
    bpb = seq // DIL_BLOCK

    def blk(b, r, n):
        return b * bpb + n * dilation + r

    def spec(part, prev):
        if prev:
            return pl.BlockSpec((DIL_BLOCK, width), lambda b, r, n: (blk(b, r, jnp.maximum(n - 1, 0)), part))
        return pl.BlockSpec((DIL_BLOCK, width), lambda b, r, n: (blk(b, r, n), part))

    return pl.pallas_call(
        _dil_kernel,
        out_shape=(jax.ShapeDtypeStruct((t, width), BF16), jax.ShapeDtypeStruct((t, 2 * HEAD_DIM), F32)),
        grid=(n_batch, dilation, nb),
        in_specs=[spec(0, False), spec(1, True), spec(1, False), spec(2, True), spec(2, False),
                  pl.BlockSpec((2, N_HEADS, DIL_BLOCK, 2 * DIL_BLOCK), lambda b, r, n: (0, 0, 0, 0))],
        out_specs=(pl.BlockSpec((DIL_BLOCK, width), lambda b, r, n: (blk(b, r, n), 0)),
                   pl.BlockSpec((DIL_BLOCK, 2 * HEAD_DIM), lambda b, r, n: (blk(b, r, n), 0))),
        compiler_params=_params(("parallel", "parallel", "arbitrary")),
        name=f"dilated{group}",
    )(qkv, qkv, qkv, qkv, qkv, bias)


def _unpermute_rows(a, dilation):
    if dilation == 1:
        return a
    t, c = a.shape
    tm = _perm_tile(dilation)
    return a.reshape(t // tm, dilation, tm // dilation, c).transpose(0, 2, 1, 3).reshape(t, c)


def _perm_matrix(rows, dilation):
    per = rows // dilation
    p = np.zeros((rows, rows), np.float32)
    for r in range(dilation):
        for n in range(per):
            p[n * dilation + r, r * per + n] = 1.0
    return jnp.asarray(p, BF16)


def _t5_bucket_np(n):
    max_exact = NUM_BUCKETS // 2
    nf = np.maximum(n, 1).astype(np.float32)
    large = max_exact + (np.log(nf / np.float32(max_exact)) / np.float32(math.log(MAX_DISTANCE / max_exact))
                         * np.float32(NUM_BUCKETS - max_exact)).astype(np.int32)
    large = np.minimum(large, NUM_BUCKETS - 1)
    return np.where(n < max_exact, n, large)


def _mix_tail(mix_in, wo_ref, x_ref, gm_ref, g_ref, sh_ref, sc_ref, wr_ref, br_ref,
              x_out, h_out, lg_out):
    mix = _dot(mix_in, wo_ref[...])
    x_new = x_ref[...] + gm_ref[0] * mix
    x_out[...] = x_new
    h = _norm_mod(x_new, g_ref[...], sh_ref[0], sc_ref[0])
    bits = pltpu.bitcast(h.astype(BF16).astype(F32), jnp.uint32)
    half = h.shape[1] // 2
    h_out[...] = (bits[:, :half] >> 16) | (bits[:, half:] & jnp.uint32(0xFFFF0000))
    h_hi, h_mid, _ = _split3(h)
    h_hi = h_hi.astype(BF16)
    lg_out[...] = (_dot(h_hi, wr_ref[0]) + (_dot(h_mid.astype(BF16), wr_ref[0]) + _dot(h_hi, wr_ref[1]))
                   + br_ref[...])


def _merge_oproj_kernel(o0_ref, o1_ref, o2_ref, p1_ref, p2_ref, lse_ref, ex_ref, wo_ref, x_ref, gm_ref, g_ref,
                        sh_ref, sc_ref, wr_ref, br_ref, x_out, h_out, lg_out):
    lse = lse_ref[...]
    l0, l1, l2 = lse[:, 0:16], lse[:, 16:32], lse[:, 32:48]
    m = jnp.maximum(jnp.maximum(l0, l1), l2)
    e0, e1, e2 = jnp.exp(l0 - m), jnp.exp(l1 - m), jnp.exp(l2 - m)
    inv = 1.0 / (e0 + e1 + e2)
    ex = ex_ref[...]
    o1 = _dot(p1_ref[...], o1_ref[...])
    o2 = _dot(p2_ref[...], o2_ref[0].reshape(ROW_TILE, o2_ref.shape[-1]))
    merged = (_dot((e0 * inv).astype(BF16), ex) * o0_ref[...].astype(F32)
              + _dot((e1 * inv).astype(BF16), ex) * o1
              + _dot((e2 * inv).astype(BF16), ex) * o2)
    _mix_tail(merged.astype(BF16), wo_ref, x_ref, gm_ref, g_ref, sh_ref, sc_ref, wr_ref, br_ref,
              x_out, h_out, lg_out)


def _oproj_kernel(o_ref, wo_ref, x_ref, gm_ref, g_ref, sh_ref, sc_ref, wr_ref, br_ref,
                  x_out, h_out, lg_out):
    _mix_tail(o_ref[...], wo_ref, x_ref, gm_ref, g_ref, sh_ref, sc_ref, wr_ref, br_ref,
              x_out, h_out, lg_out)


def _mixer_tail(mix_inputs, w_o, x, gate_m, g_ffn, shift_f, scale_f, w_r, b_r, seq, merged):
    t, d = x.shape
    tpb = seq // ROW_TILE
    row = lambda i: (i, 0)
    full = lambda i: (0, 0)
    per_b = lambda i: (i // tpb, 0, 0)
    tail_specs = [
        pl.BlockSpec((d, d), full),
        pl.BlockSpec((ROW_TILE, d), row),
        pl.BlockSpec((1, 1, d), per_b),
        pl.BlockSpec((1, d), full),
        pl.BlockSpec((1, 1, d), per_b),
        pl.BlockSpec((1, 1, d), per_b),
        pl.BlockSpec((2, d, N_EXPERTS), lambda i: (0, 0, 0)),
        pl.BlockSpec((1, N_EXPERTS), full),
    ]
    w_hi, w_mid, _ = _split3(w_r)
    tail_args = (w_o, x, gate_m, g_ffn.reshape(1, d), shift_f, scale_f, jnp.stack([w_hi, w_mid]).astype(BF16),
                 b_r.reshape(1, N_EXPERTS))
    if merged:
        o0, o1, o2, lse = mix_inputs
        expand = jnp.asarray(np.kron(np.eye(N_HEADS, dtype=np.float32),
                                     np.ones((1, HEAD_DIM), np.float32)), BF16)
        d1, d2 = DIL_PAIRS[1][1], DIL_PAIRS[2][1]
        assert _perm_tile(d1) == ROW_TILE
        t2 = _perm_tile(d2)
        sub2 = ROW_TILE // d2
        o2v = o2.reshape(t // t2, d2, t2 // d2, d)
        per2 = t2 // ROW_TILE
        kern = _merge_oproj_kernel
        specs = [pl.BlockSpec((ROW_TILE, d), row), pl.BlockSpec((ROW_TILE, d), row),
                 pl.BlockSpec((1, d2, sub2, d), lambda i: (i // per2, 0, i % per2, 0)),
                 pl.BlockSpec((ROW_TILE, ROW_TILE), full), pl.BlockSpec((ROW_TILE, ROW_TILE), full),
                 pl.BlockSpec((ROW_TILE, 3 * N_HEADS), row), pl.BlockSpec((N_HEADS, d), full)]
        args = (o0, o1, o2v, _perm_matrix(ROW_TILE, d1), _perm_matrix(ROW_TILE, d2), lse, expand)
    else:
        kern = _oproj_kernel
        specs = [pl.BlockSpec((ROW_TILE, d), row)]
        args = mix_inputs
    return pl.pallas_call(
        kern,
        out_shape=(jax.ShapeDtypeStruct((t, d), F32), jax.ShapeDtypeStruct((t, d // 2), jnp.uint32),
                   jax.ShapeDtypeStruct((t, N_EXPERTS), F32)),
        grid=(t // ROW_TILE,),
        in_specs=specs + tail_specs,
        out_specs=(pl.BlockSpec((ROW_TILE, d), row), pl.BlockSpec((ROW_TILE, d // 2), row),
                   pl.BlockSpec((ROW_TILE, N_EXPERTS), row)),
        compiler_params=_params(("parallel",)),
        name="mixer_tail_merge" if merged else "mixer_tail",
    )(*args, *tail_args)


def _fox_kernel(q_ref, k_ref, v_ref, o_ref, vt_scr):
    qi = pl.program_id(2)
    seq = k_ref.shape[0]
    nk = seq // FOX_TILE
    half = HEAD_DIM

    ext = FOX_SUM_ROWS
    @pl.when(qi == 0)
    def _():
        lane = lax.broadcasted_iota(jnp.int32, (FOX_TILE, 2 * half), 1)
        erow = lax.broadcasted_iota(jnp.int32, (2 * ext, 2 * FOX_TILE), 0)
        ecol = lax.broadcasted_iota(jnp.int32, (2 * ext, 2 * FOX_TILE), 1)
        sum_rows = jnp.where((erow < ext) == (ecol < FOX_TILE), 1.0, 0.0).astype(BF16)
        for j in range(nk):
            vj = v_ref[j * FOX_TILE:(j + 1) * FOX_TILE, :].astype(F32)
            vt_scr[j, :2 * half, :FOX_TILE] = jnp.where(lane < half, vj, 0.0).T.astype(BF16)
            vt_scr[j, :2 * half, FOX_TILE:] = jnp.where(lane >= half, vj, 0.0).T.astype(BF16)
            vt_scr[j, 2 * half:, :] = sum_rows

    q0 = q_ref[:, :AUG]
    q1 = q_ref[:, AUG:]

    def softmax_step(s, m):
        m_new = jnp.maximum(m, jnp.max(s, axis=0, keepdims=True))
        return m_new, jnp.exp2(m - m_new), jnp.exp2((s - m_new).astype(BF16))

    def step(kj, carry, masked):
        m0, m1, acc = carry
        rows = pl.ds(pl.multiple_of(kj * FOX_TILE, FOX_TILE), FOX_TILE)
        s0 = _dot_nt(k_ref[rows, :AUG], q0)
        s1 = _dot_nt(k_ref[rows, AUG:], q1)
        if masked:
            kpos = lax.broadcasted_iota(jnp.int32, (FOX_TILE, FOX_TILE), 0)
            qpos = lax.broadcasted_iota(jnp.int32, (FOX_TILE, FOX_TILE), 1)
            keep = kpos <= qpos
            s0 = jnp.where(keep, s0, NEG)
            s1 = jnp.where(keep, s1, NEG)
        m0, a0, p0 = softmax_step(s0, m0)
        m1, a1, p1 = softmax_step(s1, m1)
        pv = _dot(vt_scr[kj], jnp.concatenate([p0, p1], axis=0))
        alpha = jnp.concatenate([jnp.broadcast_to(a0, (half, FOX_TILE)), jnp.broadcast_to(a1, (half, FOX_TILE)),
                                 jnp.broadcast_to(a0, (ext, FOX_TILE)), jnp.broadcast_to(a1, (ext, FOX_TILE))],
                                axis=0)
        return m0, m1, alpha * acc + pv

    neg = jnp.full((1, FOX_TILE), NEG, F32)
    init = (neg, neg, jnp.zeros((2 * half + 2 * ext, FOX_TILE), F32))
    carry = lax.fori_loop(0, qi, lambda kj, c: step(kj, c, False), init)
    _, _, acc = step(qi, carry, True)
    inv0 = 1.0 / acc[2 * half:2 * half + 1, :]
    inv1 = 1.0 / acc[2 * half + ext:2 * half + ext + 1, :]
    inv = jnp.concatenate([jnp.broadcast_to(inv0, (half, FOX_TILE)),
                           jnp.broadcast_to(inv1, (half, FOX_TILE))], axis=0)
    o_ref[...] = (acc[:2 * half, :] * inv).T.astype(o_ref.dtype)


def _fox(q_aug, k_aug, v, n_batch, seq):
    t = q_aug.shape[0]
    nq = seq // FOX_TILE
    pairs = N_HEADS // 2
    return pl.pallas_call(
        _fox_kernel,
        out_shape=jax.ShapeDtypeStruct((t, N_HEADS * HEAD_DIM), BF16),
        grid=(n_batch, pairs, nq),
        in_specs=[
            pl.BlockSpec((FOX_TILE, 2 * AUG), lambda b, p, qi: (b * nq + qi, p)),
            pl.BlockSpec((seq, 2 * AUG), lambda b, p, qi: (b, p)),
            pl.BlockSpec((seq, 2 * HEAD_DIM), lambda b, p, qi: (b, p)),
        ],
        out_specs=pl.BlockSpec((FOX_TILE, 2 * HEAD_DIM), lambda b, p, qi: (b * nq + qi, p)),
        scratch_shapes=[pltpu.VMEM((nq, 2 * HEAD_DIM + 2 * FOX_SUM_ROWS, 2 * FOX_TILE), BF16)],
        compiler_params=_params(("parallel", "parallel", "arbitrary")),
        name="fox",
    )(q_aug, k_aug, v)


def _expert_kernel(be_ref, nu_ref, tok_ref, h_hbm, wgu_ref, bgu_ref, wd_ref, bd_ref, o_ref,
                   wgu_bf, wd_bf, xbuf_a, xbuf_b, gsem):
    i = pl.program_id(0)
    d_ff = wd_ref.shape[1]
    odd = lax.rem(i, 2) == 1
    pairs = MOE_TILE // 2
    bufs = ((xbuf_a, gsem.at[0]), (xbuf_b, gsem.at[1]))

    def issue_gather(blk, buf, sem):
        base = blk * pairs
        for c in range(pairs):
            word = tok_ref[base + c]
            for r, tok in ((2 * c, word & 0xFFFF), (2 * c + 1, lax.shift_right_logical(word, 16))):
                pltpu.async_copy(h_hbm.at[pl.ds(tok, 1), :], buf.at[pl.ds(r, 1), :], sem, priority=r % 2)

    def wait_gather(buf, sem):
        pltpu.make_async_copy(h_hbm.at[pl.ds(0, MOE_TILE), :], buf, sem).wait()

    @pl.when(i == 0)
    def _():
        issue_gather(0, *bufs[0])

    @pl.when(jnp.logical_or(i == 0, be_ref[i] != be_ref[jnp.maximum(i - 1, 0)]))
    def _():
        wgu_bf[...] = wgu_ref[0].astype(BF16)
        wd_bf[...] = wd_ref[0].astype(BF16)

    def block(cur, nxt):
        @pl.when(i <= nu_ref[0])
        def _():
            wait_gather(*cur)

        @pl.when(i < nu_ref[0])
        def _():
            issue_gather(i + 1, *nxt)
            words = cur[0][...]
            x = jnp.concatenate([pltpu.bitcast(words << 16, F32),
                                 pltpu.bitcast(words & jnp.uint32(0xFFFF0000), F32)], axis=1).astype(BF16)
            gu = _dot(x, wgu_bf[...]) + bgu_ref[0]
            g = jnp.minimum(gu[:, :d_ff], SWIGLU_LIMIT)
            u = jnp.clip(gu[:, d_ff:], -SWIGLU_LIMIT, SWIGLU_LIMIT)
            act = (u + 1.0) * g * (1.0 / (1.0 + jnp.exp(-SWIGLU_ALPHA * g)))
            y = _dot(act.astype(BF16), wd_bf[...]) + bd_ref[0]
            o_ref[...] = y.astype(o_ref.dtype)

    @pl.when(jnp.logical_not(odd))
    def _():
        block(bufs[0], bufs[1])

    @pl.when(odd)
    def _():
        block(bufs[1], bufs[0])

    @pl.when(i >= nu_ref[0])
    def _():
        o_ref[...] = jnp.zeros_like(o_ref)


def _experts(h, tok_rows, w_gu, b_gu, w_down, b_down, block_expert, n_used, layer):
    t = h.shape[0]
    d = w_gu.shape[2]
    n_rows = tok_rows.shape[0]
    depth, e, _, n_gu = w_gu.shape
    d_ff = w_down.shape[2]
    n_blocks = n_rows // MOE_TILE
    assert t <= 65536
    packed = tok_rows[0::2] | (tok_rows[1::2] << 16)
    grid_spec = pltpu.PrefetchScalarGridSpec(
        num_scalar_prefetch=3,
        grid=(n_blocks,),
        in_specs=[
            pl.BlockSpec(memory_space=pl.ANY),
            pl.BlockSpec((None, 1, d, n_gu), lambda i, be, nu, tk: (layer, be[i], 0, 0)),
            pl.BlockSpec((None, 1, 1, n_gu), lambda i, be, nu, tk: (layer, be[i], 0, 0)),
            pl.BlockSpec((None, 1, d_ff, d), lambda i, be, nu, tk: (layer, be[i], 0, 0)),
            pl.BlockSpec((None, 1, 1, d), lambda i, be, nu, tk: (layer, be[i], 0, 0)),
        ],
        out_specs=pl.BlockSpec((MOE_TILE, d), lambda i, be, nu, tk: (i, 0)),
        scratch_shapes=[pltpu.VMEM((d, n_gu), BF16), pltpu.VMEM((d_ff, d), BF16),
                        pltpu.VMEM((MOE_TILE, d // 2), jnp.uint32), pltpu.VMEM((MOE_TILE, d // 2), jnp.uint32),
                        pltpu.SemaphoreType.DMA((2,))],
    )
    return pl.pallas_call(
        _expert_kernel,
        out_shape=jax.ShapeDtypeStruct((n_rows, d), BF16),
        grid_spec=grid_spec,
        compiler_params=_params(("arbitrary",)),
        name="experts",
    )(block_expert, n_used, packed, h, w_gu, b_gu.reshape(depth, e, 1, n_gu), w_down,
      b_down.reshape(depth, e, 1, d))


def _router_kernel(lg_ref, tri_ref, upper_ref, dest_ref, gate_ref, cnt_ref, carry, base):
    phase = pl.program_id(0)
    i = pl.program_id(1)
    tm = lg_ref.shape[0]
    lane = lax.broadcasted_iota(jnp.int32, (tm, N_EXPERTS), 1).astype(F32)
    wide = lax.broadcasted_iota(jnp.int32, (tm, 128), 1)

    work = lg_ref[...]
    hots, vals = [], []
    for _ in range(TOP_K):
        m = jnp.max(work, axis=1, keepdims=True)
        idx = jnp.min(jnp.where(work == m, lane, float(N_EXPERTS)), axis=1, keepdims=True)
        hot = lane == idx
        hots.append(hot)
        vals.append(m)
        work = jnp.where(hot, -jnp.inf, work)
    exps = [jnp.exp(v - vals[0]) for v in vals]
    inv = 1.0 / (exps[0] + exps[1] + exps[2] + exps[3])
    gate_tile = jnp.zeros((tm, 128), F32)
    for k in range(TOP_K):
        gate_tile = jnp.where(wide == k, exps[k] * inv, gate_tile)
    gate_ref[...] = gate_tile

    chosen = jnp.zeros((tm, N_EXPERTS), F32)
    for hot in hots:
        chosen = chosen + hot.astype(F32)

    @pl.when(jnp.logical_and(phase == 0, i == 0))
    def _():
        carry[...] = jnp.zeros_like(carry)
        base[...] = jnp.zeros_like(base)

    @pl.when(jnp.logical_and(phase == 1, i == 0))
    def _():
        counts = carry[...]
        padded = jnp.floor((counts + (MOE_TILE - 1)) * (1.0 / MOE_TILE)) * MOE_TILE
        base[...] = jnp.dot(jnp.broadcast_to(padded, (8, N_EXPERTS)), upper_ref[...],
                            preferred_element_type=F32, precision=lax.Precision.HIGHEST)[0:1, :]
        carry[...] = jnp.zeros_like(carry)

    ahead = _dot(tri_ref[...], chosen.astype(BF16))
    pos = ahead + carry[...] + base[...]
    dest_tile = jnp.zeros((tm, 128), F32)
    for k in range(TOP_K):
        row = jnp.sum(jnp.where(hots[k], pos, 0.0), axis=1, keepdims=True)
        dest_tile = jnp.where(wide == k, row, dest_tile)
    dest_ref[...] = dest_tile.astype(jnp.int32)
    carry[...] = carry[...] + jnp.sum(chosen, axis=0, keepdims=True)
    cnt_ref[...] = carry[...]


def _router(logits):
    t = logits.shape[0]
    nt = t // ROW_TILE
    tri = jnp.asarray(np.tril(np.ones((ROW_TILE, ROW_TILE), np.float32), -1), BF16)
    upper = jnp.asarray(np.triu(np.ones((N_EXPERTS, N_EXPERTS), np.float32), 1), F32)
    return pl.pallas_call(
        _router_kernel,
        out_shape=(jax.ShapeDtypeStruct((t + ROW_TILE, 128), jnp.int32),
                   jax.ShapeDtypeStruct((t + ROW_TILE, 128), F32),
                   jax.ShapeDtypeStruct((1, N_EXPERTS), F32)),
        grid=(2, nt),
        in_specs=[
            pl.BlockSpec((ROW_TILE, N_EXPERTS), lambda p, i: (i, 0)),
            pl.BlockSpec((ROW_TILE, ROW_TILE), lambda p, i: (0, 0)),
            pl.BlockSpec((N_EXPERTS, N_EXPERTS), lambda p, i: (0, 0)),
        ],
        out_specs=(pl.BlockSpec((ROW_TILE, 128), lambda p, i: (p * i + (1 - p) * nt, 0)),
                   pl.BlockSpec((ROW_TILE, 128), lambda p, i: (p * i + (1 - p) * nt, 0)),
                   pl.BlockSpec((1, N_EXPERTS), lambda p, i: (0, 0))),
        scratch_shapes=[pltpu.VMEM((1, N_EXPERTS), F32), pltpu.VMEM((1, N_EXPERTS), F32)],
        compiler_params=_params(("arbitrary", "arbitrary")),
        name="router",
    )(logits, tri, upper)


def _route(logits):
    t = logits.shape[0]
    n_slots = t * TOP_K
    n_rows = n_slots + N_EXPERTS * MOE_TILE
    n_blocks = n_rows // MOE_TILE
    dest_w, gates_w, counts = _router(logits)
    counts = counts.reshape(N_EXPERTS).astype(jnp.int32)
    pad_ends = jnp.cumsum((counts + MOE_TILE - 1) // MOE_TILE * MOE_TILE)
    block_start = jnp.arange(n_blocks, dtype=jnp.int32) * MOE_TILE
    block_expert = jnp.minimum(jnp.sum(block_start[:, None] >= pad_ends[None, :], axis=1),
                               N_EXPERTS - 1).astype(jnp.int32)
    n_used = (pad_ends[-1] // MOE_TILE).astype(jnp.int32).reshape(1)
    dest = dest_w[:t, :TOP_K].T.reshape(-1)
    tok = jnp.tile(jnp.arange(t, dtype=jnp.int32), TOP_K)
    _, tok_sorted = lax.sort((dest, tok), num_keys=1)
    padded = (counts + MOE_TILE - 1) // MOE_TILE * MOE_TILE
    pad_starts = pad_ends - padded
    starts = jnp.cumsum(counts) - counts
    rows = jnp.arange(n_rows, dtype=jnp.int32)
    filled_end = jnp.repeat((pad_starts + counts)[block_expert], MOE_TILE)
    compact = rows + jnp.repeat((starts - pad_starts)[block_expert], MOE_TILE)
    tok_rows = jnp.where(rows < filled_end, tok_sorted[jnp.clip(compact, 0, n_slots - 1)], rows % t)
    return dest, tok_rows, gates_w, block_expert, n_used


def _combine_kernel(x_ref, y0_ref, y1_ref, y2_ref, y3_ref, gt_ref, gf_ref, g_ref, o_ref, *, final):
    gt = gt_ref[...]
    moe = (gt[:, 0:1] * y0_ref[...].astype(F32) + gt[:, 1:2] * y1_ref[...].astype(F32)
           + gt[:, 2:3] * y2_ref[...].astype(F32) + gt[:, 3:4] * y3_ref[...].astype(F32))
    x_new = x_ref[...] + gf_ref[0] * moe
    if final:
        x_new = _norm_mod(x_new, g_ref[...], None, None)
    o_ref[...] = x_new


def _combine(x, y_slots, gates, gate_f, g_final, seq, final):
    t, d = x.shape
    tpb = seq // ROW_TILE
    nt = t // ROW_TILE
    assert TOP_K == 4
    return pl.pallas_call(
        functools.partial(_combine_kernel, final=final),
        out_shape=jax.ShapeDtypeStruct((t, d), F32),
        grid=(nt,),
        in_specs=[pl.BlockSpec((ROW_TILE, d), lambda i: (i, 0))] + [
            pl.BlockSpec((ROW_TILE, d), functools.partial(lambda i, k: (k * nt + i, 0), k=k))
            for k in range(TOP_K)] + [
            pl.BlockSpec((ROW_TILE, 128), lambda i: (i, 0)),
            pl.BlockSpec((1, 1, d), lambda i: (i // tpb, 0, 0)),
            pl.BlockSpec((1, d), lambda i: (0, 0)),
        ],
        out_specs=pl.BlockSpec((ROW_TILE, d), lambda i: (i, 0)),
        compiler_params=_params(("parallel",)),
        name="combine_final" if final else "combine",
    )(x, y_slots, y_slots, y_slots, y_slots, gates, gate_f, g_final.reshape(1, d))


def _moe(x, h, logits, gate_f, w_gu, b_gu, w_down, b_down, g_final, seq, layer, final):
    t, d = x.shape
    dest, tok_rows, gates, block_expert, n_used = _route(logits)
    ybuf = _experts(h, tok_rows, w_gu, b_gu, w_down, b_down, block_expert, n_used, layer)
    y_slots = jnp.take(ybuf, dest, axis=0, mode="clip")
    return _combine(x, y_slots, gates, gate_f, g_final, seq, final)


def kernel(x, c, ada_w, ada_b, norm_mix_g, norm_ffn_g, a_w_qkv, a_w_o, rel_bias, kv_norm_g, w_kvf, b_f,
           b_w_q, b_w_o, router_w, router_b, w_gu, b_gu, w_down, b_down, final_norm_g):
    n_batch, seq, d = x.shape
    t = n_batch * seq
    width = N_HEADS * HEAD_DIM
    xf = x.reshape(t, d)

    ada = _ada(c, ada_w, ada_b)
    mods = [[ada[l, :, i * d:(i + 1) * d].reshape(n_batch, 1, d) for i in range(6)] for l in range(2)]

    shift_m, scale_m, gate_m, shift_f, scale_f, gate_f = mods[0]
    qscale = np.ones((3, 3, 1), np.float32)
    qscale[:, 0] = HEAD_DIM ** -0.5
    w_qkv = (a_w_qkv[0].reshape(d, 3, 3, width) * qscale).astype(BF16)

    qi = np.arange(DIL_BLOCK, dtype=np.int32)[:, None]
    kj = np.arange(2 * DIL_BLOCK, dtype=np.int32)[None, :]
    delta = qi + DIL_BLOCK - kj
    in_band = (delta >= 0) & (delta <= DIL_BLOCK)
    outs, lses = [], []
    for g, (window, dilation) in enumerate(DIL_PAIRS):
        assert window // dilation == DIL_BLOCK
        qkv = _proj(xf, norm_mix_g[0], shift_m, scale_m, w_qkv[:, g].reshape(d, 3 * width), seq, dilation, g)
        bucket = _t5_bucket_np(np.clip(delta, 0, None) * dilation)
        tab = rel_bias[:, g * N_HEADS:(g + 1) * N_HEADS].astype(F32)
        onehot = jnp.asarray(np.eye(NUM_BUCKETS, dtype=np.float32)[bucket])
        bias = jnp.einsum('ijb,bh->hij', onehot, tab, precision=lax.Precision.HIGHEST)
        bias = jnp.where(in_band, bias, NEG)
        bias_first = jnp.where(kj >= DIL_BLOCK, bias, NEG)
        o, lse = _dilated_group(qkv, jnp.stack([bias_first, bias]), g, dilation, n_batch, seq)
        outs.append(o)
        lses.append(_unpermute_rows(lse, dilation)[:, :N_HEADS])
    lse_all = jnp.concatenate(lses, axis=1)
    x1, h1, logits1 = _mixer_tail((outs[0], outs[1], outs[2], lse_all), a_w_o[0].astype(BF16), xf, gate_m,
                                  norm_ffn_g[0], shift_f, scale_f, router_w[0], router_b[0], seq, True)
    x2 = _moe(x1, h1, logits1, gate_f, w_gu, b_gu, w_down, b_down, final_norm_g, seq, 0, False)

    w_f = jnp.pad(jnp.tile(w_kvf[:, 2 * width:], (1, 3)), ((0, 0), (0, 128 - 3 * N_HEADS))).astype(BF16)
    b_fp = jnp.pad(jnp.tile(b_f, 3), (0, 128 - 3 * N_HEADS)).reshape(1, 128)
    k_aug, v_sh, fcum = _kvf(x2, kv_norm_g, _aug_weight(w_kvf[:, :width]), w_kvf[:, width:2 * width].astype(BF16),
                             w_f, b_fp, n_batch)

    shift_m, scale_m, gate_m, shift_f, scale_f, gate_f = mods[1]
    q_aug = _qaug(x2, norm_mix_g[1], shift_m, scale_m, _aug_weight(b_w_q[0] * (HEAD_DIM ** -0.5 * LOG2E)), fcum, seq)
    o1 = _fox(q_aug, k_aug, v_sh, n_batch, seq)

    x3, h3, logits3 = _mixer_tail((o1,), b_w_o[0].astype(BF16), x2, gate_m, norm_ffn_g[1], shift_f, scale_f,
                                  router_w[1], router_b[1], seq, False)
    out = _moe(x3, h3, logits3, gate_f, w_gu, b_gu, w_down, b_down, final_norm_g, seq, 1, True)
    return out.reshape(n_batch, seq, d)
```

```python
import functools
import math

import numpy as np
import jax
import jax.numpy as jnp
from jax import lax
from jax.experimental import pallas as pl
from jax.experimental.pallas import tpu as pltpu

F32 = jnp.float32
BF16 = jnp.bfloat16

D_MODEL = 1024
HEAD_DIM = 64
N_HEADS = 16
DIL_PAIRS = ((128, 1), (512, 4), (2048, 16))
DIL_BLOCK = 128
NUM_BUCKETS = 32
MAX_DISTANCE = 2048
N_EXPERTS = 32
TOP_K = 4
SWIGLU_LIMIT = 7.0
SWIGLU_ALPHA = 1.702
RMS_EPS = 1e-6
NEG = -1e30
LOG2E = math.log2(math.e)

ROW_TILE = 512
MOE_TILE = 512
FOX_TILE = 512
AUG = 128
FOX_SUM_ROWS = 8
VMEM_LIMIT = 56 * 1024 * 1024


def _params(sem, vmem=VMEM_LIMIT):
    return pltpu.CompilerParams(dimension_semantics=sem, vmem_limit_bytes=vmem)


def _dot(a, b):
    return jnp.dot(a, b, preferred_element_type=F32)


def _dot_nt(a, b):
    return lax.dot_general(a, b, (((1,), (1,)), ((), ())), preferred_element_type=F32)


def _norm_mod(x, g, shift, scale):
    ms = jnp.mean(x * x, axis=-1, keepdims=True)
    y = x * lax.rsqrt(ms + RMS_EPS) * g
    if scale is not None:
        y = y * (1.0 + scale) + shift
    return y


def _ada_kernel(c_ref, w_ref, b_ref, o_ref):
    c = c_ref[...]
    act = c * (1.0 / (1.0 + jnp.exp(-c)))
    o_ref[0] = jnp.dot(act, w_ref[0], preferred_element_type=F32,
                       precision=lax.Precision.HIGHEST) + b_ref[0]


def _ada(c, ada_w, ada_b):
    depth, d, n = ada_w.shape
    bsz = c.shape[0]
    tn = 1536
    return pl.pallas_call(
        _ada_kernel,
        out_shape=jax.ShapeDtypeStruct((depth, bsz, n), F32),
        grid=(depth, n // tn),
        in_specs=[
            pl.BlockSpec((bsz, d), lambda l, j: (0, 0)),
            pl.BlockSpec((1, d, tn), lambda l, j: (l, 0, j)),
            pl.BlockSpec((1, 1, tn), lambda l, j: (l, 0, j)),
        ],
        out_specs=pl.BlockSpec((1, bsz, tn), lambda l, j: (l, 0, j)),
        compiler_params=_params(("parallel", "parallel")),
        name="ada",
    )(c, ada_w, ada_b.reshape(depth, 1, n))


def _perm_tile(dilation):
    return max(ROW_TILE, DIL_BLOCK * dilation)


def _proj_kernel(x_ref, g_ref, sh_ref, sc_ref, w_ref, o_ref, h_scr, *xs_scr, dilation, single_step):
    def fill():
        if dilation == 1:
            h_scr[...] = _norm_mod(x_ref[...], g_ref[...], sh_ref[0], sc_ref[0]).astype(BF16)
        else:
            (xs,) = xs_scr
            n_lane = xs.shape[0]
            for c in range(n_lane):
                xs[c] = x_ref[:, c * 128:(c + 1) * 128]
            chunk = x_ref.shape[0] // dilation
            for r in range(dilation):
                xr = jnp.concatenate([xs[c, pl.ds(r, chunk, stride=dilation), :] for c in range(n_lane)], axis=1)
                h_scr[r * chunk:(r + 1) * chunk, :] = _norm_mod(xr, g_ref[...], sh_ref[0], sc_ref[0]).astype(BF16)

    if single_step:
        fill()
    else:
        pl.when(pl.program_id(1) == 0)(fill)

    o_ref[...] = _dot(h_scr[...], w_ref[...]).astype(o_ref.dtype)


def _proj(x, g, shift, scale, w, seq, dilation, group):
    t, d = x.shape
    n = w.shape[1]
    tm = _perm_tile(dilation)
    single_step = tm == ROW_TILE
    tn = n if single_step else d
    tpb = seq // tm
    return pl.pallas_call(
        functools.partial(_proj_kernel, dilation=dilation, single_step=single_step),
        out_shape=jax.ShapeDtypeStruct((t, n), BF16),
        grid=(t // tm, n // tn),
        in_specs=[
            pl.BlockSpec((tm, d), lambda i, j: (i, 0)),
            pl.BlockSpec((1, d), lambda i, j: (0, 0)),
            pl.BlockSpec((1, 1, d), lambda i, j: (i // tpb, 0, 0)),
            pl.BlockSpec((1, 1, d), lambda i, j: (i // tpb, 0, 0)),
            pl.BlockSpec((d, tn), lambda i, j: (0, j)),
        ],
        out_specs=pl.BlockSpec((tm, tn), lambda i, j: (i, j)),
        scratch_shapes=[pltpu.VMEM((tm, d), BF16)] + (
            [pltpu.VMEM((d // 128, tm, 128), F32)] if dilation > 1 else []),
        compiler_params=_params(("parallel", "arbitrary")),
        name=f"proj{group}",
    )(x, g.reshape(1, d), shift, scale, w)


def _split3(f):
    def top(v):
        return lax.bitcast_convert_type(lax.bitcast_convert_type(v, jnp.uint32) & jnp.uint32(0xFFFF0000), F32)
    hi = top(f)
    r1 = f - hi
    mid = top(r1)
    return hi, mid, r1 - mid


def _forget_pieces(fc):
    hi, mid, lo = _split3(fc)
    lane = lax.broadcasted_iota(jnp.int32, fc.shape, 1)
    x = jnp.where(lane < N_HEADS, hi, jnp.where(lane < 2 * N_HEADS, mid, jnp.where(lane < 3 * N_HEADS, lo, 0.0)))
    return x.astype(BF16)


def _aug_tables(key_side):
    place = np.zeros((128, N_HEADS * AUG), np.float32)
    const = np.zeros((1, N_HEADS * AUG), np.float32)
    for h in range(N_HEADS):
        base = h * AUG + HEAD_DIM
        for piece in range(3):
            if key_side:
                place[piece * N_HEADS + h, base + piece] = 1.0
                const[0, base + 3 + piece] = 1.0
            else:
                place[piece * N_HEADS + h, base + 3 + piece] = 1.0
                const[0, base + piece] = -1.0
    return jnp.asarray(place, BF16), jnp.asarray(const, F32)


def _aug_weight(w):
    d = w.shape[0]
    w = jnp.pad(w.reshape(d, N_HEADS, HEAD_DIM), ((0, 0), (0, 0), (0, AUG - HEAD_DIM)))
    return w.reshape(d, N_HEADS * AUG).astype(BF16)


def _kvf_kernel(x_ref, g_ref, wk_ref, wv_ref, wf_ref, bf_ref, tri_ref, pl_ref, cr_ref,
                k_ref, v_ref, f_ref, carry, *, tpb):
    h = _norm_mod(x_ref[...], g_ref[...], None, None).astype(BF16)
    v_ref[...] = _dot(h, wv_ref[...]).astype(v_ref.dtype)
    z = _dot(h, wf_ref[...]) + bf_ref[...]
    lf = jnp.minimum(z, 0.0) - jnp.log(1.0 + jnp.exp(-jnp.abs(z)))
    hi, mid, lo = _split3(lf)
    tri = tri_ref[...]
    cs = _dot(tri, hi.astype(BF16)) + _dot(tri, mid.astype(BF16)) + _dot(tri, lo.astype(BF16))

    @pl.when(pl.program_id(0) % tpb == 0)
    def _():
        carry[...] = jnp.zeros_like(carry)

    cs = cs + carry[...]
    carry[...] = cs[ROW_TILE - 1:ROW_TILE, :]
    f2 = cs * LOG2E
    f_ref[...] = f2
    k_ref[...] = (_dot(h, wk_ref[...]) + _dot(_forget_pieces(f2), pl_ref[...]) + cr_ref[...]).astype(k_ref.dtype)


def _kvf(x, g, w_k, w_v, w_f, b_f, n_batch):
    t, d = x.shape
    n_aug = N_HEADS * AUG
    tri = jnp.asarray(np.tril(np.ones((ROW_TILE, ROW_TILE), np.float32)), BF16)
    place, const = _aug_tables(True)
    assert (t // ROW_TILE) % n_batch == 0
    full = lambda i: (0, 0)
    row = lambda i: (i, 0)
    return pl.pallas_call(
        functools.partial(_kvf_kernel, tpb=t // ROW_TILE // n_batch),
        out_shape=(jax.ShapeDtypeStruct((t, n_aug), BF16), jax.ShapeDtypeStruct((t, d), BF16),
                   jax.ShapeDtypeStruct((t, 128), F32)),
        grid=(t // ROW_TILE,),
        in_specs=[
            pl.BlockSpec((ROW_TILE, d), row),
            pl.BlockSpec((1, d), full),
            pl.BlockSpec((d, n_aug), full),
            pl.BlockSpec((d, d), full),
            pl.BlockSpec((d, 128), full),
            pl.BlockSpec((1, 128), full),
            pl.BlockSpec((ROW_TILE, ROW_TILE), full),
            pl.BlockSpec((128, n_aug), full),
            pl.BlockSpec((1, n_aug), full),
        ],
        out_specs=(pl.BlockSpec((ROW_TILE, n_aug), row), pl.BlockSpec((ROW_TILE, d), row),
                   pl.BlockSpec((ROW_TILE, 128), row)),
        scratch_shapes=[pltpu.VMEM((1, 128), F32)],
        compiler_params=_params(("arbitrary",)),
        name="kvf",
    )(x, g.reshape(1, d), w_k, w_v, w_f, b_f, tri, place, const)


def _qaug_kernel(x_ref, g_ref, sh_ref, sc_ref, w_ref, f_ref, pl_ref, cr_ref, o_ref):
    h = _norm_mod(x_ref[...], g_ref[...], sh_ref[0], sc_ref[0]).astype(BF16)
    o_ref[...] = (_dot(h, w_ref[...]) + _dot(_forget_pieces(f_ref[...]), pl_ref[...])
                  + cr_ref[...]).astype(o_ref.dtype)


def _qaug(x, g, shift, scale, w, fcum, seq):
    t, d = x.shape
    n_aug = N_HEADS * AUG
    tpb = seq // ROW_TILE
    place, const = _aug_tables(False)
    full = lambda i: (0, 0)
    row = lambda i: (i, 0)
    per_b = lambda i: (i // tpb, 0, 0)
    return pl.pallas_call(
        _qaug_kernel,
        out_shape=jax.ShapeDtypeStruct((t, n_aug), BF16),
        grid=(t // ROW_TILE,),
        in_specs=[
            pl.BlockSpec((ROW_TILE, d), row),
            pl.BlockSpec((1, d), full),
            pl.BlockSpec((1, 1, d), per_b),
            pl.BlockSpec((1, 1, d), per_b),
            pl.BlockSpec((d, n_aug), full),
            pl.BlockSpec((ROW_TILE, 128), row),
            pl.BlockSpec((128, n_aug), full),
            pl.BlockSpec((1, n_aug), full),
        ],
        out_specs=pl.BlockSpec((ROW_TILE, n_aug), row),
        compiler_params=_params(("parallel",)),
        name="qaug",
    )(x, g.reshape(1, d), shift, scale, w, fcum, place, const)


def _dil_kernel(q_ref, kp_ref, kc_ref, vp_ref, vc_ref, bias_ref, o_ref, lse_ref):
    table = jnp.minimum(pl.program_id(2), 1)
    pair_w = 2 * HEAD_DIM
    lane = lax.broadcasted_iota(jnp.int32, (DIL_BLOCK, pair_w), 1)
    first = lane < HEAD_DIM
    first2 = lax.broadcasted_iota(jnp.int32, (2 * DIL_BLOCK, pair_w), 1) < HEAD_DIM
    lse_tile = jnp.zeros((DIL_BLOCK, pair_w), F32)
    zero = jnp.zeros((), BF16)
    n_pairs = N_HEADS // 2
    slices = [slice(pair * pair_w, (pair + 1) * pair_w) for pair in range(n_pairs)]
    scores = []
    for pair, sl in enumerate(slices):
        q = q_ref[:, sl]
        k2 = jnp.concatenate([kp_ref[:, sl], kc_ref[:, sl]], axis=0)
        for which in range(2):
            qh = jnp.where(first, q, zero) if which == 0 else jnp.where(first, zero, q)
            scores.append(_dot_nt(qh, k2) + bias_ref[table, 2 * pair + which])
    probs, invs = [], []
    for h, s in enumerate(scores):
        m = jnp.max(s, axis=-1, keepdims=True)
        p = jnp.exp(s - m)
        den = jnp.sum(p, axis=-1, keepdims=True)
        probs.append(p.astype(BF16))
        invs.append(1.0 / den)
        lse_tile = jnp.where(lane == h, m + jnp.log(den), lse_tile)
    for pair, sl in enumerate(slices):
        v2 = jnp.concatenate([vp_ref[:, sl], vc_ref[:, sl]], axis=0)
        vcat = jnp.concatenate([jnp.where(first2, v2, zero), jnp.where(first2, zero, v2)], axis=0)
        o = _dot(jnp.concatenate(probs[2 * pair:2 * pair + 2], axis=1), vcat)
        o_ref[:, sl] = (o * jnp.where(first, invs[2 * pair], invs[2 * pair + 1])).astype(o_ref.dtype)
    lse_ref[...] = lse_tile


def _dilated_group(qkv, bias, group, dilation, n_batch, seq):
    t = qkv.shape[0]
    width = N_HEADS * HEAD_DIM
    nb = seq // dilation // DIL_BLOCK
    bpb = seq // DIL_BLOCK

    def blk(b, r, n):
        return b * bpb + n * dilation + r

    def spec(part, prev):
        if prev:
            return pl.BlockSpec((DIL_BLOCK, width), lambda b, r, n: (blk(b, r, jnp.maximum(n - 1, 0)), part))
        return pl.BlockSpec((DIL_BLOCK, width), lambda b, r, n: (blk(b, r, n), part))

    return pl.pallas_call(
        _dil_kernel,
        out_shape=(jax.ShapeDtypeStruct((t, width), BF16), jax.ShapeDtypeStruct((t, 2 * HEAD_DIM), F32)),
        grid=(n_batch, dilation, nb),
        in_specs=[spec(0, False), spec(1, True), spec(1, False), spec(2, True), spec(2, False),
                  pl.BlockSpec((2, N_HEADS, DIL_BLOCK, 2 * DIL_BLOCK), lambda b, r, n: (0, 0, 0, 0))],
        out_specs=(pl.BlockSpec((DIL_BLOCK, width), lambda b, r, n: (blk(b, r, n), 0)),
                   pl.BlockSpec((DIL_BLOCK, 2 * HEAD_DIM), lambda b, r, n: (blk(b, r, n), 0))),
        compiler_params=_params(("parallel", "parallel", "arbitrary")),
        name=f"dilated{group}",
    )(qkv, qkv, qkv, qkv, qkv, bias)


def _unpermute_rows(a, dilation):
    if dilation == 1:
        return a
    t, c = a.shape
    tm = _perm_tile(dilation)
    return a.reshape(t // tm, dilation, tm // dilation, c).transpose(0, 2, 1, 3).reshape(t, c)


def _perm_matrix(rows, dilation):
    per = rows // dilation
    p = np.zeros((rows, rows), np.float32)
    for r in range(dilation):
        for n in range(per):
            p[n * dilation + r, r * per + n] = 1.0
    return jnp.asarray(p, BF16)


def _t5_bucket_np(n):
    max_exact = NUM_BUCKETS // 2
    nf = np.maximum(n, 1).astype(np.float32)
    large = max_exact + (np.log(nf / np.float32(max_exact)) / np.float32(math.log(MAX_DISTANCE / max_exact))
                         * np.float32(NUM_BUCKETS - max_exact)).astype(np.int32)
    large = np.minimum(large, NUM_BUCKETS - 1)
    return np.where(n < max_exact, n, large)


def _mix_tail(mix_in, wo_ref, x_ref, gm_ref, g_ref, sh_ref, sc_ref, wr_ref, br_ref,
              x_out, h_out, lg_out):
    mix = _dot(mix_in, wo_ref[...])
    x_new = x_ref[...] + gm_ref[0] * mix
    x_out[...] = x_new
    h = _norm_mod(x_new, g_ref[...], sh_ref[0], sc_ref[0])
    bits = pltpu.bitcast(h.astype(BF16).astype(F32), jnp.uint32)
    half = h.shape[1] // 2
    h_out[...] = (bits[:, :half] >> 16) | (bits[:, half:] & jnp.uint32(0xFFFF0000))
    h_hi, h_mid, _ = _split3(h)
    h_hi = h_hi.astype(BF16)
    lg_out[...] = (_dot(h_hi, wr_ref[0]) + (_dot(h_mid.astype(BF16), wr_ref[0]) + _dot(h_hi, wr_ref[1]))
                   + br_ref[...])


def _merge_oproj_kernel(o0_ref, o1_ref, o2_ref, p1_ref, p2_ref, lse_ref, ex_ref, wo_ref, x_ref, gm_ref, g_ref,
                        sh_ref, sc_ref, wr_ref, br_ref, x_out, h_out, lg_out):
    lse = lse_ref[...]
    l0, l1, l2 = lse[:, 0:16], lse[:, 16:32], lse[:, 32:48]
    m = jnp.maximum(jnp.maximum(l0, l1), l2)
    e0, e1, e2 = jnp.exp(l0 - m), jnp.exp(l1 - m), jnp.exp(l2 - m)
    inv = 1.0 / (e0 + e1 + e2)
    ex = ex_ref[...]
    o1 = _dot(p1_ref[...], o1_ref[...])
    o2 = _dot(p2_ref[...], o2_ref[0].reshape(ROW_TILE, o2_ref.shape[-1]))
    merged = (_dot((e0 * inv).astype(BF16), ex) * o0_ref[...].astype(F32)
              + _dot((e1 * inv).astype(BF16), ex) * o1
              + _dot((e2 * inv).astype(BF16), ex) * o2)
    _mix_tail(merged.astype(BF16), wo_ref, x_ref, gm_ref, g_ref, sh_ref, sc_ref, wr_ref, br_ref,
              x_out, h_out, lg_out)


def _oproj_kernel(o_ref, wo_ref, x_ref, gm_ref, g_ref, sh_ref, sc_ref, wr_ref, br_ref,
                  x_out, h_out, lg_out):
    _mix_tail(o_ref[...], wo_ref, x_ref, gm_ref, g_ref, sh_ref, sc_ref, wr_ref, br_ref,
              x_out, h_out, lg_out)


def _mixer_tail(mix_inputs, w_o, x, gate_m, g_ffn, shift_f, scale_f, w_r, b_r, seq, merged):
    t, d = x.shape
    tpb = seq // ROW_TILE
    row = lambda i: (i, 0)
    full = lambda i: (0, 0)
    per_b = lambda i: (i // tpb, 0, 0)
    tail_specs = [
        pl.BlockSpec((d, d), full),
        pl.BlockSpec((ROW_TILE, d), row),
        pl.BlockSpec((1, 1, d), per_b),
        pl.BlockSpec((1, d), full),
        pl.BlockSpec((1, 1, d), per_b),
        pl.BlockSpec((1, 1, d), per_b),
        pl.BlockSpec((2, d, N_EXPERTS), lambda i: (0, 0, 0)),
        pl.BlockSpec((1, N_EXPERTS), full),
    ]
    w_hi, w_mid, _ = _split3(w_r)
    tail_args = (w_o, x, gate_m, g_ffn.reshape(1, d), shift_f, scale_f, jnp.stack([w_hi, w_mid]).astype(BF16),
                 b_r.reshape(1, N_EXPERTS))
    if merged:
        o0, o1, o2, lse = mix_inputs
        expand = jnp.asarray(np.kron(np.eye(N_HEADS, dtype=np.float32),
                                     np.ones((1, HEAD_DIM), np.float32)), BF16)
        d1, d2 = DIL_PAIRS[1][1], DIL_PAIRS[2][1]
        assert _perm_tile(d1) == ROW_TILE
        t2 = _perm_tile(d2)
        sub2 = ROW_TILE // d2
        o2v = o2.reshape(t // t2, d2, t2 // d2, d)
        per2 = t2 // ROW_TILE
        kern = _merge_oproj_kernel
        specs = [pl.BlockSpec((ROW_TILE, d), row), pl.BlockSpec((ROW_TILE, d), row),
                 pl.BlockSpec((1, d2, sub2, d), lambda i: (i // per2, 0, i % per2, 0)),
                 pl.BlockSpec((ROW_TILE, ROW_TILE), full), pl.BlockSpec((ROW_TILE, ROW_TILE), full),
                 pl.BlockSpec((ROW_TILE, 3 * N_HEADS), row), pl.BlockSpec((N_HEADS, d), full)]
        args = (o0, o1, o2v, _perm_matrix(ROW_TILE, d1), _perm_matrix(ROW_TILE, d2), lse, expand)
    else:
        kern = _oproj_kernel
        specs = [pl.BlockSpec((ROW_TILE, d), row)]
        args = mix_inputs
    return pl.pallas_call(
        kern,
        out_shape=(jax.ShapeDtypeStruct((t, d), F32), jax.ShapeDtypeStruct((t, d // 2), jnp.uint32),
                   jax.ShapeDtypeStruct((t, N_EXPERTS), F32)),
        grid=(t // ROW_TILE,),
        in_specs=specs + tail_specs,
        out_specs=(pl.BlockSpec((ROW_TILE, d), row), pl.BlockSpec((ROW_TILE, d // 2), row),
                   pl.BlockSpec((ROW_TILE, N_EXPERTS), row)),
        compiler_params=_params(("parallel",)),
        name="mixer_tail_merge" if merged else "mixer_tail",
    )(*args, *tail_args)


def _fox_kernel(q_ref, k_ref, v_ref, o_ref, vt_scr):
    qi = pl.program_id(2)
    seq = k_ref.shape[0]
    nk = seq // FOX_TILE
    half = HEAD_DIM

    ext = FOX_SUM_ROWS
    @pl.when(qi == 0)
    def _():
        lane = lax.broadcasted_iota(jnp.int32, (FOX_TILE, 2 * half), 1)
        erow = lax.broadcasted_iota(jnp.int32, (2 * ext, 2 * FOX_TILE), 0)
        ecol = lax.broadcasted_iota(jnp.int32, (2 * ext, 2 * FOX_TILE), 1)
        sum_rows = jnp.where((erow < ext) == (ecol < FOX_TILE), 1.0, 0.0).astype(BF16)
        for j in range(nk):
            vj = v_ref[j * FOX_TILE:(j + 1) * FOX_TILE, :].astype(F32)
            vt_scr[j, :2 * half, :FOX_TILE] = jnp.where(lane < half, vj, 0.0).T.astype(BF16)
            vt_scr[j, :2 * half, FOX_TILE:] = jnp.where(lane >= half, vj, 0.0).T.astype(BF16)
            vt_scr[j, 2 * half:, :] = sum_rows

    q0 = q_ref[:, :AUG]
    q1 = q_ref[:, AUG:]

    def softmax_step(s, m):
        m_new = jnp.maximum(m, jnp.max(s, axis=0, keepdims=True))
        return m_new, jnp.exp2(m - m_new), jnp.exp2((s - m_new).astype(BF16))

    def step(kj, carry, masked):
        m0, m1, acc = carry
        rows = pl.ds(pl.multiple_of(kj * FOX_TILE, FOX_TILE), FOX_TILE)
        s0 = _dot_nt(k_ref[rows, :AUG], q0)
        s1 = _dot_nt(k_ref[rows, AUG:], q1)
        if masked:
            kpos = lax.broadcasted_iota(jnp.int32, (FOX_TILE, FOX_TILE), 0)
            qpos = lax.broadcasted_iota(jnp.int32, (FOX_TILE, FOX_TILE), 1)
            keep = kpos <= qpos
            s0 = jnp.where(keep, s0, NEG)
            s1 = jnp.where(keep, s1, NEG)
        m0, a0, p0 = softmax_step(s0, m0)
        m1, a1, p1 = softmax_step(s1, m1)
        pv = _dot(vt_scr[kj], jnp.concatenate([p0, p1], axis=0))
        alpha = jnp.concatenate([jnp.broadcast_to(a0, (half, FOX_TILE)), jnp.broadcast_to(a1, (half, FOX_TILE)),
                                 jnp.broadcast_to(a0, (ext, FOX_TILE)), jnp.broadcast_to(a1, (ext, FOX_TILE))],
                                axis=0)
        return m0, m1, alpha * acc + pv

    neg = jnp.full((1, FOX_TILE), NEG, F32)
    init = (neg, neg, jnp.zeros((2 * half + 2 * ext, FOX_TILE), F32))
    carry = lax.fori_loop(0, qi, lambda kj, c: step(kj, c, False), init)
    _, _, acc = step(qi, carry, True)
    inv0 = 1.0 / acc[2 * half:2 * half + 1, :]
    inv1 = 1.0 / acc[2 * half + ext:2 * half + ext + 1, :]
    inv = jnp.concatenate([jnp.broadcast_to(inv0, (half, FOX_TILE)),
                           jnp.broadcast_to(inv1, (half, FOX_TILE))], axis=0)
    o_ref[...] = (acc[:2 * half, :] * inv).T.astype(o_ref.dtype)


def _fox(q_aug, k_aug, v, n_batch, seq):
    t = q_aug.shape[0]
    nq = seq // FOX_TILE
    pairs = N_HEADS // 2
    return pl.pallas_call(
        _fox_kernel,
        out_shape=jax.ShapeDtypeStruct((t, N_HEADS * HEAD_DIM), BF16),
        grid=(n_batch, pairs, nq),
        in_specs=[
            pl.BlockSpec((FOX_TILE, 2 * AUG), lambda b, p, qi: (b * nq + qi, p)),
            pl.BlockSpec((seq, 2 * AUG), lambda b, p, qi: (b, p)),
            pl.BlockSpec((seq, 2 * HEAD_DIM), lambda b, p, qi: (b, p)),
        ],
        out_specs=pl.BlockSpec((FOX_TILE, 2 * HEAD_DIM), lambda b, p, qi: (b * nq + qi, p)),
        scratch_shapes=[pltpu.VMEM((nq, 2 * HEAD_DIM + 2 * FOX_SUM_ROWS, 2 * FOX_TILE), BF16)],
        compiler_params=_params(("parallel", "parallel", "arbitrary")),
        name="fox",
    )(q_aug, k_aug, v)


def _expert_kernel(be_ref, nu_ref, tok_ref, h_hbm, wgu_ref, bgu_ref, wd_ref, bd_ref, o_ref,
                   wgu_bf, wd_bf, xbuf_a, xbuf_b, gsem):
    i = pl.program_id(0)
    d_ff = wd_ref.shape[1]
    odd = lax.rem(i, 2) == 1
    pairs = MOE_TILE // 2
    bufs = ((xbuf_a, gsem.at[0]), (xbuf_b, gsem.at[1]))

    def issue_gather(blk, buf, sem):
        base = blk * pairs
        for c in range(pairs):
            word = tok_ref[base + c]
            for r, tok in ((2 * c, word & 0xFFFF), (2 * c + 1, lax.shift_right_logical(word, 16))):
                pltpu.async_copy(h_hbm.at[pl.ds(tok, 1), :], buf.at[pl.ds(r, 1), :], sem, priority=r % 2)

    def wait_gather(buf, sem):
        pltpu.make_async_copy(h_hbm.at[pl.ds(0, MOE_TILE), :], buf, sem).wait()

    @pl.when(i == 0)
    def _():
        issue_gather(0, *bufs[0])

    @pl.when(jnp.logical_or(i == 0, be_ref[i] != be_ref[jnp.maximum(i - 1, 0)]))
    def _():
        wgu_bf[...] = wgu_ref[0].astype(BF16)
        wd_bf[...] = wd_ref[0].astype(BF16)

    def block(cur, nxt):
        @pl.when(i <= nu_ref[0])
        def _():
            wait_gather(*cur)

        @pl.when(i < nu_ref[0])
        def _():
            issue_gather(i + 1, *nxt)
            words = cur[0][...]
            x = jnp.concatenate([pltpu.bitcast(words << 16, F32),
                                 pltpu.bitcast(words & jnp.uint32(0xFFFF0000), F32)], axis=1).astype(BF16)
            gu = _dot(x, wgu_bf[...]) + bgu_ref[0]
            g = jnp.minimum(gu[:, :d_ff], SWIGLU_LIMIT)
            u = jnp.clip(gu[:, d_ff:], -SWIGLU_LIMIT, SWIGLU_LIMIT)
            act = (u + 1.0) * g * (1.0 / (1.0 + jnp.exp(-SWIGLU_ALPHA * g)))
            y = _dot(act.astype(BF16), wd_bf[...]) + bd_ref[0]
            o_ref[...] = y.astype(o_ref.dtype)

    @pl.when(jnp.logical_not(odd))
    def _():
        block(bufs[0], bufs[1])

    @pl.when(odd)
    def _():
        block(bufs[1], bufs[0])

    @pl.when(i >= nu_ref[0])
    def _():
        o_ref[...] = jnp.zeros_like(o_ref)


def _experts(h, tok_rows, w_gu, b_gu, w_down, b_down, block_expert, n_used, layer):
    t = h.shape[0]
    d = w_gu.shape[2]
    n_rows = tok_rows.shape[0]
    depth, e, _, n_gu = w_gu.shape
    d_ff = w_down.shape[2]
    n_blocks = n_rows // MOE_TILE
    assert t <= 65536
    packed = tok_rows[0::2] | (tok_rows[1::2] << 16)
    grid_spec = pltpu.PrefetchScalarGridSpec(
        num_scalar_prefetch=3,
        grid=(n_blocks,),
        in_specs=[
            pl.BlockSpec(memory_space=pl.ANY),
            pl.BlockSpec((None, 1, d, n_gu), lambda i, be, nu, tk: (layer, be[i], 0, 0)),
            pl.BlockSpec((None, 1, 1, n_gu), lambda i, be, nu, tk: (layer, be[i], 0, 0)),
            pl.BlockSpec((None, 1, d_ff, d), lambda i, be, nu, tk: (layer, be[i], 0, 0)),
            pl.BlockSpec((None, 1, 1, d), lambda i, be, nu, tk: (layer, be[i], 0, 0)),
        ],
        out_specs=pl.BlockSpec((MOE_TILE, d), lambda i, be, nu, tk: (i, 0)),
        scratch_shapes=[pltpu.VMEM((d, n_gu), BF16), pltpu.VMEM((d_ff, d), BF16),
                        pltpu.VMEM((MOE_TILE, d // 2), jnp.uint32), pltpu.VMEM((MOE_TILE, d // 2), jnp.uint32),
                        pltpu.SemaphoreType.DMA((2,))],
    )
    return pl.pallas_call(
        _expert_kernel,
        out_shape=jax.ShapeDtypeStruct((n_rows, d), BF16),
        grid_spec=grid_spec,
        compiler_params=_params(("arbitrary",)),
        name="experts",
    )(block_expert, n_used, packed, h, w_gu, b_gu.reshape(depth, e, 1, n_gu), w_down,
      b_down.reshape(depth, e, 1, d))


def _router_kernel(lg_ref, tri_ref, upper_ref, dest_ref, gate_ref, cnt_ref, carry, base):
    phase = pl.program_id(0)
    i = pl.program_id(1)
    tm = lg_ref.shape[0]
    lane = lax.broadcasted_iota(jnp.int32, (tm, N_EXPERTS), 1).astype(F32)
    wide = lax.broadcasted_iota(jnp.int32, (tm, 128), 1)

    work = lg_ref[...]
    hots, vals = [], []
    for _ in range(TOP_K):
        m = jnp.max(work, axis=1, keepdims=True)
        idx = jnp.min(jnp.where(work == m, lane, float(N_EXPERTS)), axis=1, keepdims=True)
        hot = lane == idx
        hots.append(hot)
        vals.append(m)
        work = jnp.where(hot, -jnp.inf, work)
    exps = [jnp.exp(v - vals[0]) for v in vals]
    inv = 1.0 / (exps[0] + exps[1] + exps[2] + exps[3])
    gate_tile = jnp.zeros((tm, 128), F32)
    for k in range(TOP_K):
        gate_tile = jnp.where(wide == k, exps[k] * inv, gate_tile)
    gate_ref[...] = gate_tile

    chosen = jnp.zeros((tm, N_EXPERTS), F32)
    for hot in hots:
        chosen = chosen + hot.astype(F32)

    @pl.when(jnp.logical_and(phase == 0, i == 0))
    def _():
        carry[...] = jnp.zeros_like(carry)
        base[...] = jnp.zeros_like(base)

    @pl.when(jnp.logical_and(phase == 1, i == 0))
    def _():
        counts = carry[...]
        padded = jnp.floor((counts + (MOE_TILE - 1)) * (1.0 / MOE_TILE)) * MOE_TILE
        base[...] = jnp.dot(jnp.broadcast_to(padded, (8, N_EXPERTS)), upper_ref[...],
                            preferred_element_type=F32, precision=lax.Precision.HIGHEST)[0:1, :]
        carry[...] = jnp.zeros_like(carry)

    ahead = _dot(tri_ref[...], chosen.astype(BF16))
    pos = ahead + carry[...] + base[...]
    dest_tile = jnp.zeros((tm, 128), F32)
    for k in range(TOP_K):
        row = jnp.sum(jnp.where(hots[k], pos, 0.0), axis=1, keepdims=True)
        dest_tile = jnp.where(wide == k, row, dest_tile)
    dest_ref[...] = dest_tile.astype(jnp.int32)
    carry[...] = carry[...] + jnp.sum(chosen, axis=0, keepdims=True)
    cnt_ref[...] = carry[...]


def _router(logits):
    t = logits.shape[0]
    nt = t // ROW_TILE
    tri = jnp.asarray(np.tril(np.ones((ROW_TILE, ROW_TILE), np.float32), -1), BF16)
    upper = jnp.asarray(np.triu(np.ones((N_EXPERTS, N_EXPERTS), np.float32), 1), F32)
    return pl.pallas_call(
        _router_kernel,
        out_shape=(jax.ShapeDtypeStruct((t + ROW_TILE, 128), jnp.int32),
                   jax.ShapeDtypeStruct((t + ROW_TILE, 128), F32),
                   jax.ShapeDtypeStruct((1, N_EXPERTS), F32)),
        grid=(2, nt),
        in_specs=[
            pl.BlockSpec((ROW_TILE, N_EXPERTS), lambda p, i: (i, 0)),
            pl.BlockSpec((ROW_TILE, ROW_TILE), lambda p, i: (0, 0)),
            pl.BlockSpec((N_EXPERTS, N_EXPERTS), lambda p, i: (0, 0)),
        ],
        out_specs=(pl.BlockSpec((ROW_TILE, 128), lambda p, i: (p * i + (1 - p) * nt, 0)),
                   pl.BlockSpec((ROW_TILE, 128), lambda p, i: (p * i + (1 - p) * nt, 0)),
                   pl.BlockSpec((1, N_EXPERTS), lambda p, i: (0, 0))),
        scratch_shapes=[pltpu.VMEM((1, N_EXPERTS), F32), pltpu.VMEM((1, N_EXPERTS), F32)],
        compiler_params=_params(("arbitrary", "arbitrary")),
        name="router",
    )(logits, tri, upper)


def _route(logits):
    t = logits.shape[0]
    n_slots = t * TOP_K
    n_rows = n_slots + N_EXPERTS * MOE_TILE
    n_blocks = n_rows // MOE_TILE
    dest_w, gates_w, counts = _router(logits)
    counts = counts.reshape(N_EXPERTS).astype(jnp.int32)
    pad_ends = jnp.cumsum((counts + MOE_TILE - 1) // MOE_TILE * MOE_TILE)
    block_start = jnp.arange(n_blocks, dtype=jnp.int32) * MOE_TILE
    block_expert = jnp.minimum(jnp.sum(block_start[:, None] >= pad_ends[None, :], axis=1),
                               N_EXPERTS - 1).astype(jnp.int32)
    n_used = (pad_ends[-1] // MOE_TILE).astype(jnp.int32).reshape(1)
    dest = dest_w[:t, :TOP_K].T.reshape(-1)
    tok = jnp.tile(jnp.arange(t, dtype=jnp.int32), TOP_K)
    _, tok_sorted = lax.sort((dest, tok), num_keys=1)
    padded = (counts + MOE_TILE - 1) // MOE_TILE * MOE_TILE
    pad_starts = pad_ends - padded
    starts = jnp.cumsum(counts) - counts
    rows = jnp.arange(n_rows, dtype=jnp.int32)
    filled_end = jnp.repeat((pad_starts + counts)[block_expert], MOE_TILE)
    compact = rows + jnp.repeat((starts - pad_starts)[block_expert], MOE_TILE)
    tok_rows = jnp.where(rows < filled_end, tok_sorted[jnp.clip(compact, 0, n_slots - 1)], rows % t)
    return dest, tok_rows, gates_w, block_expert, n_used


def _combine_kernel(x_ref, y0_ref, y1_ref, y2_ref, y3_ref, gt_ref, gf_ref, g_ref, o_ref, *, final):
    gt = gt_ref[...]
    moe = (gt[:, 0:1] * y0_ref[...].astype(F32) + gt[:, 1:2] * y1_ref[...].astype(F32)
           + gt[:, 2:3] * y2_ref[...].astype(F32) + gt[:, 3:4] * y3_ref[...].astype(F32))
    x_new = x_ref[...] + gf_ref[0] * moe
    if final:
        x_new = _norm_mod(x_new, g_ref[...], None, None)
    o_ref[...] = x_new


def _combine(x, y_slots, gates, gate_f, g_final, seq, final):
    t, d = x.shape
    tpb = seq // ROW_TILE
    nt = t // ROW_TILE
    assert TOP_K == 4
    return pl.pallas_call(
        functools.partial(_combine_kernel, final=final),
        out_shape=jax.ShapeDtypeStruct((t, d), F32),
        grid=(nt,),
        in_specs=[pl.BlockSpec((ROW_TILE, d), lambda i: (i, 0))] + [
            pl.BlockSpec((ROW_TILE, d), functools.partial(lambda i, k: (k * nt + i, 0), k=k))
            for k in range(TOP_K)] + [
            pl.BlockSpec((ROW_TILE, 128), lambda i: (i, 0)),
            pl.BlockSpec((1, 1, d), lambda i: (i // tpb, 0, 0)),
            pl.BlockSpec((1, d), lambda i: (0, 0)),
        ],
        out_specs=pl.BlockSpec((ROW_TILE, d), lambda i: (i, 0)),
        compiler_params=_params(("parallel",)),
        name="combine_final" if final else "combine",
    )(x, y_slots, y_slots, y_slots, y_slots, gates, gate_f, g_final.reshape(1, d))


def _moe(x, h, logits, gate_f, w_gu, b_gu, w_down, b_down, g_final, seq, layer, final):
    t, d = x.shape
    dest, tok_rows, gates, block_expert, n_used = _route(logits)
    ybuf = _experts(h, tok_rows, w_gu, b_gu, w_down, b_down, block_expert, n_used, layer)
    y_slots = ybuf.at[dest].get(mode="promise_in_bounds")
    return _combine(x, y_slots, gates, gate_f, g_final, seq, final)


def kernel(x, c, ada_w, ada_b, norm_mix_g, norm_ffn_g, a_w_qkv, a_w_o, rel_bias, kv_norm_g, w_kvf, b_f,
           b_w_q, b_w_o, router_w, router_b, w_gu, b_gu, w_down, b_down, final_norm_g):
    n_batch, seq, d = x.shape
    t = n_batch * seq
    width = N_HEADS * HEAD_DIM
    xf = x.reshape(t, d)

    ada = _ada(c, ada_w, ada_b)
    mods = [[ada[l, :, i * d:(i + 1) * d].reshape(n_batch, 1, d) for i in range(6)] for l in range(2)]

    shift_m, scale_m, gate_m, shift_f, scale_f, gate_f = mods[0]
    qscale = np.ones((3, 3, 1), np.float32)
    qscale[:, 0] = HEAD_DIM ** -0.5
    w_qkv = (a_w_qkv[0].reshape(d, 3, 3, width) * qscale).astype(BF16)

    qi = np.arange(DIL_BLOCK, dtype=np.int32)[:, None]
    kj = np.arange(2 * DIL_BLOCK, dtype=np.int32)[None, :]
    delta = qi + DIL_BLOCK - kj
    in_band = (delta >= 0) & (delta <= DIL_BLOCK)
    outs, lses = [], []
    for g, (window, dilation) in enumerate(DIL_PAIRS):
        assert window // dilation == DIL_BLOCK
        qkv = _proj(xf, norm_mix_g[0], shift_m, scale_m, w_qkv[:, g].reshape(d, 3 * width), seq, dilation, g)
        bucket = _t5_bucket_np(np.clip(delta, 0, None) * dilation)
        tab = rel_bias[:, g * N_HEADS:(g + 1) * N_HEADS].astype(F32)
        onehot = jnp.asarray(np.eye(NUM_BUCKETS, dtype=np.float32)[bucket])
        bias = jnp.einsum('ijb,bh->hij', onehot, tab, precision=lax.Precision.HIGHEST)
        bias = jnp.where(in_band, bias, NEG)
        bias_first = jnp.where(kj >= DIL_BLOCK, bias, NEG)
        o, lse = _dilated_group(qkv, jnp.stack([bias_first, bias]), g, dilation, n_batch, seq)
        outs.append(o)
        lses.append(_unpermute_rows(lse, dilation)[:, :N_HEADS])
    lse_all = jnp.concatenate(lses, axis=1)
    x1, h1, logits1 = _mixer_tail((outs[0], outs[1], outs[2], lse_all), a_w_o[0].astype(BF16), xf, gate_m,
                                  norm_ffn_g[0], shift_f, scale_f, router_w[0], router_b[0], seq, True)
    x2 = _moe(x1, h1, logits1, gate_f, w_gu, b_gu, w_down, b_down, final_norm_g, seq, 0, False)

    w_f = jnp.pad(jnp.tile(w_kvf[:, 2 * width:], (1, 3)), ((0, 0), (0, 128 - 3 * N_HEADS))).astype(BF16)
    b_fp = jnp.pad(jnp.tile(b_f, 3), (0, 128 - 3 * N_HEADS)).reshape(1, 128)
    k_aug, v_sh, fcum = _kvf(x2, kv_norm_g, _aug_weight(w_kvf[:, :width]), w_kvf[:, width:2 * width].astype(BF16),
                             w_f, b_fp, n_batch)

    shift_m, scale_m, gate_m, shift_f, scale_f, gate_f = mods[1]
    q_aug = _qaug(x2, norm_mix_g[1], shift_m, scale_m, _aug_weight(b_w_q[0] * (HEAD_DIM ** -0.5 * LOG2E)), fcum, seq)
    o1 = _fox(q_aug, k_aug, v_sh, n_batch, seq)

    x3, h3, logits3 = _mixer_tail((o1,), b_w_o[0].astype(BF16), x2, gate_m, norm_ffn_g[1], shift_f, scale_f,
                                  router_w[1], router_b[1], seq, False)
    out = _moe(x3, h3, logits3, gate_f, w_gu, b_gu, w_down, b_down, final_norm_g, seq, 1, True)
    return out.reshape(n_batch, seq, d)
```
